```python
import jax, jax.numpy as jnp
from jax import lax
import numpy as np

D_MODEL = 4096
BATCH = 4
SEQ = 2048
DEPTH = 1
DEC_BATCH = 128
DEC_SEQ = 8
PAST_LEN = 16384
PAGE_SIZE = 128

N_META = 16
W_A = D_MODEL // 2
K_A = 3
W_R = D_MODEL // 2
K_R = 4
LRU_HEADS = 16
LRU_BLOCK = W_R // LRU_HEADS
C_RG = 8.0
N_GROUPS = 4
EXPERTS_PER_GROUP = 8
N_EXPERTS = N_GROUPS * EXPERTS_PER_GROUP
TOP_K = 2
D_EXPERT = D_MODEL // 8
EPS = 1e-6
SPLIT_SIZES = (W_A, W_A, W_A, W_R, W_R, D_MODEL, D_MODEL)
N_IN = sum(SPLIT_SIZES)

kernel_name = "hybrid_shortconv_rglru_hmoe_step"


def rmsnorm(x, g):
    xf = x.astype(jnp.float32)
    y = xf * lax.rsqrt(jnp.mean(xf * xf, axis=-1, keepdims=True) + EPS)
    return (y * g.astype(jnp.float32)).astype(x.dtype)


def causal_dwconv(u, buf, w):
    k = w.shape[0]
    t = u.shape[1]
    full = jnp.concatenate([buf.astype(u.dtype), u], axis=1)
    out = full[:, 0:t] * w[0]
    for j in range(1, k):
        out = out + full[:, j:j + t] * w[j]
    return out, full[:, full.shape[1] - (k - 1):]


def block_diag(x, w, b):
    n, t, _ = x.shape
    xb = x.reshape(n, t, LRU_HEADS, LRU_BLOCK)
    return jnp.einsum('nthi,hij->nthj', xb, w).reshape(n, t, W_R) + b


def rg_lru(x, h0, w_a, b_a, w_x, b_x, lam):
    r = jax.nn.sigmoid(block_diag(x, w_a, b_a).astype(jnp.float32))
    i = jax.nn.sigmoid(block_diag(x, w_x, b_x).astype(jnp.float32))
    log_a = -C_RG * r * jax.nn.softplus(-lam.astype(jnp.float32))
    a = jnp.exp(log_a)
    bterm = jnp.sqrt(-jnp.expm1(2.0 * log_a)) * i * x.astype(jnp.float32)

    def step(h, ab):
        a_t, b_t = ab
        h = a_t * h + b_t
        return h, h

    h_last, hs = lax.scan(step, h0.astype(jnp.float32), (jnp.swapaxes(a, 0, 1), jnp.swapaxes(bterm, 0, 1)))
    return jnp.swapaxes(hs, 0, 1).astype(x.dtype), h_last.astype(h0.dtype)


def hier_moe(x, w_group, b_group, w_router, b_router, w_gate, w_up, w_down):
    n, t, d = x.shape
    xt = x.reshape(n * t, d)
    m = xt.shape[0]
    g_prob = jax.nn.softmax((xt @ w_group).astype(jnp.float32) + b_group.astype(jnp.float32), axis=-1)
    g_p, g_idx = lax.top_k(g_prob, 1)
    e_logits = ((xt @ w_router).astype(jnp.float32) + b_router.astype(jnp.float32)).reshape(m, N_GROUPS, EXPERTS_PER_GROUP)
    e_sel = jnp.take_along_axis(e_logits, g_idx[:, :, None], axis=1)[:, 0]
    e_p, e_idx = lax.top_k(jax.nn.softmax(e_sel, axis=-1), TOP_K)
    weights = e_p / jnp.sum(e_p, axis=-1, keepdims=True) * g_p
    expert_id = g_idx * EXPERTS_PER_GROUP + e_idx
    gates = jnp.sum(jax.nn.one_hot(expert_id, N_EXPERTS, dtype=jnp.float32) * weights[..., None], axis=1)
    gates = gates.astype(x.dtype)
    out = jnp.zeros_like(xt)
    for e in range(N_EXPERTS):
        h = jax.nn.silu(xt @ w_gate[e]) * (xt @ w_up[e])
        out = out + gates[:, e:e + 1] * (h @ w_down[e])
    return out.reshape(n, t, d)


def trunk_layer(x, buf_a, buf_r, h0, norm1, w_in, conv_a_w, conv_r_w, conv_r_b, lru_wa, lru_ba, lru_wx, lru_bx,
                lru_lam, w_br_a, w_br_r, w_o, norm2, w_group, b_group, w_router, b_router, w_gate, w_up, w_down):
    u = rmsnorm(x, norm1)
    z = u @ w_in
    idx = list(np.cumsum(SPLIT_SIZES)[:-1])
    b_a, c_a, v_a, x_r, y_r, g_a, g_r = jnp.split(z, idx, axis=-1)
    conv_a_out, new_buf_a = causal_dwconv(c_a * v_a, buf_a, conv_a_w)
    out_a = b_a * conv_a_out
    xr_c, new_buf_r = causal_dwconv(x_r, buf_r, conv_r_w)
    xr_c = xr_c + conv_r_b
    h_seq, h_last = rg_lru(xr_c, h0, lru_wa, lru_ba, lru_wx, lru_bx, lru_lam)
    out_r = h_seq * jax.nn.gelu(y_r)
    merged = jax.nn.sigmoid(g_a) * (out_a @ w_br_a) + jax.nn.sigmoid(g_r) * (out_r @ w_br_r)
    x = x + merged @ w_o
    x = x + hier_moe(rmsnorm(x, norm2), w_group, b_group, w_router, b_router, w_gate, w_up, w_down)
    return x, new_buf_a, new_buf_r, h_last


def setup_inputs(seed: int = 0) -> dict:
    key = jax.random.key(seed)
    ks = jax.random.split(key, 32)
    f32 = jnp.float32
    nrm = lambda k, shape, s: jax.random.normal(k, shape, f32) * s
    u = jax.random.uniform(ks[13], (DEPTH, W_R), f32, 0.9, 0.999)
    s = u ** (1.0 / C_RG)
    lam = jnp.log(s / (1.0 - s))
    return {
        "x_prompt": nrm(ks[0], (BATCH, SEQ, D_MODEL), 1.0),
        "x_sample": nrm(ks[1], (DEC_BATCH, DEC_SEQ, D_MODEL), 1.0),
        "state_conv_a": nrm(ks[2], (DEPTH, DEC_BATCH, K_A - 1, W_A), 1.0),
        "state_conv_r": nrm(ks[3], (DEPTH, DEC_BATCH, K_R - 1, W_R), 1.0),
        "state_h": nrm(ks[4], (DEPTH, DEC_BATCH, W_R), 0.5),
        "meta_tokens": nrm(ks[5], (N_META, D_MODEL), 1.0),
        "norm1": 1.0 + nrm(ks[6], (DEPTH, D_MODEL), 0.02),
        "w_in": nrm(ks[7], (DEPTH, D_MODEL, N_IN), D_MODEL ** -0.5),
        "conv_a_w": nrm(ks[8], (DEPTH, K_A, W_A), K_A ** -0.5),
        "conv_r_w": nrm(ks[9], (DEPTH, K_R, W_R), K_R ** -0.5),
        "conv_r_b": nrm(ks[10], (DEPTH, W_R), 0.02),
        "lru_wa": nrm(ks[11], (DEPTH, LRU_HEADS, LRU_BLOCK, LRU_BLOCK), LRU_BLOCK ** -0.5),
        "lru_ba": nrm(ks[12], (DEPTH, W_R), 0.02),
        "lru_wx": nrm(ks[14], (DEPTH, LRU_HEADS, LRU_BLOCK, LRU_BLOCK), LRU_BLOCK ** -0.5),
        "lru_bx": nrm(ks[15], (DEPTH, W_R), 0.02),
        "lru_lam": lam,
        "w_br_a": nrm(ks[16], (DEPTH, W_A, D_MODEL), W_A ** -0.5),
        "w_br_r": nrm(ks[17], (DEPTH, W_R, D_MODEL), W_R ** -0.5),
        "w_o": nrm(ks[18], (DEPTH, D_MODEL, D_MODEL), D_MODEL ** -0.5),
        "norm2": 1.0 + nrm(ks[19], (DEPTH, D_MODEL), 0.02),
        "w_group": nrm(ks[20], (DEPTH, D_MODEL, N_GROUPS), D_MODEL ** -0.5),
        "b_group": nrm(ks[21], (DEPTH, N_GROUPS), 0.01),
        "w_router": nrm(ks[22], (DEPTH, D_MODEL, N_EXPERTS), D_MODEL ** -0.5),
        "b_router": nrm(ks[23], (DEPTH, N_EXPERTS), 0.01),
        "w_gate": nrm(ks[24], (DEPTH, N_EXPERTS, D_MODEL, D_EXPERT), D_MODEL ** -0.5),
        "w_up": nrm(ks[25], (DEPTH, N_EXPERTS, D_MODEL, D_EXPERT), D_MODEL ** -0.5),
        "w_down": nrm(ks[26], (DEPTH, N_EXPERTS, D_EXPERT, D_MODEL), D_EXPERT ** -0.5),
        "norm_f": 1.0 + nrm(ks[27], (D_MODEL,), 0.02),
    }


def reference(x_prompt, x_sample, state_conv_a, state_conv_r, state_h, meta_tokens, norm1, w_in, conv_a_w, conv_r_w,
              conv_r_b, lru_wa, lru_ba, lru_wx, lru_bx, lru_lam, w_br_a, w_br_r, w_o, norm2, w_group, b_group,
              w_router, b_router, w_gate, w_up, w_down, norm_f):
    bp = x_prompt.shape[0]
    meta = jnp.broadcast_to(meta_tokens[None].astype(x_prompt.dtype), (bp, N_META, D_MODEL))
    xp = jnp.concatenate([meta, x_prompt], axis=1)
    xs = x_sample
    zero_a = jnp.zeros((bp, K_A - 1, W_A), x_prompt.dtype)
    zero_r = jnp.zeros((bp, K_R - 1, W_R), x_prompt.dtype)
    zero_h = jnp.zeros((bp, W_R), state_h.dtype)
    pa, pr, ph, sa, sr, sh = [], [], [], [], [], []
    for l in range(DEPTH):
        lw = (norm1[l], w_in[l], conv_a_w[l], conv_r_w[l], conv_r_b[l], lru_wa[l], lru_ba[l], lru_wx[l], lru_bx[l],
              lru_lam[l], w_br_a[l], w_br_r[l], w_o[l], norm2[l], w_group[l], b_group[l], w_router[l], b_router[l],
              w_gate[l], w_up[l], w_down[l])
        xp, ba, br, hl = trunk_layer(xp, zero_a, zero_r, zero_h, *lw)
        pa.append(ba); pr.append(br); ph.append(hl)
        xs, ba, br, hl = trunk_layer(xs, state_conv_a[l], state_conv_r[l], state_h[l], *lw)
        sa.append(ba); sr.append(br); sh.append(hl)
    y_prompt = rmsnorm(xp, norm_f)[:, N_META:]
    y_sample = rmsnorm(xs, norm_f)
    return (y_prompt, y_sample, jnp.stack(pa), jnp.stack(pr), jnp.stack(ph), jnp.stack(sa), jnp.stack(sr), jnp.stack(sh))
```

```python
import functools

import jax
import jax.numpy as jnp
from jax import lax
from jax.experimental import pallas as pl
from jax.experimental.pallas import tpu as pltpu

F32, BF16, I32 = jnp.float32, jnp.bfloat16, jnp.int32

EPS = 1e-6
C_RG = 8.0
K_A = 3
K_R = 4
LRU_HEADS = 16
TOP_K = 2

SUBLANES = 8
LANES = 128
VMEM_LIMIT_BYTES = 56 * 1024 * 1024

TM = 512
CB = 256
T_EXP = 256
TR = 256


def _dot(a, b):
    return jnp.dot(a, b, preferred_element_type=F32)


def _rmsnorm_rows(x, g):
    y = x * lax.rsqrt(jnp.mean(x * x, axis=-1, keepdims=True) + EPS)
    return y * g


def _params(n_axes):
    return pltpu.CompilerParams(dimension_semantics=("arbitrary",) * n_axes,
                                vmem_limit_bytes=VMEM_LIMIT_BYTES)


def _norm1_kernel(xp_ref, xs_ref, meta_ref, g_ref, u_ref, um_ref, *, n_prompt_tiles):
    i = pl.program_id(0)
    g = g_ref[...]

    @pl.when(i < n_prompt_tiles)
    def _():
        u_ref[...] = _rmsnorm_rows(xp_ref[...], g).astype(BF16)

    @pl.when(i >= n_prompt_tiles)
    def _():
        u_ref[...] = _rmsnorm_rows(xs_ref[...], g).astype(BF16)

    @pl.when(i == 0)
    def _():
        um_ref[...] = _rmsnorm_rows(meta_ref[...], g).astype(BF16)


def _norm1(xp, xs, meta, g, n_prompt_tiles, n_tiles):
    d = xp.shape[1]
    return pl.pallas_call(
        functools.partial(_norm1_kernel, n_prompt_tiles=n_prompt_tiles),
        grid=(n_tiles,),
        in_specs=[
            pl.BlockSpec((TM, d), lambda i: (jnp.minimum(i, n_prompt_tiles - 1), 0)),
            pl.BlockSpec((TM, d), lambda i: (jnp.maximum(i - n_prompt_tiles, 0), 0)),
            pl.BlockSpec(meta.shape, lambda i: (0, 0)),
            pl.BlockSpec((1, d), lambda i: (0, 0)),
        ],
        out_specs=[
            pl.BlockSpec((TM, d), lambda i: (i, 0)),
            pl.BlockSpec(meta.shape, lambda i: (0, 0)),
        ],
        out_shape=[
            jax.ShapeDtypeStruct((n_tiles * TM, d), BF16),
            jax.ShapeDtypeStruct(meta.shape, BF16),
        ],
        compiler_params=_params(1),
        name="norm1",
    )(xp, xs, meta, g)


def _conv_taps(window, cw, k_width):
    acc = window(0) * cw[0:1, :]
    for k in range(1, k_width):
        acc = acc + window(k) * cw[k:k + 1, :]
    return acc


def _conv_flat(ext_ref, values, header_ref, cw, k_width, rows):
    ext_ref[0:SUBLANES, :] = header_ref[...]
    ext_ref[SUBLANES:SUBLANES + rows, :] = values
    base = SUBLANES - (k_width - 1)
    return _conv_taps(lambda k: ext_ref[pl.ds(base + k, rows), :], cw, k_width)


def _conv_grouped(ext3_ref, values, header, cw, k_width, rows):
    n_seq = rows // SUBLANES
    cols = values.shape[1]
    ext3_ref[:, 0:SUBLANES, :] = header.reshape(n_seq, SUBLANES, cols)
    ext3_ref[:, SUBLANES:, :] = values.reshape(n_seq, SUBLANES, cols)
    base = SUBLANES - (k_width - 1)
    return _conv_taps(
        lambda k: ext3_ref[:, base + k:base + k + SUBLANES, :].reshape(rows, cols), cw, k_width)


def _mixer_a_kernel(u_ref, um_ref, wb_ref, wc_ref, wv_ref, cw_ref, hdr_ref,
                    oa_ref, tailp_ref, tails_ref,
                    wbf_ref, ext_ref, ext3_ref, carry_ref, mhdr_ref,
                    *, tiles_per_seq, n_prompt_tiles, n_meta):
    m = pl.program_id(1)
    cw = cw_ref[...]

    @pl.when(m == 0)
    def _():
        wbf_ref[0] = wb_ref[...].astype(BF16)
        wbf_ref[1] = wc_ref[...].astype(BF16)
        wbf_ref[2] = wv_ref[...].astype(BF16)
        um = um_ref[...]
        cv_meta = _dot(um, wbf_ref[1]) * _dot(um, wbf_ref[2])
        mhdr_ref[...] = cv_meta[n_meta - SUBLANES:, :]

    u = u_ref[...]
    zb = _dot(u, wbf_ref[0])
    cv = _dot(u, wbf_ref[1]) * _dot(u, wbf_ref[2])

    @pl.when(m < n_prompt_tiles)
    def _():
        @pl.when((m % tiles_per_seq) == 0)
        def _():
            carry_ref[...] = mhdr_ref[...]

        conv = _conv_flat(ext_ref, cv, carry_ref, cw, K_A, TM)
        oa_ref[...] = (zb * conv).astype(BF16)
        carry_ref[...] = ext_ref[TM:, :]
        tailp_ref[0] = ext_ref[TM:, :]

    @pl.when(m >= n_prompt_tiles)
    def _():
        conv = _conv_grouped(ext3_ref, cv, hdr_ref[...], cw, K_A, TM)
        oa_ref[...] = (zb * conv).astype(BF16)
        tails_ref[...] = cv


def _mixer_a(u, um, w_in, conv_w, hdr_s, *, width, n_seq, tiles_per_seq, n_prompt_tiles, n_tiles):
    d = u.shape[1]
    nb = width // CB
    n_meta = um.shape[0]
    last_seq = n_seq - 1

    def wspec(seg):
        return pl.BlockSpec((d, CB), lambda j, m, seg=seg: (0, seg * nb + j))

    return pl.pallas_call(
        functools.partial(_mixer_a_kernel, tiles_per_seq=tiles_per_seq,
                          n_prompt_tiles=n_prompt_tiles, n_meta=n_meta),
        grid=(nb, n_tiles),
        in_specs=[
            pl.BlockSpec((TM, d), lambda j, m: (m, 0)),
            pl.BlockSpec(um.shape, lambda j, m: (0, 0)),
            wspec(0), wspec(1), wspec(2),
            pl.BlockSpec((K_A, CB), lambda j, m: (0, j)),
            pl.BlockSpec((TM, CB), lambda j, m: (jnp.maximum(m - n_prompt_tiles, 0), j)),
        ],
        out_specs=[
            pl.BlockSpec((TM, CB), lambda j, m: (m, j)),
            pl.BlockSpec((1, SUBLANES, CB),
                         lambda j, m: (jnp.minimum(m // tiles_per_seq, last_seq), 0, j)),
            pl.BlockSpec((TM, CB), lambda j, m: (jnp.maximum(m - n_prompt_tiles, 0), j)),
        ],
        out_shape=[
            jax.ShapeDtypeStruct((n_tiles * TM, width), BF16),
            jax.ShapeDtypeStruct((n_seq, SUBLANES, width), F32),
            jax.ShapeDtypeStruct(hdr_s.shape, F32),
        ],
        scratch_shapes=[
            pltpu.VMEM((3, d, CB), BF16),
            pltpu.VMEM((TM + SUBLANES, CB), F32),
            pltpu.VMEM((TM // SUBLANES, 2 * SUBLANES, CB), F32),
            pltpu.VMEM((SUBLANES, CB), F32),
            pltpu.VMEM((SUBLANES, CB), F32),
        ],
        compiler_params=_params(2),
        name="mixer_a",
    )(u, um, w_in, w_in, w_in, conv_w, hdr_s)


def _softplus(x):
    return jnp.maximum(x, 0.0) + jnp.log1p(jnp.exp(-jnp.abs(x)))


def _lru_kernel(u_ref, um_ref, wx_ref, wy_ref, cw_ref, cbias_ref, wa_ref, ba_ref, wi_ref, bi_ref, lam_ref,
                hdr_ref, h0_ref,
                or_ref, tailp_ref, tails_ref, hp_ref, hs_ref,
                wbf_ref, gbf_ref, ext_ref, ext3_ref, a_ref, b_ref, h_ref, carry_ref, hcarry_ref,
                mhdr_ref, mh_ref,
                *, tiles_per_seq, n_prompt_tiles, n_meta, lru_block):
    m = pl.program_id(1)
    cw = cw_ref[...]
    cbias = cbias_ref[...]
    heads = CB // lru_block
    row8 = lax.broadcasted_iota(I32, (SUBLANES, CB), 0)

    def gate_terms(xc):
        xb = xc.astype(BF16)
        ra, ri = [], []
        for hh in range(heads):
            xh = xb[:, hh * lru_block:(hh + 1) * lru_block]
            ra.append(_dot(xh, gbf_ref[0, hh]))
            ri.append(_dot(xh, gbf_ref[1, hh]))
        r = jax.nn.sigmoid(jnp.concatenate(ra, axis=-1) + ba_ref[...])
        i = jax.nn.sigmoid(jnp.concatenate(ri, axis=-1) + bi_ref[...])
        log_a = -C_RG * r * _softplus(-lam_ref[...])
        a = jnp.exp(log_a)
        b = jnp.sqrt((1.0 + a * a) * jnp.tanh(-log_a)) * i * xc
        return a, b

    def scan(rows, h_init, per_group_h0):
        def body(g, hp):
            r0 = pl.multiple_of(g * SUBLANES, SUBLANES)
            a8 = a_ref[pl.ds(r0, SUBLANES), :]
            b8 = b_ref[pl.ds(r0, SUBLANES), :]
            for dist in (1, 2, 4):
                keep = row8 >= dist
                b8 = jnp.where(keep, a8 * pltpu.roll(b8, dist, 0) + b8, b8)
                a8 = jnp.where(keep, a8 * pltpu.roll(a8, dist, 0), a8)
            h_in = h0_ref[pl.ds(g, 1), :] if per_group_h0 else hp
            h8 = a8 * h_in + b8
            h_ref[pl.ds(r0, SUBLANES), :] = h8
            return h8[SUBLANES - 1:SUBLANES, :]
        return lax.fori_loop(0, rows // SUBLANES, body, h_init, unroll=2)

    @pl.when(m == 0)
    def _():
        wbf_ref[0] = wx_ref[...].astype(BF16)
        wbf_ref[1] = wy_ref[...].astype(BF16)
        gbf_ref[0] = wa_ref[...].astype(BF16)
        gbf_ref[1] = wi_ref[...].astype(BF16)
        zx_meta = _dot(um_ref[...], wbf_ref[0])
        carry_ref[...] = jnp.zeros((SUBLANES, CB), F32)
        xc = _conv_flat(ext_ref, zx_meta, carry_ref, cw, K_R, n_meta) + cbias
        a, b = gate_terms(xc)
        a_ref[0:n_meta, :] = a
        b_ref[0:n_meta, :] = b
        h_meta = scan(n_meta, jnp.zeros((1, CB), F32), False)
        mh_ref[...] = jnp.broadcast_to(h_meta, (SUBLANES, CB))
        mhdr_ref[...] = zx_meta[n_meta - SUBLANES:, :]

    u = u_ref[...]
    zx = _dot(u, wbf_ref[0])
    zy = _dot(u, wbf_ref[1])

    @pl.when(m < n_prompt_tiles)
    def _():
        @pl.when((m % tiles_per_seq) == 0)
        def _():
            carry_ref[...] = mhdr_ref[...]
            hcarry_ref[...] = mh_ref[...]

        xc = _conv_flat(ext_ref, zx, carry_ref, cw, K_R, TM) + cbias
        a, b = gate_terms(xc)
        a_ref[...] = a
        b_ref[...] = b
        hcarry_ref[0:1, :] = scan(TM, hcarry_ref[0:1, :], False)
        or_ref[...] = (h_ref[...] * jax.nn.gelu(zy)).astype(BF16)
        carry_ref[...] = ext_ref[TM:, :]
        tailp_ref[0] = ext_ref[TM:, :]
        hp_ref[0] = h_ref[TM - SUBLANES:, :]

    @pl.when(m >= n_prompt_tiles)
    def _():
        xc = _conv_grouped(ext3_ref, zx, hdr_ref[...], cw, K_R, TM) + cbias
        a, b = gate_terms(xc)
        a_ref[...] = a
        b_ref[...] = b
        scan(TM, jnp.zeros((1, CB), F32), True)
        h = h_ref[...]
        or_ref[...] = (h * jax.nn.gelu(zy)).astype(BF16)
        tails_ref[...] = zx
        hs_ref[...] = h


def _mixer_r(u, um, w_in, conv_w, conv_b, lru_wa, lru_ba, lru_wx, lru_bx, lru_lam, hdr_s, h0_s,
             *, width, seg0, n_seq, tiles_per_seq, n_prompt_tiles, n_tiles):
    d = u.shape[1]
    nb = width // CB
    n_meta = um.shape[0]
    last_seq = n_seq - 1
    lru_block = lru_wa.shape[-1]
    heads = CB // lru_block
    seqs_per_tile = TM // SUBLANES

    def wspec(seg):
        return pl.BlockSpec((d, CB), lambda j, m, seg=seg: (0, seg * nb + j))

    def vec():
        return pl.BlockSpec((1, CB), lambda j, m: (0, j))

    def gspec():
        return pl.BlockSpec((heads, lru_block, lru_block), lambda j, m: (j, 0, 0))

    def sample_rows():
        return pl.BlockSpec((TM, CB), lambda j, m: (jnp.maximum(m - n_prompt_tiles, 0), j))

    def seq_tail():
        return pl.BlockSpec((1, SUBLANES, CB),
                            lambda j, m: (jnp.minimum(m // tiles_per_seq, last_seq), 0, j))

    return pl.pallas_call(
        functools.partial(_lru_kernel, tiles_per_seq=tiles_per_seq, n_prompt_tiles=n_prompt_tiles,
                          n_meta=n_meta, lru_block=lru_block),
        grid=(nb, n_tiles),
        in_specs=[
            pl.BlockSpec((TM, d), lambda j, m: (m, 0)),
            pl.BlockSpec(um.shape, lambda j, m: (0, 0)),
            wspec(seg0), wspec(seg0 + 1),
            pl.BlockSpec((K_R, CB), lambda j, m: (0, j)),
            vec(),
            gspec(), vec(), gspec(), vec(), vec(),
            sample_rows(),
            pl.BlockSpec((seqs_per_tile, CB), lambda j, m: (jnp.maximum(m - n_prompt_tiles, 0), j)),
        ],
        out_specs=[
            pl.BlockSpec((TM, CB), lambda j, m: (m, j)),
            seq_tail(), sample_rows(), seq_tail(), sample_rows(),
        ],
        out_shape=[
            jax.ShapeDtypeStruct((n_tiles * TM, width), BF16),
            jax.ShapeDtypeStruct((n_seq, SUBLANES, width), F32),
            jax.ShapeDtypeStruct(hdr_s.shape, F32),
            jax.ShapeDtypeStruct((n_seq, SUBLANES, width), F32),
            jax.ShapeDtypeStruct(hdr_s.shape, F32),
        ],
        scratch_shapes=[
            pltpu.VMEM((2, d, CB), BF16),
            pltpu.VMEM((2, heads, lru_block, lru_block), BF16),
            pltpu.VMEM((TM + SUBLANES, CB), F32),
            pltpu.VMEM((seqs_per_tile, 2 * SUBLANES, CB), F32),
            pltpu.VMEM((TM, CB), F32),
            pltpu.VMEM((TM, CB), F32),
            pltpu.VMEM((TM, CB), F32),
            pltpu.VMEM((SUBLANES, CB), F32),
            pltpu.VMEM((SUBLANES, CB), F32),
            pltpu.VMEM((SUBLANES, CB), F32),
            pltpu.VMEM((SUBLANES, CB), F32),
        ],
        compiler_params=_params(2),
        name="mixer_r",
    )(u, um, w_in, w_in, conv_w, conv_b, lru_wa, lru_ba, lru_wx, lru_bx, lru_lam, hdr_s, h0_s)


def _merge_kernel(u_ref, oa_ref, or_ref, wga_ref, wgr_ref, wba_ref, wbr_ref, mg_ref, wg_bf, wb_bf):
    @pl.when(pl.program_id(1) == 0)
    def _():
        wg_bf[0] = wga_ref[...].astype(BF16)
        wg_bf[1] = wgr_ref[...].astype(BF16)
        wb_bf[0] = wba_ref[...].astype(BF16)
        wb_bf[1] = wbr_ref[...].astype(BF16)

    u = u_ref[...]
    ga = jax.nn.sigmoid(_dot(u, wg_bf[0]))
    gr = jax.nn.sigmoid(_dot(u, wg_bf[1]))
    mg = ga * _dot(oa_ref[...], wb_bf[0]) + gr * _dot(or_ref[...], wb_bf[1])
    mg_ref[...] = mg.astype(BF16)


def _merge(u, out_a, out_r, w_in, w_br_a, w_br_r, *, seg_cols, n_tiles):
    d = u.shape[1]
    width = out_a.shape[1]
    nb = d // CB
    seg_a = seg_cols // CB

    return pl.pallas_call(
        _merge_kernel,
        grid=(nb, n_tiles),
        in_specs=[
            pl.BlockSpec((TM, d), lambda j, m: (m, 0)),
            pl.BlockSpec((TM, width), lambda j, m: (m, 0)),
            pl.BlockSpec((TM, width), lambda j, m: (m, 0)),
            pl.BlockSpec((d, CB), lambda j, m: (0, seg_a + j)),
            pl.BlockSpec((d, CB), lambda j, m: (0, seg_a + nb + j)),
            pl.BlockSpec((width, CB), lambda j, m: (0, j)),
            pl.BlockSpec((width, CB), lambda j, m: (0, j)),
        ],
        out_specs=pl.BlockSpec((TM, CB), lambda j, m: (m, j)),
        out_shape=jax.ShapeDtypeStruct((n_tiles * TM, d), BF16),
        scratch_shapes=[pltpu.VMEM((2, d, CB), BF16), pltpu.VMEM((2, width, CB), BF16)],
        compiler_params=_params(2),
        name="merge",
    )(u, out_a, out_r, w_in, w_in, w_br_a, w_br_r)


def _oproj_kernel(mg_ref, xp_ref, xs_ref, wo_ref, x1_ref, wbf_ref, *, n_prompt_tiles):
    m = pl.program_id(1)

    @pl.when(m == 0)
    def _():
        wbf_ref[...] = wo_ref[...].astype(BF16)

    y = _dot(mg_ref[...], wbf_ref[...])

    @pl.when(m < n_prompt_tiles)
    def _():
        x1_ref[...] = xp_ref[...] + y

    @pl.when(m >= n_prompt_tiles)
    def _():
        x1_ref[...] = xs_ref[...] + y


def _oproj(mg, xp, xs, w_o, *, n_prompt_tiles, n_tiles):
    d = mg.shape[1]
    cb = 2 * CB
    nb = d // cb
    return pl.pallas_call(
        functools.partial(_oproj_kernel, n_prompt_tiles=n_prompt_tiles),
        grid=(nb, n_tiles),
        in_specs=[
            pl.BlockSpec((TM, d), lambda j, m: (m, 0)),
            pl.BlockSpec((TM, cb), lambda j, m: (jnp.minimum(m, n_prompt_tiles - 1), j)),
            pl.BlockSpec((TM, cb), lambda j, m: (jnp.maximum(m - n_prompt_tiles, 0), j)),
            pl.BlockSpec((d, cb), lambda j, m: (0, j)),
        ],
        out_specs=pl.BlockSpec((TM, cb), lambda j, m: (m, j)),
        out_shape=jax.ShapeDtypeStruct((n_tiles * TM, d), F32),
        scratch_shapes=[pltpu.VMEM((d, cb), BF16)],
        compiler_params=_params(2),
        name="oproj",
    )(mg, xp, xs, w_o)


def _router_kernel(x1_ref, g_ref, wt_ref, bt_ref, xn_ref, eid_ref, wts_ref, *, n_groups):
    xn = _rmsnorm_rows(x1_ref[...], g_ref[...])
    xn_ref[...] = xn
    lg = lax.dot_general(wt_ref[...].astype(BF16), xn.astype(BF16), (((1,), (1,)), ((), ())),
                         preferred_element_type=F32) + bt_ref[...]
    rows = xn.shape[0]
    row8 = lax.broadcasted_iota(I32, (SUBLANES, rows), 0).astype(F32)

    def first_index_of(v, vmax):
        return jnp.min(jnp.where(v == vmax, row8, float(SUBLANES)), axis=0, keepdims=True)

    gl = jnp.where(row8 < float(n_groups), lg[0:SUBLANES, :], -jnp.inf)
    ge = jnp.exp(gl - jnp.max(gl, axis=0, keepdims=True))
    gprob = ge / jnp.sum(ge, axis=0, keepdims=True)
    g_p = jnp.max(gprob, axis=0, keepdims=True)
    g_idx = first_index_of(gprob, g_p)

    esel = jnp.zeros((SUBLANES, rows), F32)
    for k in range(n_groups):
        esel = jnp.where(g_idx == float(k), lg[(k + 1) * SUBLANES:(k + 2) * SUBLANES, :], esel)
    ee = jnp.exp(esel - jnp.max(esel, axis=0, keepdims=True))
    ep = ee / jnp.sum(ee, axis=0, keepdims=True)
    p1 = jnp.max(ep, axis=0, keepdims=True)
    i1 = first_index_of(ep, p1)
    ep_rest = jnp.where(row8 == i1, -1.0, ep)
    p2 = jnp.max(ep_rest, axis=0, keepdims=True)
    i2 = first_index_of(ep_rest, p2)

    den = p1 + p2
    wts_ref[0:1, :] = p1 / den * g_p
    wts_ref[1:2, :] = p2 / den * g_p
    eid_ref[0:1, :] = (g_idx * float(SUBLANES) + i1).astype(I32)
    eid_ref[1:2, :] = (g_idx * float(SUBLANES) + i2).astype(I32)


def _router(x1, g2, wt, bt, *, n_groups):
    m_rows, d = x1.shape
    return pl.pallas_call(
        functools.partial(_router_kernel, n_groups=n_groups),
        grid=(m_rows // TR,),
        in_specs=[
            pl.BlockSpec((TR, d), lambda i: (i, 0)),
            pl.BlockSpec((1, d), lambda i: (0, 0)),
            pl.BlockSpec(wt.shape, lambda i: (0, 0)),
            pl.BlockSpec(bt.shape, lambda i: (0, 0)),
        ],
        out_specs=[
            pl.BlockSpec((TR, d), lambda i: (i, 0)),
            pl.BlockSpec((TOP_K, TR), lambda i: (0, i)),
            pl.BlockSpec((TOP_K, TR), lambda i: (0, i)),
        ],
        out_shape=[
            jax.ShapeDtypeStruct((m_rows, d), F32),
            jax.ShapeDtypeStruct((TOP_K, m_rows), I32),
            jax.ShapeDtypeStruct((TOP_K, m_rows), F32),
        ],
        compiler_params=_params(1),
        name="router",
    )(x1, g2, wt, bt)


INFO_TILE_EXPERT, INFO_NUM_TILES, INFO_VALID_END, INFO_ROWS = 0, 1, 2, SUBLANES


def _positions_kernel(eid_ref, pos_ref, info_ref, exc_ref, *, n_experts, tile_shift):
    m_rows = eid_ref.shape[1]
    chunk = 2 * LANES
    n_chunks = m_rows // chunk
    tile = 1 << tile_shift
    e_col = lax.broadcasted_iota(I32, (n_experts, chunk), 0)
    upper = (lax.broadcasted_iota(I32, (chunk, chunk), 0)
             < lax.broadcasted_iota(I32, (chunk, chunk), 1)).astype(BF16)

    def onehots(c):
        off = pl.multiple_of(c * chunk, chunk)
        ids = eid_ref[:, pl.ds(off, chunk)]
        return off, [(e_col == ids[k:k + 1, :]) for k in range(TOP_K)]

    def count(c, carry):
        off, hot = onehots(c)
        used = jnp.where(hot[0] | hot[1], 1.0, 0.0)
        exc_ref[:, pl.ds(off, chunk)] = _dot(used.astype(BF16), upper) + carry
        return carry + jnp.sum(used, axis=1, keepdims=True)

    counts = lax.fori_loop(0, n_chunks, count, jnp.zeros((n_experts, 1), F32))

    lane = lax.broadcasted_iota(I32, (n_experts, LANES), 1)
    sub = lax.broadcasted_iota(I32, (n_experts, LANES), 0)

    def to_lanes(col):
        return jnp.sum(jnp.where(lane == sub, col, 0.0), axis=0, keepdims=True)

    n_tiles_e = ((counts.astype(I32) + (tile - 1)) >> tile_shift).astype(F32)
    first_tile = jnp.sum(jnp.where(lane < sub, to_lanes(n_tiles_e), 0.0), axis=1, keepdims=True)
    end_tile = first_tile + n_tiles_e
    first_row = first_tile * float(tile)

    def place(c, _):
        off, hot = onehots(c)
        dest = exc_ref[:, pl.ds(off, chunk)] + first_row
        for k in range(TOP_K):
            pos_ref[k:k + 1, pl.ds(off, chunk)] = jnp.sum(
                jnp.where(hot[k], dest, 0.0), axis=0, keepdims=True).astype(I32)
        return 0

    lax.fori_loop(0, n_chunks, place, 0)

    tile_expert = jnp.sum(jnp.where(end_tile <= lane.astype(F32), 1.0, 0.0), axis=0, keepdims=True)
    num_tiles = jnp.max(end_tile, axis=0, keepdims=True)
    valid_end = to_lanes(first_row + counts)
    info_ref[...] = jnp.zeros(info_ref.shape, I32)
    info_ref[INFO_TILE_EXPERT:INFO_TILE_EXPERT + 1, :] = jnp.minimum(
        tile_expert, float(n_experts - 1)).astype(I32)
    info_ref[INFO_NUM_TILES:INFO_NUM_TILES + 1, :] = jnp.broadcast_to(num_tiles, (1, LANES)).astype(I32)
    info_ref[INFO_VALID_END:INFO_VALID_END + 1, :] = valid_end.astype(I32)


def _positions(eid, *, n_experts):
    m_rows = eid.shape[1]
    tile_shift = T_EXP.bit_length() - 1
    return pl.pallas_call(
        functools.partial(_positions_kernel, n_experts=n_experts, tile_shift=tile_shift),
        out_shape=[
            jax.ShapeDtypeStruct((TOP_K, m_rows), I32),
            jax.ShapeDtypeStruct((INFO_ROWS, LANES), I32),
        ],
        scratch_shapes=[pltpu.VMEM((n_experts, m_rows), F32)],
        compiler_params=pltpu.CompilerParams(vmem_limit_bytes=VMEM_LIMIT_BYTES),
        name="positions",
    )(eid)


def _scatter_kernel(pos_ref, info_ref, x_ref, xg_ref, zero_ref, sem, zsem, *, n_experts, n_tiles, m_rows):
    i = pl.program_id(0)

    def row_copy(src_row, dst_row):
        return pltpu.make_async_copy(x_ref.at[pl.ds(src_row, 1), :], xg_ref.at[pl.ds(dst_row, 1), :], sem)

    def zero_copy(dst_row, n):
        return pltpu.make_async_copy(zero_ref.at[pl.ds(0, n), :], xg_ref.at[pl.ds(dst_row, n), :], zsem)

    def zero_pieces():
        half = T_EXP // 2
        for e in range(n_experts):
            row = info_ref[INFO_VALID_END * LANES + e]
            pad = (T_EXP - (row & (T_EXP - 1))) & (T_EXP - 1)
            singles = pad & (SUBLANES - 1)
            for t in range(SUBLANES - 1):
                yield t < singles, row + t, 1
            row = row + singles
            for bit in range(SUBLANES.bit_length() - 1, T_EXP.bit_length() - 1):
                n = 1 << bit
                yield (pad & n) != 0, pl.multiple_of(row, SUBLANES), n
                row = row + (pad & n)
        for t in range(n_tiles):
            unused = t >= info_ref[INFO_NUM_TILES * LANES]
            yield unused, t * T_EXP, half
            yield unused, t * T_EXP + half, half

    @pl.when(i == 0)
    def _():
        zero_ref[...] = jnp.zeros(zero_ref.shape, F32)
        for needed, row, n in zero_pieces():
            @pl.when(needed)
            def _():
                zero_copy(row, n).start()

    base = i * TR

    def start(r, _):
        for k in range(TOP_K):
            row_copy(r, pos_ref[k * m_rows + base + r]).start()
        return 0

    def wait(r, _):
        for k in range(TOP_K):
            row_copy(0, 0).wait()
        return 0

    lax.fori_loop(0, TR, start, 0, unroll=8)
    lax.fori_loop(0, TR, wait, 0, unroll=8)

    @pl.when(i == 0)
    def _():
        for needed, row, n in zero_pieces():
            @pl.when(needed)
            def _():
                zero_copy(row, n).wait()


def _scatter_rows(pos_flat, info_flat, xn, *, n_experts, n_tiles):
    m_rows, d = xn.shape
    p_rows = n_tiles * T_EXP
    return pl.pallas_call(
        functools.partial(_scatter_kernel, n_experts=n_experts, n_tiles=n_tiles, m_rows=m_rows),
        grid_spec=pltpu.PrefetchScalarGridSpec(
            num_scalar_prefetch=2,
            grid=(m_rows // TR,),
            in_specs=[pl.BlockSpec((TR, d), lambda i, pos, info: (i, 0))],
            out_specs=pl.BlockSpec(memory_space=pl.ANY),
            scratch_shapes=[pltpu.VMEM((T_EXP // 2, d), F32), pltpu.SemaphoreType.DMA(()),
                            pltpu.SemaphoreType.DMA(())],
        ),
        out_shape=jax.ShapeDtypeStruct((p_rows, d), F32),
        compiler_params=_params(1),
        name="scatter_rows",
    )(pos_flat, info_flat, xn)


def _tile_index(i, info_ref):
    return jnp.minimum(i, info_ref[INFO_NUM_TILES * LANES] - 1)


def _tile_expert(i, info_ref):
    return info_ref[INFO_TILE_EXPERT * LANES + _tile_index(i, info_ref)]


def _expert_changed(i, info_ref):
    prev = info_ref[INFO_TILE_EXPERT * LANES + jnp.maximum(i - 1, 0)]
    return jnp.logical_or(i == 0, info_ref[INFO_TILE_EXPERT * LANES + i] != prev)


def _expert_up_kernel(info_ref, x_ref, wg_ref, wu_ref, h_ref, wbf_ref):
    i = pl.program_id(0)

    @pl.when(i < info_ref[INFO_NUM_TILES * LANES])
    def _():
        @pl.when(_expert_changed(i, info_ref))
        def _():
            wbf_ref[0] = wg_ref[0].astype(BF16)
            wbf_ref[1] = wu_ref[0].astype(BF16)

        x = x_ref[...].astype(BF16)
        h = jax.nn.silu(_dot(x, wbf_ref[0])) * _dot(x, wbf_ref[1])
        h_ref[...] = h.astype(BF16)

    @pl.when(i >= info_ref[INFO_NUM_TILES * LANES])
    def _():
        h_ref[...] = jnp.zeros(h_ref.shape, BF16)


def _expert_up(info_flat, xg, w_gate, w_up, *, n_tiles):
    p_rows, d = xg.shape
    d_exp = w_gate.shape[-1]

    def wspec():
        return pl.BlockSpec((1, d, d_exp), lambda i, info: (_tile_expert(i, info), 0, 0))

    return pl.pallas_call(
        _expert_up_kernel,
        grid_spec=pltpu.PrefetchScalarGridSpec(
            num_scalar_prefetch=1,
            grid=(n_tiles,),
            in_specs=[pl.BlockSpec((T_EXP, d), lambda i, info: (_tile_index(i, info), 0)), wspec(), wspec()],
            out_specs=pl.BlockSpec((T_EXP, d_exp), lambda i, info: (i, 0)),
            scratch_shapes=[pltpu.VMEM((2, d, d_exp), BF16)],
        ),
        out_shape=jax.ShapeDtypeStruct((p_rows, d_exp), BF16),
        compiler_params=_params(1),
        name="expert_up",
    )(info_flat, xg, w_gate, w_up)


def _expert_down_kernel(info_ref, h_ref, wd_ref, y_ref, wbf_ref):
    i = pl.program_id(0)

    @pl.when(i < info_ref[INFO_NUM_TILES * LANES])
    def _():
        @pl.when(_expert_changed(i, info_ref))
        def _():
            wbf_ref[...] = wd_ref[0].astype(BF16)

        y_ref[...] = _dot(h_ref[...], wbf_ref[...])

    @pl.when(i >= info_ref[INFO_NUM_TILES * LANES])
    def _():
        y_ref[...] = jnp.zeros(y_ref.shape, F32)


def _expert_down(info_flat, h, w_down, *, n_tiles):
    p_rows, d_exp = h.shape
    d = w_down.shape[-1]
    return pl.pallas_call(
        _expert_down_kernel,
        grid_spec=pltpu.PrefetchScalarGridSpec(
            num_scalar_prefetch=1,
            grid=(n_tiles,),
            in_specs=[
                pl.BlockSpec((T_EXP, d_exp), lambda i, info: (_tile_index(i, info), 0)),
                pl.BlockSpec((1, d_exp, d), lambda i, info: (_tile_expert(i, info), 0, 0)),
            ],
            out_specs=pl.BlockSpec((T_EXP, d), lambda i, info: (i, 0)),
            scratch_shapes=[pltpu.VMEM((d_exp, d), BF16)],
        ),
        out_shape=jax.ShapeDtypeStruct((p_rows, d), F32),
        compiler_params=_params(1),
        name="expert_down",
    )(info_flat, h, w_down)


def _combine_kernel(pos_ref, x1_ref, w_ref, g_ref, y_ref, out_ref, ybuf_ref, sem, *, m_rows, row0):
    base = row0 + pl.program_id(0) * TR

    def row_copy(k, r):
        return pltpu.make_async_copy(y_ref.at[pl.ds(pos_ref[k * m_rows + base + r], 1), :],
                                     ybuf_ref.at[k, pl.ds(r, 1), :], sem)

    def start(r, _):
        for k in range(TOP_K):
            row_copy(k, r).start()
        return 0

    def wait(r, _):
        for k in range(TOP_K):
            pltpu.make_async_copy(y_ref.at[pl.ds(0, 1), :], ybuf_ref.at[k, pl.ds(r, 1), :], sem).wait()
        return 0

    lax.fori_loop(0, TR, start, 0, unroll=8)
    lax.fori_loop(0, TR, wait, 0, unroll=8)

    w = w_ref[...]
    moe = w[:, 0:1] * ybuf_ref[0] + w[:, 1:2] * ybuf_ref[1]
    out_ref[...] = _rmsnorm_rows(x1_ref[...] + moe, g_ref[...])


def _combine(pos_flat, x1, wts_t, norm_f, y, *, row0, n_rows):
    m_rows, d = x1.shape
    tile0 = row0 // TR
    return pl.pallas_call(
        functools.partial(_combine_kernel, m_rows=m_rows, row0=row0),
        grid_spec=pltpu.PrefetchScalarGridSpec(
            num_scalar_prefetch=1,
            grid=(n_rows // TR,),
            in_specs=[
                pl.BlockSpec((TR, d), lambda i, pos: (tile0 + i, 0)),
                pl.BlockSpec((TR, TOP_K), lambda i, pos: (tile0 + i, 0)),
                pl.BlockSpec((1, d), lambda i, pos: (0, 0)),
                pl.BlockSpec(memory_space=pl.ANY),
            ],
            out_specs=pl.BlockSpec((TR, d), lambda i, pos: (i, 0)),
            scratch_shapes=[pltpu.VMEM((TOP_K, TR, d), F32), pltpu.SemaphoreType.DMA(())],
        ),
        out_shape=jax.ShapeDtypeStruct((n_rows, d), F32),
        compiler_params=_params(1),
        name="combine",
    )(pos_flat, x1, wts_t, norm_f, y)


def _sample_headers(state, k_width):
    n, _, w = state.shape
    padded = jnp.pad(state, ((0, 0), (SUBLANES - (k_width - 1), 0), (0, 0)))
    return padded.reshape(n * SUBLANES, w)


def kernel(x_prompt, x_sample, state_conv_a, state_conv_r, state_h, meta_tokens, norm1, w_in, conv_a_w, conv_r_w,
           conv_r_b, lru_wa, lru_ba, lru_wx, lru_bx, lru_lam, w_br_a, w_br_r, w_o, norm2, w_group, b_group,
           w_router, b_router, w_gate, w_up, w_down, norm_f):
    batch, seq, d = x_prompt.shape
    dec_batch, dec_seq, _ = x_sample.shape
    depth = norm1.shape[0]
    width = w_br_a.shape[1]
    n_groups = w_group.shape[-1]
    n_experts = w_router.shape[-1]
    n_meta = meta_tokens.shape[0]
    assert depth == 1, "meta rows are folded into an initial state, which only carries one layer"
    assert dec_seq == SUBLANES and n_meta % SUBLANES == 0 and n_meta >= SUBLANES
    assert seq % TM == 0 and (dec_batch * dec_seq) % TM == 0
    assert n_groups <= SUBLANES and n_experts == n_groups * SUBLANES
    assert width % CB == 0 and d % (2 * CB) == 0 and CB % lru_wa.shape[-1] == 0

    n_prompt_rows = batch * seq
    n_sample_rows = dec_batch * dec_seq
    m_rows = n_prompt_rows + n_sample_rows
    n_prompt_tiles = n_prompt_rows // TM
    n_tiles = m_rows // TM
    tiles_per_seq = seq // TM
    n_exp_tiles = (TOP_K * m_rows) // T_EXP + n_experts
    assert n_exp_tiles <= LANES

    xp = x_prompt.reshape(n_prompt_rows, d)
    xs = x_sample.reshape(n_sample_rows, d)
    row = lambda v: v.reshape(1, -1)

    u, um = _norm1(xp, xs, meta_tokens, row(norm1[0]), n_prompt_tiles, n_tiles)

    geom = dict(width=width, n_seq=batch, tiles_per_seq=tiles_per_seq,
                n_prompt_tiles=n_prompt_tiles, n_tiles=n_tiles)
    out_a, tail_a_p, tail_a_s = _mixer_a(
        u, um, w_in[0], conv_a_w[0], _sample_headers(state_conv_a[0], K_A), **geom)
    out_r, tail_r_p, tail_r_s, h_p, h_s = _mixer_r(
        u, um, w_in[0], conv_r_w[0], row(conv_r_b[0]), lru_wa[0], row(lru_ba[0]), lru_wx[0], row(lru_bx[0]),
        row(lru_lam[0]), _sample_headers(state_conv_r[0], K_R), state_h[0], seg0=3, **geom)

    merged = _merge(u, out_a, out_r, w_in[0], w_br_a[0], w_br_r[0], seg_cols=5 * width, n_tiles=n_tiles)
    x1 = _oproj(merged, xp, xs, w_o[0], n_prompt_tiles=n_prompt_tiles, n_tiles=n_tiles)

    wt = jnp.concatenate([jnp.pad(w_group[0].T, ((0, SUBLANES - n_groups), (0, 0))), w_router[0].T], axis=0)
    bt = jnp.concatenate([jnp.pad(b_group[0], (0, SUBLANES - n_groups)), b_router[0]]).reshape(-1, 1)
    xn, eid, wts = _router(x1, row(norm2[0]), wt, bt, n_groups=n_groups)

    pos, info = _positions(eid, n_experts=n_experts)
    pos_flat = pos.reshape(-1)
    info_flat = info.reshape(-1)
    xg = _scatter_rows(pos_flat, info_flat, xn, n_experts=n_experts, n_tiles=n_exp_tiles)
    hid = _expert_up(info_flat, xg, w_gate[0], w_up[0], n_tiles=n_exp_tiles)
    y = _expert_down(info_flat, hid, w_down[0], n_tiles=n_exp_tiles)

    wts_t = wts.T
    g_f = row(norm_f)
    y_prompt = _combine(pos_flat, x1, wts_t, g_f, y, row0=0, n_rows=n_prompt_rows)
    y_sample = _combine(pos_flat, x1, wts_t, g_f, y, row0=n_prompt_rows, n_rows=n_sample_rows)

    def sample_tail(t, k):
        return t.reshape(dec_batch, SUBLANES, width)[:, SUBLANES - k:, :][None]

    return (
        y_prompt.reshape(batch, seq, d),
        y_sample.reshape(dec_batch, dec_seq, d),
        tail_a_p[:, SUBLANES - (K_A - 1):, :][None],
        tail_r_p[:, SUBLANES - (K_R - 1):, :][None],
        h_p[:, SUBLANES - 1, :][None],
        sample_tail(tail_a_s, K_A - 1),
        sample_tail(tail_r_s, K_R - 1),
        sample_tail(h_s, 1)[:, :, 0, :],
    )
```

```python
import functools

import jax
import jax.numpy as jnp
from jax import lax
from jax.experimental import pallas as pl
from jax.experimental.pallas import tpu as pltpu

F32, BF16, I32 = jnp.float32, jnp.bfloat16, jnp.int32

EPS = 1e-6
C_RG = 8.0
K_A = 3
K_R = 4
LRU_HEADS = 16
TOP_K = 2

SUBLANES = 8
LANES = 128
VMEM_LIMIT_BYTES = 56 * 1024 * 1024

TM = 512
CB = 256
T_EXP = 256
TR = 256


def _dot(a, b):
    return jnp.dot(a, b, preferred_element_type=F32)


def _rmsnorm_rows(x, g):
    y = x * lax.rsqrt(jnp.mean(x * x, axis=-1, keepdims=True) + EPS)
    return y * g


def _params(n_axes):
    return pltpu.CompilerParams(dimension_semantics=("arbitrary",) * n_axes,
                                vmem_limit_bytes=VMEM_LIMIT_BYTES)


def _norm1_kernel(xp_ref, xs_ref, meta_ref, g_ref, u_ref, um_ref, *, n_prompt_tiles):
    i = pl.program_id(0)
    g = g_ref[...]

    @pl.when(i < n_prompt_tiles)
    def _():
        u_ref[...] = _rmsnorm_rows(xp_ref[...], g).astype(BF16)

    @pl.when(i >= n_prompt_tiles)
    def _():
        u_ref[...] = _rmsnorm_rows(xs_ref[...], g).astype(BF16)

    @pl.when(i == 0)
    def _():
        um_ref[...] = _rmsnorm_rows(meta_ref[...], g).astype(BF16)


def _norm1(xp, xs, meta, g, n_prompt_tiles, n_tiles):
    d = xp.shape[1]
    return pl.pallas_call(
        functools.partial(_norm1_kernel, n_prompt_tiles=n_prompt_tiles),
        grid=(n_tiles,),
        in_specs=[
            pl.BlockSpec((TM, d), lambda i: (jnp.minimum(i, n_prompt_tiles - 1), 0)),
            pl.BlockSpec((TM, d), lambda i: (jnp.maximum(i - n_prompt_tiles, 0), 0)),
            pl.BlockSpec(meta.shape, lambda i: (0, 0)),
            pl.BlockSpec((1, d), lambda i: (0, 0)),
        ],
        out_specs=[
            pl.BlockSpec((TM, d), lambda i: (i, 0)),
            pl.BlockSpec(meta.shape, lambda i: (0, 0)),
        ],
        out_shape=[
            jax.ShapeDtypeStruct((n_tiles * TM, d), BF16),
            jax.ShapeDtypeStruct(meta.shape, BF16),
        ],
        compiler_params=_params(1),
        name="norm1",
    )(xp, xs, meta, g)


def _conv_taps(window, cw, k_width):
    acc = window(0) * cw[0:1, :]
    for k in range(1, k_width):
        acc = acc + window(k) * cw[k:k + 1, :]
    return acc


def _conv_flat(ext_ref, values, header_ref, cw, k_width, rows):
    ext_ref[0:SUBLANES, :] = header_ref[...]
    ext_ref[SUBLANES:SUBLANES + rows, :] = values
    base = SUBLANES - (k_width - 1)
    return _conv_taps(lambda k: ext_ref[pl.ds(base + k, rows), :], cw, k_width)


def _conv_grouped(ext3_ref, values, header, cw, k_width, rows):
    n_seq = rows // SUBLANES
    cols = values.shape[1]
    ext3_ref[:, 0:SUBLANES, :] = header.reshape(n_seq, SUBLANES, cols)
    ext3_ref[:, SUBLANES:, :] = values.reshape(n_seq, SUBLANES, cols)
    base = SUBLANES - (k_width - 1)
    return _conv_taps(
        lambda k: ext3_ref[:, base + k:base + k + SUBLANES, :].reshape(rows, cols), cw, k_width)


def _mixer_a_kernel(u_ref, um_ref, wb_ref, wc_ref, wv_ref, cw_ref, hdr_ref,
                    oa_ref, tailp_ref, tails_ref,
                    wbf_ref, ext_ref, ext3_ref, carry_ref, mhdr_ref,
                    *, tiles_per_seq, n_prompt_tiles, n_meta):
    m = pl.program_id(1)
    cw = cw_ref[...]

    @pl.when(m == 0)
    def _():
        wbf_ref[0] = wb_ref[...].astype(BF16)
        wbf_ref[1] = wc_ref[...].astype(BF16)
        wbf_ref[2] = wv_ref[...].astype(BF16)
        um = um_ref[...]
        cv_meta = _dot(um, wbf_ref[1]) * _dot(um, wbf_ref[2])
        mhdr_ref[...] = cv_meta[n_meta - SUBLANES:, :]

    u = u_ref[...]
    zb = _dot(u, wbf_ref[0])
    cv = _dot(u, wbf_ref[1]) * _dot(u, wbf_ref[2])

    @pl.when(m < n_prompt_tiles)
    def _():
        @pl.when((m % tiles_per_seq) == 0)
        def _():
            carry_ref[...] = mhdr_ref[...]

        conv = _conv_flat(ext_ref, cv, carry_ref, cw, K_A, TM)
        oa_ref[...] = (zb * conv).astype(BF16)
        carry_ref[...] = ext_ref[TM:, :]
        tailp_ref[0] = ext_ref[TM:, :]

    @pl.when(m >= n_prompt_tiles)
    def _():
        conv = _conv_grouped(ext3_ref, cv, hdr_ref[...], cw, K_A, TM)
        oa_ref[...] = (zb * conv).astype(BF16)
        tails_ref[...] = cv


def _mixer_a(u, um, w_in, conv_w, hdr_s, *, width, n_seq, tiles_per_seq, n_prompt_tiles, n_tiles):
    d = u.shape[1]
    nb = width // CB
    n_meta = um.shape[0]
    last_seq = n_seq - 1

    def wspec(seg):
        return pl.BlockSpec((d, CB), lambda j, m, seg=seg: (0, seg * nb + j))

    return pl.pallas_call(
        functools.partial(_mixer_a_kernel, tiles_per_seq=tiles_per_seq,
                          n_prompt_tiles=n_prompt_tiles, n_meta=n_meta),
        grid=(nb, n_tiles),
        in_specs=[
            pl.BlockSpec((TM, d), lambda j, m: (m, 0)),
            pl.BlockSpec(um.shape, lambda j, m: (0, 0)),
            wspec(0), wspec(1), wspec(2),
            pl.BlockSpec((K_A, CB), lambda j, m: (0, j)),
            pl.BlockSpec((TM, CB), lambda j, m: (jnp.maximum(m - n_prompt_tiles, 0), j)),
        ],
        out_specs=[
            pl.BlockSpec((TM, CB), lambda j, m: (m, j)),
            pl.BlockSpec((1, SUBLANES, CB),
                         lambda j, m: (jnp.minimum(m // tiles_per_seq, last_seq), 0, j)),
            pl.BlockSpec((TM, CB), lambda j, m: (jnp.maximum(m - n_prompt_tiles, 0), j)),
        ],
        out_shape=[
            jax.ShapeDtypeStruct((n_tiles * TM, width), BF16),
            jax.ShapeDtypeStruct((n_seq, SUBLANES, width), F32),
            jax.ShapeDtypeStruct(hdr_s.shape, F32),
        ],
        scratch_shapes=[
            pltpu.VMEM((3, d, CB), BF16),
            pltpu.VMEM((TM + SUBLANES, CB), F32),
            pltpu.VMEM((TM // SUBLANES, 2 * SUBLANES, CB), F32),
            pltpu.VMEM((SUBLANES, CB), F32),
            pltpu.VMEM((SUBLANES, CB), F32),
        ],
        compiler_params=_params(2),
        name="mixer_a",
    )(u, um, w_in, w_in, w_in, conv_w, hdr_s)


def _softplus(x):
    return jnp.maximum(x, 0.0) + jnp.log1p(jnp.exp(-jnp.abs(x)))


def _lru_kernel(u_ref, um_ref, wx_ref, wy_ref, cw_ref, cbias_ref, wa_ref, ba_ref, wi_ref, bi_ref, lam_ref,
                hdr_ref, h0_ref,
                or_ref, tailp_ref, tails_ref, hp_ref, hs_ref,
                wbf_ref, gbf_ref, ext_ref, ext3_ref, a_ref, b_ref, h_ref, carry_ref, hcarry_ref,
                mhdr_ref, mh_ref,
                *, tiles_per_seq, n_prompt_tiles, n_meta, lru_block):
    m = pl.program_id(1)
    cw = cw_ref[...]
    cbias = cbias_ref[...]
    heads = CB // lru_block
    row8 = lax.broadcasted_iota(I32, (SUBLANES, CB), 0)

    def gate_terms(xc):
        xb = xc.astype(BF16)
        ra, ri = [], []
        for hh in range(heads):
            xh = xb[:, hh * lru_block:(hh + 1) * lru_block]
            ra.append(_dot(xh, gbf_ref[0, hh]))
            ri.append(_dot(xh, gbf_ref[1, hh]))
        r = jax.nn.sigmoid(jnp.concatenate(ra, axis=-1) + ba_ref[...])
        i = jax.nn.sigmoid(jnp.concatenate(ri, axis=-1) + bi_ref[...])
        log_a = -C_RG * r * _softplus(-lam_ref[...])
        a = jnp.exp(log_a)
        b = jnp.sqrt((1.0 + a * a) * jnp.tanh(-log_a)) * i * xc
        return a, b

    def scan(rows, h_init, per_group_h0):
        def body(g, hp):
            r0 = pl.multiple_of(g * SUBLANES, SUBLANES)
            a8 = a_ref[pl.ds(r0, SUBLANES), :]
            b8 = b_ref[pl.ds(r0, SUBLANES), :]
            for dist in (1, 2, 4):
                keep = row8 >= dist
                b8 = jnp.where(keep, a8 * pltpu.roll(b8, dist, 0) + b8, b8)
                a8 = jnp.where(keep, a8 * pltpu.roll(a8, dist, 0), a8)
            h_in = h0_ref[pl.ds(g, 1), :] if per_group_h0 else hp
            h8 = a8 * h_in + b8
            h_ref[pl.ds(r0, SUBLANES), :] = h8
            return h8[SUBLANES - 1:SUBLANES, :]
        return lax.fori_loop(0, rows // SUBLANES, body, h_init, unroll=min(8, rows // SUBLANES))

    @pl.when(m == 0)
    def _():
        wbf_ref[0] = wx_ref[...].astype(BF16)
        wbf_ref[1] = wy_ref[...].astype(BF16)
        gbf_ref[0] = wa_ref[...].astype(BF16)
        gbf_ref[1] = wi_ref[...].astype(BF16)
        zx_meta = _dot(um_ref[...], wbf_ref[0])
        carry_ref[...] = jnp.zeros((SUBLANES, CB), F32)
        xc = _conv_flat(ext_ref, zx_meta, carry_ref, cw, K_R, n_meta) + cbias
        a, b = gate_terms(xc)
        a_ref[0:n_meta, :] = a
        b_ref[0:n_meta, :] = b
        h_meta = scan(n_meta, jnp.zeros((1, CB), F32), False)
        mh_ref[...] = jnp.broadcast_to(h_meta, (SUBLANES, CB))
        mhdr_ref[...] = zx_meta[n_meta - SUBLANES:, :]

    u = u_ref[...]
    zx = _dot(u, wbf_ref[0])
    zy = _dot(u, wbf_ref[1])

    @pl.when(m < n_prompt_tiles)
    def _():
        @pl.when((m % tiles_per_seq) == 0)
        def _():
            carry_ref[...] = mhdr_ref[...]
            hcarry_ref[...] = mh_ref[...]

        xc = _conv_flat(ext_ref, zx, carry_ref, cw, K_R, TM) + cbias
        a, b = gate_terms(xc)
        a_ref[...] = a
        b_ref[...] = b
        hcarry_ref[0:1, :] = scan(TM, hcarry_ref[0:1, :], False)
        or_ref[...] = (h_ref[...] * jax.nn.gelu(zy)).astype(BF16)
        carry_ref[...] = ext_ref[TM:, :]
        tailp_ref[0] = ext_ref[TM:, :]
        hp_ref[0] = h_ref[TM - SUBLANES:, :]

    @pl.when(m >= n_prompt_tiles)
    def _():
        xc = _conv_grouped(ext3_ref, zx, hdr_ref[...], cw, K_R, TM) + cbias
        a, b = gate_terms(xc)
        a_ref[...] = a
        b_ref[...] = b
        scan(TM, jnp.zeros((1, CB), F32), True)
        h = h_ref[...]
        or_ref[...] = (h * jax.nn.gelu(zy)).astype(BF16)
        tails_ref[...] = zx
        hs_ref[...] = h


def _mixer_r(u, um, w_in, conv_w, conv_b, lru_wa, lru_ba, lru_wx, lru_bx, lru_lam, hdr_s, h0_s,
             *, width, seg0, n_seq, tiles_per_seq, n_prompt_tiles, n_tiles):
    d = u.shape[1]
    nb = width // CB
    n_meta = um.shape[0]
    last_seq = n_seq - 1
    lru_block = lru_wa.shape[-1]
    heads = CB // lru_block
    seqs_per_tile = TM // SUBLANES

    def wspec(seg):
        return pl.BlockSpec((d, CB), lambda j, m, seg=seg: (0, seg * nb + j))

    def vec():
        return pl.BlockSpec((1, CB), lambda j, m: (0, j))

    def gspec():
        return pl.BlockSpec((heads, lru_block, lru_block), lambda j, m: (j, 0, 0))

    def sample_rows():
        return pl.BlockSpec((TM, CB), lambda j, m: (jnp.maximum(m - n_prompt_tiles, 0), j))

    def seq_tail():
        return pl.BlockSpec((1, SUBLANES, CB),
                            lambda j, m: (jnp.minimum(m // tiles_per_seq, last_seq), 0, j))

    return pl.pallas_call(
        functools.partial(_lru_kernel, tiles_per_seq=tiles_per_seq, n_prompt_tiles=n_prompt_tiles,
                          n_meta=n_meta, lru_block=lru_block),
        grid=(nb, n_tiles),
        in_specs=[
            pl.BlockSpec((TM, d), lambda j, m: (m, 0)),
            pl.BlockSpec(um.shape, lambda j, m: (0, 0)),
            wspec(seg0), wspec(seg0 + 1),
            pl.BlockSpec((K_R, CB), lambda j, m: (0, j)),
            vec(),
            gspec(), vec(), gspec(), vec(), vec(),
            sample_rows(),
            pl.BlockSpec((seqs_per_tile, CB), lambda j, m: (jnp.maximum(m - n_prompt_tiles, 0), j)),
        ],
        out_specs=[
            pl.BlockSpec((TM, CB), lambda j, m: (m, j)),
            seq_tail(), sample_rows(), seq_tail(), sample_rows(),
        ],
        out_shape=[
            jax.ShapeDtypeStruct((n_tiles * TM, width), BF16),
            jax.ShapeDtypeStruct((n_seq, SUBLANES, width), F32),
            jax.ShapeDtypeStruct(hdr_s.shape, F32),
            jax.ShapeDtypeStruct((n_seq, SUBLANES, width), F32),
            jax.ShapeDtypeStruct(hdr_s.shape, F32),
        ],
        scratch_shapes=[
            pltpu.VMEM((2, d, CB), BF16),
            pltpu.VMEM((2, heads, lru_block, lru_block), BF16),
            pltpu.VMEM((TM + SUBLANES, CB), F32),
            pltpu.VMEM((seqs_per_tile, 2 * SUBLANES, CB), F32),
            pltpu.VMEM((TM, CB), F32),
            pltpu.VMEM((TM, CB), F32),
            pltpu.VMEM((TM, CB), F32),
            pltpu.VMEM((SUBLANES, CB), F32),
            pltpu.VMEM((SUBLANES, CB), F32),
            pltpu.VMEM((SUBLANES, CB), F32),
            pltpu.VMEM((SUBLANES, CB), F32),
        ],
        compiler_params=_params(2),
        name="mixer_r",
    )(u, um, w_in, w_in, conv_w, conv_b, lru_wa, lru_ba, lru_wx, lru_bx, lru_lam, hdr_s, h0_s)


def _merge_kernel(u_ref, oa_ref, or_ref, wga_ref, wgr_ref, wba_ref, wbr_ref, mg_ref, wg_bf, wb_bf):
    @pl.when(pl.program_id(1) == 0)
    def _():
        wg_bf[0] = wga_ref[...].astype(BF16)
        wg_bf[1] = wgr_ref[...].astype(BF16)
        wb_bf[0] = wba_ref[...].astype(BF16)
        wb_bf[1] = wbr_ref[...].astype(BF16)

    u = u_ref[...]
    ga = jax.nn.sigmoid(_dot(u, wg_bf[0]))
    gr = jax.nn.sigmoid(_dot(u, wg_bf[1]))
    mg = ga * _dot(oa_ref[...], wb_bf[0]) + gr * _dot(or_ref[...], wb_bf[1])
    mg_ref[...] = mg.astype(BF16)


def _merge(u, out_a, out_r, w_in, w_br_a, w_br_r, *, seg_cols, n_tiles):
    d = u.shape[1]
    width = out_a.shape[1]
    nb = d // CB
    seg_a = seg_cols // CB

    return pl.pallas_call(
        _merge_kernel,
        grid=(nb, n_tiles),
        in_specs=[
            pl.BlockSpec((TM, d), lambda j, m: (m, 0)),
            pl.BlockSpec((TM, width), lambda j, m: (m, 0)),
            pl.BlockSpec((TM, width), lambda j, m: (m, 0)),
            pl.BlockSpec((d, CB), lambda j, m: (0, seg_a + j)),
            pl.BlockSpec((d, CB), lambda j, m: (0, seg_a + nb + j)),
            pl.BlockSpec((width, CB), lambda j, m: (0, j)),
            pl.BlockSpec((width, CB), lambda j, m: (0, j)),
        ],
        out_specs=pl.BlockSpec((TM, CB), lambda j, m: (m, j)),
        out_shape=jax.ShapeDtypeStruct((n_tiles * TM, d), BF16),
        scratch_shapes=[pltpu.VMEM((2, d, CB), BF16), pltpu.VMEM((2, width, CB), BF16)],
        compiler_params=_params(2),
        name="merge",
    )(u, out_a, out_r, w_in, w_in, w_br_a, w_br_r)


def _oproj_kernel(mg_ref, xp_ref, xs_ref, wo_ref, x1_ref, wbf_ref, *, n_prompt_tiles):
    m = pl.program_id(1)

    @pl.when(m == 0)
    def _():
        wbf_ref[...] = wo_ref[...].astype(BF16)

    y = _dot(mg_ref[...], wbf_ref[...])

    @pl.when(m < n_prompt_tiles)
    def _():
        x1_ref[...] = xp_ref[...] + y

    @pl.when(m >= n_prompt_tiles)
    def _():
        x1_ref[...] = xs_ref[...] + y


def _oproj(mg, xp, xs, w_o, *, n_prompt_tiles, n_tiles):
    d = mg.shape[1]
    cb = 2 * CB
    nb = d // cb
    return pl.pallas_call(
        functools.partial(_oproj_kernel, n_prompt_tiles=n_prompt_tiles),
        grid=(nb, n_tiles),
        in_specs=[
            pl.BlockSpec((TM, d), lambda j, m: (m, 0)),
            pl.BlockSpec((TM, cb), lambda j, m: (jnp.minimum(m, n_prompt_tiles - 1), j)),
            pl.BlockSpec((TM, cb), lambda j, m: (jnp.maximum(m - n_prompt_tiles, 0), j)),
            pl.BlockSpec((d, cb), lambda j, m: (0, j)),
        ],
        out_specs=pl.BlockSpec((TM, cb), lambda j, m: (m, j)),
        out_shape=jax.ShapeDtypeStruct((n_tiles * TM, d), F32),
        scratch_shapes=[pltpu.VMEM((d, cb), BF16)],
        compiler_params=_params(2),
        name="oproj",
    )(mg, xp, xs, w_o)


def _router_kernel(x1_ref, g_ref, wt_ref, bt_ref, xn_ref, eid_ref, wts_ref, *, n_groups):
    xn = _rmsnorm_rows(x1_ref[...], g_ref[...])
    xn_ref[...] = xn
    lg = lax.dot_general(wt_ref[...].astype(BF16), xn.astype(BF16), (((1,), (1,)), ((), ())),
                         preferred_element_type=F32) + bt_ref[...]
    rows = xn.shape[0]
    row8 = lax.broadcasted_iota(I32, (SUBLANES, rows), 0).astype(F32)

    def first_index_of(v, vmax):
        return jnp.min(jnp.where(v == vmax, row8, float(SUBLANES)), axis=0, keepdims=True)

    gl = jnp.where(row8 < float(n_groups), lg[0:SUBLANES, :], -jnp.inf)
    ge = jnp.exp(gl - jnp.max(gl, axis=0, keepdims=True))
    gprob = ge / jnp.sum(ge, axis=0, keepdims=True)
    g_p = jnp.max(gprob, axis=0, keepdims=True)
    g_idx = first_index_of(gprob, g_p)

    esel = jnp.zeros((SUBLANES, rows), F32)
    for k in range(n_groups):
        esel = jnp.where(g_idx == float(k), lg[(k + 1) * SUBLANES:(k + 2) * SUBLANES, :], esel)
    ee = jnp.exp(esel - jnp.max(esel, axis=0, keepdims=True))
    ep = ee / jnp.sum(ee, axis=0, keepdims=True)
    p1 = jnp.max(ep, axis=0, keepdims=True)
    i1 = first_index_of(ep, p1)
    ep_rest = jnp.where(row8 == i1, -1.0, ep)
    p2 = jnp.max(ep_rest, axis=0, keepdims=True)
    i2 = first_index_of(ep_rest, p2)

    den = p1 + p2
    wts_ref[0:1, :] = p1 / den * g_p
    wts_ref[1:2, :] = p2 / den * g_p
    eid_ref[0:1, :] = (g_idx * float(SUBLANES) + i1).astype(I32)
    eid_ref[1:2, :] = (g_idx * float(SUBLANES) + i2).astype(I32)


def _router(x1, g2, wt, bt, *, n_groups):
    m_rows, d = x1.shape
    return pl.pallas_call(
        functools.partial(_router_kernel, n_groups=n_groups),
        grid=(m_rows // TR,),
        in_specs=[
            pl.BlockSpec((TR, d), lambda i: (i, 0)),
            pl.BlockSpec((1, d), lambda i: (0, 0)),
            pl.BlockSpec(wt.shape, lambda i: (0, 0)),
            pl.BlockSpec(bt.shape, lambda i: (0, 0)),
        ],
        out_specs=[
            pl.BlockSpec((TR, d), lambda i: (i, 0)),
            pl.BlockSpec((TOP_K, TR), lambda i: (0, i)),
            pl.BlockSpec((TOP_K, TR), lambda i: (0, i)),
        ],
        out_shape=[
            jax.ShapeDtypeStruct((m_rows, d), F32),
            jax.ShapeDtypeStruct((TOP_K, m_rows), I32),
            jax.ShapeDtypeStruct((TOP_K, m_rows), F32),
        ],
        compiler_params=_params(1),
        name="router",
    )(x1, g2, wt, bt)


INFO_TILE_EXPERT, INFO_NUM_TILES, INFO_VALID_END, INFO_END_TILE, INFO_ROWS = 0, 1, 2, 3, SUBLANES


def _positions_kernel(eid_ref, pos_ref, info_ref, exc_ref, *, n_experts, tile_shift):
    m_rows = eid_ref.shape[1]
    chunk = 2 * LANES
    n_chunks = m_rows // chunk
    tile = 1 << tile_shift
    e_col = lax.broadcasted_iota(I32, (n_experts, chunk), 0)
    upper = (lax.broadcasted_iota(I32, (chunk, chunk), 0)
             < lax.broadcasted_iota(I32, (chunk, chunk), 1)).astype(BF16)

    def onehots(c):
        off = pl.multiple_of(c * chunk, chunk)
        ids = eid_ref[:, pl.ds(off, chunk)]
        return off, [(e_col == ids[k:k + 1, :]) for k in range(TOP_K)]

    def count(c, carry):
        off, hot = onehots(c)
        used = jnp.where(hot[0] | hot[1], 1.0, 0.0)
        exc_ref[:, pl.ds(off, chunk)] = _dot(used.astype(BF16), upper) + carry
        return carry + jnp.sum(used, axis=1, keepdims=True)

    counts = lax.fori_loop(0, n_chunks, count, jnp.zeros((n_experts, 1), F32))

    lane = lax.broadcasted_iota(I32, (n_experts, LANES), 1)
    sub = lax.broadcasted_iota(I32, (n_experts, LANES), 0)

    def to_lanes(col):
        return jnp.sum(jnp.where(lane == sub, col, 0.0), axis=0, keepdims=True)

    n_tiles_e = ((counts.astype(I32) + (tile - 1)) >> tile_shift).astype(F32)
    first_tile = jnp.sum(jnp.where(lane < sub, to_lanes(n_tiles_e), 0.0), axis=1, keepdims=True)
    end_tile = first_tile + n_tiles_e
    first_row = first_tile * float(tile)

    def place(c, _):
        off, hot = onehots(c)
        dest = exc_ref[:, pl.ds(off, chunk)] + first_row
        for k in range(TOP_K):
            pos_ref[k:k + 1, pl.ds(off, chunk)] = jnp.sum(
                jnp.where(hot[k], dest, 0.0), axis=0, keepdims=True).astype(I32)
        return 0

    lax.fori_loop(0, n_chunks, place, 0)

    tile_expert = jnp.sum(jnp.where(end_tile <= lane.astype(F32), 1.0, 0.0), axis=0, keepdims=True)
    num_tiles = jnp.max(end_tile, axis=0, keepdims=True)
    valid_end = to_lanes(first_row + counts)
    info_ref[...] = jnp.zeros(info_ref.shape, I32)
    info_ref[INFO_TILE_EXPERT:INFO_TILE_EXPERT + 1, :] = jnp.minimum(
        tile_expert, float(n_experts - 1)).astype(I32)
    info_ref[INFO_NUM_TILES:INFO_NUM_TILES + 1, :] = jnp.broadcast_to(num_tiles, (1, LANES)).astype(I32)
    info_ref[INFO_VALID_END:INFO_VALID_END + 1, :] = valid_end.astype(I32)
    info_ref[INFO_END_TILE:INFO_END_TILE + 1, :] = to_lanes(end_tile).astype(I32)


def _positions(eid, *, n_experts):
    m_rows = eid.shape[1]
    tile_shift = T_EXP.bit_length() - 1
    return pl.pallas_call(
        functools.partial(_positions_kernel, n_experts=n_experts, tile_shift=tile_shift),
        out_shape=[
            jax.ShapeDtypeStruct((TOP_K, m_rows), I32),
            jax.ShapeDtypeStruct((INFO_ROWS, LANES), I32),
        ],
        scratch_shapes=[pltpu.VMEM((n_experts, m_rows), F32)],
        compiler_params=pltpu.CompilerParams(vmem_limit_bytes=VMEM_LIMIT_BYTES),
        name="positions",
    )(eid)


def _info(info_ref, row, lane):
    return info_ref[row * LANES + lane]


def _tile_state(i, info_ref):
    n_used = _info(info_ref, INFO_NUM_TILES, 0)
    ic = jnp.minimum(i, n_used - 1)
    e = _info(info_ref, INFO_TILE_EXPERT, ic)
    prev = _info(info_ref, INFO_TILE_EXPERT, jnp.maximum(ic - 1, 0))
    first = jnp.logical_or(ic == 0, e != prev)
    n_valid = jnp.minimum(_info(info_ref, INFO_VALID_END, e) - ic * T_EXP, T_EXP)
    return i < n_used, ic, e, first, n_valid


def _stream_expert_weights(i, info_ref, e, first, w_refs, wf_ref, wbf_ref, wsem, slot_ref):
    n_used = _info(info_ref, INFO_NUM_TILES, 0)
    rows = wf_ref.shape[2]
    chunk = min(rows, 512)

    def copies(expert, slot):
        return [pltpu.make_async_copy(w.at[expert], wf_ref.at[slot, j], wsem.at[slot])
                for j, w in enumerate(w_refs)]

    @pl.when(i == 0)
    def _():
        slot_ref[0] = 0
        for cp in copies(e, 0):
            cp.start()

    @pl.when(jnp.logical_and(first, i > 0))
    def _():
        slot_ref[0] = 1 - slot_ref[0]

    @pl.when(first)
    def _():
        slot = slot_ref[0]
        for cp in copies(e, slot):
            cp.wait()
        for j in range(len(w_refs)):
            def cast(c, _, j=j):
                r0 = pl.multiple_of(c * chunk, chunk)
                wbf_ref[j, pl.ds(r0, chunk), :] = wf_ref[slot, j, pl.ds(r0, chunk), :].astype(BF16)
                return 0
            lax.fori_loop(0, rows // chunk, cast, 0)
        next_tile = _info(info_ref, INFO_END_TILE, e)

        @pl.when(next_tile < n_used)
        def _():
            for cp in copies(_info(info_ref, INFO_TILE_EXPERT, next_tile), 1 - slot):
                cp.start()


def _expert_up_kernel(info_ref, pos_ref, xn_ref, wg_ref, wu_ref, h_ref,
                      xbuf_ref, wf_ref, wbf_ref, inv_ref, slot_ref, xsem, wsem, *, m_rows):
    i = pl.program_id(0)
    used, _, e, first, n_valid = _tile_state(i, info_ref)
    n_used = _info(info_ref, INFO_NUM_TILES, 0)

    def gather(tile, slot, n_rows, start):
        def body(r, _):
            src = inv_ref[tile * T_EXP + r] if start else 0
            cp = pltpu.make_async_copy(xn_ref.at[pl.ds(src, 1), :], xbuf_ref.at[slot, pl.ds(r, 1), :],
                                       xsem.at[slot])
            cp.start() if start else cp.wait()
            return 0
        lax.fori_loop(0, n_rows, body, 0)

    @pl.when(i == 0)
    def _():
        def invert(t, _):
            for k in range(TOP_K):
                inv_ref[pos_ref[k * m_rows + t]] = t
            return 0
        lax.fori_loop(0, m_rows, invert, 0, unroll=8)
        xbuf_ref[...] = jnp.zeros(xbuf_ref.shape, F32)
        gather(0, 0, n_valid, True)

    @pl.when(used)
    def _():
        _stream_expert_weights(i, info_ref, e, first, (wg_ref, wu_ref), wf_ref, wbf_ref, wsem, slot_ref)
        slot = i & 1
        gather(i, slot, n_valid, False)

        @pl.when(i + 1 < n_used)
        def _():
            gather(i + 1, 1 - slot, _tile_state(i + 1, info_ref)[4], True)

        x = xbuf_ref[slot].astype(BF16)
        h = jax.nn.silu(_dot(x, wbf_ref[0])) * _dot(x, wbf_ref[1])
        h_ref[...] = h.astype(BF16)

    @pl.when(jnp.logical_not(used))
    def _():
        h_ref[...] = jnp.zeros(h_ref.shape, BF16)


def _expert_up(info_flat, pos_flat, xn, w_gate, w_up, *, n_tiles):
    m_rows, d = xn.shape
    d_exp = w_gate.shape[-1]
    p_rows = n_tiles * T_EXP
    any_spec = pl.BlockSpec(memory_space=pl.ANY)
    return pl.pallas_call(
        functools.partial(_expert_up_kernel, m_rows=m_rows),
        grid_spec=pltpu.PrefetchScalarGridSpec(
            num_scalar_prefetch=2,
            grid=(n_tiles,),
            in_specs=[any_spec, any_spec, any_spec],
            out_specs=pl.BlockSpec((T_EXP, d_exp), lambda i, info, pos: (i, 0)),
            scratch_shapes=[
                pltpu.VMEM((2, T_EXP, d), F32),
                pltpu.VMEM((2, 2, d, d_exp), F32),
                pltpu.VMEM((2, d, d_exp), BF16),
                pltpu.SMEM((p_rows,), I32),
                pltpu.SMEM((1,), I32),
                pltpu.SemaphoreType.DMA((2,)),
                pltpu.SemaphoreType.DMA((2,)),
            ],
        ),
        out_shape=jax.ShapeDtypeStruct((p_rows, d_exp), BF16),
        compiler_params=_params(1),
        name="expert_up",
    )(info_flat, pos_flat, xn, w_gate, w_up)


def _expert_down_kernel(info_ref, h_ref, wd_ref, y_ref, wf_ref, wbf_ref, slot_ref, wsem):
    i = pl.program_id(0)
    used, _, e, first, _ = _tile_state(i, info_ref)

    @pl.when(used)
    def _():
        _stream_expert_weights(i, info_ref, e, first, (wd_ref,), wf_ref, wbf_ref, wsem, slot_ref)
        y_ref[...] = _dot(h_ref[...], wbf_ref[0])

    @pl.when(jnp.logical_not(used))
    def _():
        y_ref[...] = jnp.zeros(y_ref.shape, F32)


def _expert_down(info_flat, h, w_down, *, n_tiles):
    p_rows, d_exp = h.shape
    d = w_down.shape[-1]
    return pl.pallas_call(
        _expert_down_kernel,
        grid_spec=pltpu.PrefetchScalarGridSpec(
            num_scalar_prefetch=1,
            grid=(n_tiles,),
            in_specs=[
                pl.BlockSpec((T_EXP, d_exp), lambda i, info: (_tile_state(i, info)[1], 0)),
                pl.BlockSpec(memory_space=pl.ANY),
            ],
            out_specs=pl.BlockSpec((T_EXP, d), lambda i, info: (i, 0)),
            scratch_shapes=[
                pltpu.VMEM((2, 1, d_exp, d), F32),
                pltpu.VMEM((1, d_exp, d), BF16),
                pltpu.SMEM((1,), I32),
                pltpu.SemaphoreType.DMA((2,)),
            ],
        ),
        out_shape=jax.ShapeDtypeStruct((p_rows, d), F32),
        compiler_params=_params(1),
        name="expert_down",
    )(info_flat, h, w_down)


def _combine_kernel(pos_ref, x1_ref, w_ref, g_ref, y_ref, out_ref, ybuf_ref, sem, *, m_rows, row0, n_steps):
    i = pl.program_id(0)

    def gather(step, start):
        base = row0 + step * TR
        slot = step & 1

        def body(r, _):
            for k in range(TOP_K):
                src = pos_ref[k * m_rows + base + r] if start else 0
                cp = pltpu.make_async_copy(y_ref.at[pl.ds(src, 1), :], ybuf_ref.at[slot, k, pl.ds(r, 1), :],
                                           sem.at[slot])
                cp.start() if start else cp.wait()
            return 0
        lax.fori_loop(0, TR, body, 0, unroll=8)

    @pl.when(i == 0)
    def _():
        gather(0, True)

    @pl.when(i + 1 < n_steps)
    def _():
        gather(i + 1, True)

    gather(i, False)
    slot = i & 1
    w = w_ref[...]
    moe = w[:, 0:1] * ybuf_ref[slot, 0] + w[:, 1:2] * ybuf_ref[slot, 1]
    out_ref[...] = _rmsnorm_rows(x1_ref[...] + moe, g_ref[...])


def _combine(pos_flat, x1, wts_t, norm_f, y, *, row0, n_rows):
    m_rows, d = x1.shape
    tile0 = row0 // TR
    n_steps = n_rows // TR
    return pl.pallas_call(
        functools.partial(_combine_kernel, m_rows=m_rows, row0=row0, n_steps=n_steps),
        grid_spec=pltpu.PrefetchScalarGridSpec(
            num_scalar_prefetch=1,
            grid=(n_steps,),
            in_specs=[
                pl.BlockSpec((TR, d), lambda i, pos: (tile0 + i, 0)),
                pl.BlockSpec((TR, TOP_K), lambda i, pos: (tile0 + i, 0)),
                pl.BlockSpec((1, d), lambda i, pos: (0, 0)),
                pl.BlockSpec(memory_space=pl.ANY),
            ],
            out_specs=pl.BlockSpec((TR, d), lambda i, pos: (i, 0)),
            scratch_shapes=[pltpu.VMEM((2, TOP_K, TR, d), F32), pltpu.SemaphoreType.DMA((2,))],
        ),
        out_shape=jax.ShapeDtypeStruct((n_rows, d), F32),
        compiler_params=_params(1),
        name="combine",
    )(pos_flat, x1, wts_t, norm_f, y)


def _sample_headers(state, k_width):
    n, _, w = state.shape
    padded = jnp.pad(state, ((0, 0), (SUBLANES - (k_width - 1), 0), (0, 0)))
    return padded.reshape(n * SUBLANES, w)


def kernel(x_prompt, x_sample, state_conv_a, state_conv_r, state_h, meta_tokens, norm1, w_in, conv_a_w, conv_r_w,
           conv_r_b, lru_wa, lru_ba, lru_wx, lru_bx, lru_lam, w_br_a, w_br_r, w_o, norm2, w_group, b_group,
           w_router, b_router, w_gate, w_up, w_down, norm_f):
    batch, seq, d = x_prompt.shape
    dec_batch, dec_seq, _ = x_sample.shape
    depth = norm1.shape[0]
    width = w_br_a.shape[1]
    n_groups = w_group.shape[-1]
    n_experts = w_router.shape[-1]
    n_meta = meta_tokens.shape[0]
    assert depth == 1, "meta rows are folded into an initial state, which only carries one layer"
    assert dec_seq == SUBLANES and n_meta % SUBLANES == 0 and n_meta >= SUBLANES
    assert seq % TM == 0 and (dec_batch * dec_seq) % TM == 0
    assert n_groups <= SUBLANES and n_experts == n_groups * SUBLANES
    assert width % CB == 0 and d % (2 * CB) == 0 and CB % lru_wa.shape[-1] == 0

    n_prompt_rows = batch * seq
    n_sample_rows = dec_batch * dec_seq
    m_rows = n_prompt_rows + n_sample_rows
    n_prompt_tiles = n_prompt_rows // TM
    n_tiles = m_rows // TM
    tiles_per_seq = seq // TM
    n_exp_tiles = (TOP_K * m_rows) // T_EXP + n_experts
    assert n_exp_tiles <= LANES

    xp = x_prompt.reshape(n_prompt_rows, d)
    xs = x_sample.reshape(n_sample_rows, d)
    row = lambda v: v.reshape(1, -1)

    u, um = _norm1(xp, xs, meta_tokens, row(norm1[0]), n_prompt_tiles, n_tiles)

    geom = dict(width=width, n_seq=batch, tiles_per_seq=tiles_per_seq,
                n_prompt_tiles=n_prompt_tiles, n_tiles=n_tiles)
    out_a, tail_a_p, tail_a_s = _mixer_a(
        u, um, w_in[0], conv_a_w[0], _sample_headers(state_conv_a[0], K_A), **geom)
    out_r, tail_r_p, tail_r_s, h_p, h_s = _mixer_r(
        u, um, w_in[0], conv_r_w[0], row(conv_r_b[0]), lru_wa[0], row(lru_ba[0]), lru_wx[0], row(lru_bx[0]),
        row(lru_lam[0]), _sample_headers(state_conv_r[0], K_R), state_h[0], seg0=3, **geom)

    merged = _merge(u, out_a, out_r, w_in[0], w_br_a[0], w_br_r[0], seg_cols=5 * width, n_tiles=n_tiles)
    x1 = _oproj(merged, xp, xs, w_o[0], n_prompt_tiles=n_prompt_tiles, n_tiles=n_tiles)

    wt = jnp.concatenate([jnp.pad(w_group[0].T, ((0, SUBLANES - n_groups), (0, 0))), w_router[0].T], axis=0)
    bt = jnp.concatenate([jnp.pad(b_group[0], (0, SUBLANES - n_groups)), b_router[0]]).reshape(-1, 1)
    xn, eid, wts = _router(x1, row(norm2[0]), wt, bt, n_groups=n_groups)

    pos, info = _positions(eid, n_experts=n_experts)
    pos_flat = pos.reshape(-1)
    info_flat = info.reshape(-1)
    hid = _expert_up(info_flat, pos_flat, xn, w_gate[0], w_up[0], n_tiles=n_exp_tiles)
    y = _expert_down(info_flat, hid, w_down[0], n_tiles=n_exp_tiles)

    wts_t = wts.T
    g_f = row(norm_f)
    y_prompt = _combine(pos_flat, x1, wts_t, g_f, y, row0=0, n_rows=n_prompt_rows)
    y_sample = _combine(pos_flat, x1, wts_t, g_f, y, row0=n_prompt_rows, n_rows=n_sample_rows)

    def sample_tail(t, k):
        return t.reshape(dec_batch, SUBLANES, width)[:, SUBLANES - k:, :][None]

    return (
        y_prompt.reshape(batch, seq, d),
        y_sample.reshape(dec_batch, dec_seq, d),
        tail_a_p[:, SUBLANES - (K_A - 1):, :][None],
        tail_r_p[:, SUBLANES - (K_R - 1):, :][None],
        h_p[:, SUBLANES - 1, :][None],
        sample_tail(tail_a_s, K_A - 1),
        sample_tail(tail_r_s, K_R - 1),
        sample_tail(h_s, 1)[:, :, 0, :],
    )
```

```python
import functools

import jax
import jax.numpy as jnp
from jax import lax
from jax.experimental import pallas as pl
from jax.experimental.pallas import tpu as pltpu

F32, BF16, I32 = jnp.float32, jnp.bfloat16, jnp.int32

EPS = 1e-6
C_RG = 8.0
K_A = 3
K_R = 4
LRU_HEADS = 16
TOP_K = 2

SUBLANES = 8
LANES = 128
VMEM_LIMIT_BYTES = 56 * 1024 * 1024

TM = 512
TM_MIX_A = 512
TM_MIX_R = 1024
MIX_CHUNK = 256
CB = 256
T_EXP = 256
TR = 256


def _dot(a, b):
    return jnp.dot(a, b, preferred_element_type=F32)


def _rmsnorm_rows(x, g):
    y = x * lax.rsqrt(jnp.mean(x * x, axis=-1, keepdims=True) + EPS)
    return y * g


def _params(n_axes):
    return pltpu.CompilerParams(dimension_semantics=("arbitrary",) * n_axes,
                                vmem_limit_bytes=VMEM_LIMIT_BYTES)


def _norm1_kernel(xp_ref, xs_ref, meta_ref, g_ref, u_ref, um_ref, *, n_prompt_tiles):
    i = pl.program_id(0)
    g = g_ref[...]

    @pl.when(i < n_prompt_tiles)
    def _():
        u_ref[...] = _rmsnorm_rows(xp_ref[...], g).astype(BF16)

    @pl.when(i >= n_prompt_tiles)
    def _():
        u_ref[...] = _rmsnorm_rows(xs_ref[...], g).astype(BF16)

    @pl.when(i == 0)
    def _():
        um_ref[...] = _rmsnorm_rows(meta_ref[...], g).astype(BF16)


def _norm1(xp, xs, meta, g, n_prompt_tiles, n_tiles):
    d = xp.shape[1]
    return pl.pallas_call(
        functools.partial(_norm1_kernel, n_prompt_tiles=n_prompt_tiles),
        grid=(n_tiles,),
        in_specs=[
            pl.BlockSpec((TM, d), lambda i: (jnp.minimum(i, n_prompt_tiles - 1), 0)),
            pl.BlockSpec((TM, d), lambda i: (jnp.maximum(i - n_prompt_tiles, 0), 0)),
            pl.BlockSpec(meta.shape, lambda i: (0, 0)),
            pl.BlockSpec((1, d), lambda i: (0, 0)),
        ],
        out_specs=[
            pl.BlockSpec((TM, d), lambda i: (i, 0)),
            pl.BlockSpec(meta.shape, lambda i: (0, 0)),
        ],
        out_shape=[
            jax.ShapeDtypeStruct((n_tiles * TM, d), BF16),
            jax.ShapeDtypeStruct(meta.shape, BF16),
        ],
        compiler_params=_params(1),
        name="norm1",
    )(xp, xs, meta, g)


def _conv_taps(window, cw, k_width):
    acc = window(0) * cw[0:1, :]
    for k in range(1, k_width):
        acc = acc + window(k) * cw[k:k + 1, :]
    return acc


def _conv_flat(ext_ref, values, cw, k_width, r0):
    rows = values.shape[0]
    ext_ref[SUBLANES + r0:SUBLANES + r0 + rows, :] = values
    base = SUBLANES - (k_width - 1) + r0
    return _conv_taps(lambda k: ext_ref[pl.ds(base + k, rows), :], cw, k_width)


def _conv_grouped(ext3_ref, values, header, cw, k_width, s0):
    rows, cols = values.shape
    n_seq = rows // SUBLANES
    ext3_ref[s0:s0 + n_seq, 0:SUBLANES, :] = header.reshape(n_seq, SUBLANES, cols)
    ext3_ref[s0:s0 + n_seq, SUBLANES:, :] = values.reshape(n_seq, SUBLANES, cols)
    base = SUBLANES - (k_width - 1)
    return _conv_taps(
        lambda k: ext3_ref[s0:s0 + n_seq, base + k:base + k + SUBLANES, :].reshape(rows, cols), cw, k_width)


def _mixer_a_kernel(u_ref, um_ref, wb_ref, wc_ref, wv_ref, cw_ref, hdr_ref,
                    oa_ref, tailp_ref, tails_ref,
                    wbf_ref, ext_ref, ext3_ref, carry_ref, mhdr_ref,
                    *, tm, tiles_per_seq, n_prompt_tiles, n_meta):
    m = pl.program_id(1)
    cw = cw_ref[...]
    ch = MIX_CHUNK
    n_chunks = tm // ch

    @pl.when(m == 0)
    def _():
        wbf_ref[0] = wb_ref[...].astype(BF16)
        wbf_ref[1] = wc_ref[...].astype(BF16)
        wbf_ref[2] = wv_ref[...].astype(BF16)
        um = um_ref[...]
        cv_meta = _dot(um, wbf_ref[1]) * _dot(um, wbf_ref[2])
        mhdr_ref[...] = cv_meta[n_meta - SUBLANES:, :]

    def products(r0):
        u = u_ref[r0:r0 + ch, :]
        return _dot(u, wbf_ref[0]), _dot(u, wbf_ref[1]) * _dot(u, wbf_ref[2])

    @pl.when(m < n_prompt_tiles)
    def _():
        @pl.when((m % tiles_per_seq) == 0)
        def _():
            carry_ref[...] = mhdr_ref[...]

        ext_ref[0:SUBLANES, :] = carry_ref[...]
        for c in range(n_chunks):
            r0 = c * ch
            zb, cv = products(r0)
            conv = _conv_flat(ext_ref, cv, cw, K_A, r0)
            oa_ref[r0:r0 + ch, :] = (zb * conv).astype(BF16)
        carry_ref[...] = ext_ref[tm:, :]
        tailp_ref[0] = ext_ref[tm:, :]

    @pl.when(m >= n_prompt_tiles)
    def _():
        for c in range(n_chunks):
            r0 = c * ch
            zb, cv = products(r0)
            conv = _conv_grouped(ext3_ref, cv, hdr_ref[r0:r0 + ch, :], cw, K_A, r0 // SUBLANES)
            oa_ref[r0:r0 + ch, :] = (zb * conv).astype(BF16)
            tails_ref[r0:r0 + ch, :] = cv


def _mixer_a(u, um, w_in, conv_w, hdr_s, *, tm, width, n_seq, tiles_per_seq, n_prompt_tiles, n_tiles):
    d = u.shape[1]
    nb = width // CB
    n_meta = um.shape[0]
    last_seq = n_seq - 1

    def wspec(seg):
        return pl.BlockSpec((d, CB), lambda j, m, seg=seg: (0, seg * nb + j))

    return pl.pallas_call(
        functools.partial(_mixer_a_kernel, tm=tm, tiles_per_seq=tiles_per_seq,
                          n_prompt_tiles=n_prompt_tiles, n_meta=n_meta),
        grid=(nb, n_tiles),
        in_specs=[
            pl.BlockSpec((tm, d), lambda j, m: (m, 0)),
            pl.BlockSpec(um.shape, lambda j, m: (0, 0)),
            wspec(0), wspec(1), wspec(2),
            pl.BlockSpec((K_A, CB), lambda j, m: (0, j)),
            pl.BlockSpec((tm, CB), lambda j, m: (jnp.maximum(m - n_prompt_tiles, 0), j)),
        ],
        out_specs=[
            pl.BlockSpec((tm, CB), lambda j, m: (m, j)),
            pl.BlockSpec((1, SUBLANES, CB),
                         lambda j, m: (jnp.minimum(m // tiles_per_seq, last_seq), 0, j)),
            pl.BlockSpec((tm, CB), lambda j, m: (jnp.maximum(m - n_prompt_tiles, 0), j)),
        ],
        out_shape=[
            jax.ShapeDtypeStruct((n_tiles * tm, width), BF16),
            jax.ShapeDtypeStruct((n_seq, SUBLANES, width), F32),
            jax.ShapeDtypeStruct(hdr_s.shape, F32),
        ],
        scratch_shapes=[
            pltpu.VMEM((3, d, CB), BF16),
            pltpu.VMEM((tm + SUBLANES, CB), F32),
            pltpu.VMEM((tm // SUBLANES, 2 * SUBLANES, CB), F32),
            pltpu.VMEM((SUBLANES, CB), F32),
            pltpu.VMEM((SUBLANES, CB), F32),
        ],
        compiler_params=_params(2),
        name="mixer_a",
    )(u, um, w_in, w_in, w_in, conv_w, hdr_s)


def _softplus(x):
    return jnp.maximum(x, 0.0) + jnp.log1p(jnp.exp(-jnp.abs(x)))


def _lru_kernel(u_ref, um_ref, wx_ref, wy_ref, cw_ref, cbias_ref, wa_ref, ba_ref, wi_ref, bi_ref, lam_ref,
                hdr_ref, h0_ref,
                or_ref, tailp_ref, tails_ref, hp_ref, hs_ref,
                wbf_ref, gbf_ref, ext_ref, ext3_ref, carry_ref, hcarry_ref,
                mhdr_ref, mh_ref,
                *, tm, tiles_per_seq, n_prompt_tiles, n_meta, lru_block):
    m = pl.program_id(1)
    cw = cw_ref[...]
    cbias = cbias_ref[...]
    heads = CB // lru_block
    ch = MIX_CHUNK
    n_chunks = tm // ch
    row8 = lax.broadcasted_iota(I32, (SUBLANES, CB), 0)

    def gate_terms(xc):
        xb = xc.astype(BF16)
        ra, ri = [], []
        for hh in range(heads):
            xh = xb[:, hh * lru_block:(hh + 1) * lru_block]
            ra.append(_dot(xh, gbf_ref[0, hh]))
            ri.append(_dot(xh, gbf_ref[1, hh]))
        r = jax.nn.sigmoid(jnp.concatenate(ra, axis=-1) + ba_ref[...])
        i = jax.nn.sigmoid(jnp.concatenate(ri, axis=-1) + bi_ref[...])
        log_a = -C_RG * r * _softplus(-lam_ref[...])
        a = jnp.exp(log_a)
        b = jnp.sqrt((1.0 + a * a) * jnp.tanh(-log_a)) * i * xc
        return a, b

    def scan(a, b, h_prev, h0_row=None):
        out = []
        for g in range(a.shape[0] // SUBLANES):
            a8 = a[g * SUBLANES:(g + 1) * SUBLANES, :]
            b8 = b[g * SUBLANES:(g + 1) * SUBLANES, :]
            for dist in (1, 2, 4):
                keep = row8 >= dist
                b8 = jnp.where(keep, a8 * pltpu.roll(b8, dist, 0) + b8, b8)
                a8 = jnp.where(keep, a8 * pltpu.roll(a8, dist, 0), a8)
            h8 = a8 * (h_prev if h0_row is None else h0_row(g)) + b8
            out.append(h8)
            h_prev = h8[SUBLANES - 1:SUBLANES, :]
        return jnp.concatenate(out, axis=0), h_prev

    @pl.when(m == 0)
    def _():
        wbf_ref[0] = wx_ref[...].astype(BF16)
        wbf_ref[1] = wy_ref[...].astype(BF16)
        gbf_ref[0] = wa_ref[...].astype(BF16)
        gbf_ref[1] = wi_ref[...].astype(BF16)
        zx_meta = _dot(um_ref[...], wbf_ref[0])
        ext_ref[0:SUBLANES, :] = jnp.zeros((SUBLANES, CB), F32)
        a, b = gate_terms(_conv_flat(ext_ref, zx_meta, cw, K_R, 0) + cbias)
        _, h_meta = scan(a, b, jnp.zeros((1, CB), F32))
        mh_ref[...] = jnp.broadcast_to(h_meta, (SUBLANES, CB))
        mhdr_ref[...] = zx_meta[n_meta - SUBLANES:, :]

    def products(r0):
        u = u_ref[r0:r0 + ch, :]
        return _dot(u, wbf_ref[0]), _dot(u, wbf_ref[1])

    @pl.when(m < n_prompt_tiles)
    def _():
        @pl.when((m % tiles_per_seq) == 0)
        def _():
            carry_ref[...] = mhdr_ref[...]
            hcarry_ref[...] = mh_ref[...]

        ext_ref[0:SUBLANES, :] = carry_ref[...]
        h_prev = hcarry_ref[0:1, :]
        for c in range(n_chunks):
            r0 = c * ch
            zx, zy = products(r0)
            a, b = gate_terms(_conv_flat(ext_ref, zx, cw, K_R, r0) + cbias)
            h, h_prev = scan(a, b, h_prev)
            or_ref[r0:r0 + ch, :] = (h * jax.nn.gelu(zy)).astype(BF16)
        hcarry_ref[0:1, :] = h_prev
        carry_ref[...] = ext_ref[tm:, :]
        tailp_ref[0] = ext_ref[tm:, :]
        hp_ref[0] = h[ch - SUBLANES:, :]

    @pl.when(m >= n_prompt_tiles)
    def _():
        for c in range(n_chunks):
            r0 = c * ch
            s0 = r0 // SUBLANES
            zx, zy = products(r0)
            xc = _conv_grouped(ext3_ref, zx, hdr_ref[r0:r0 + ch, :], cw, K_R, s0) + cbias
            a, b = gate_terms(xc)
            h, _ = scan(a, b, None, lambda g, s0=s0: h0_ref[s0 + g:s0 + g + 1, :])
            or_ref[r0:r0 + ch, :] = (h * jax.nn.gelu(zy)).astype(BF16)
            tails_ref[r0:r0 + ch, :] = zx
            hs_ref[r0:r0 + ch, :] = h


def _mixer_r(u, um, w_in, conv_w, conv_b, lru_wa, lru_ba, lru_wx, lru_bx, lru_lam, hdr_s, h0_s,
             *, tm, width, seg0, n_seq, tiles_per_seq, n_prompt_tiles, n_tiles):
    d = u.shape[1]
    nb = width // CB
    n_meta = um.shape[0]
    last_seq = n_seq - 1
    lru_block = lru_wa.shape[-1]
    heads = CB // lru_block
    seqs_per_tile = tm // SUBLANES

    def wspec(seg):
        return pl.BlockSpec((d, CB), lambda j, m, seg=seg: (0, seg * nb + j))

    def vec():
        return pl.BlockSpec((1, CB), lambda j, m: (0, j))

    def gspec():
        return pl.BlockSpec((heads, lru_block, lru_block), lambda j, m: (j, 0, 0))

    def sample_rows():
        return pl.BlockSpec((tm, CB), lambda j, m: (jnp.maximum(m - n_prompt_tiles, 0), j))

    def seq_tail():
        return pl.BlockSpec((1, SUBLANES, CB),
                            lambda j, m: (jnp.minimum(m // tiles_per_seq, last_seq), 0, j))

    return pl.pallas_call(
        functools.partial(_lru_kernel, tm=tm, tiles_per_seq=tiles_per_seq, n_prompt_tiles=n_prompt_tiles,
                          n_meta=n_meta, lru_block=lru_block),
        grid=(nb, n_tiles),
        in_specs=[
            pl.BlockSpec((tm, d), lambda j, m: (m, 0)),
            pl.BlockSpec(um.shape, lambda j, m: (0, 0)),
            wspec(seg0), wspec(seg0 + 1),
            pl.BlockSpec((K_R, CB), lambda j, m: (0, j)),
            vec(),
            gspec(), vec(), gspec(), vec(), vec(),
            sample_rows(),
            pl.BlockSpec((seqs_per_tile, CB), lambda j, m: (jnp.maximum(m - n_prompt_tiles, 0), j)),
        ],
        out_specs=[
            pl.BlockSpec((tm, CB), lambda j, m: (m, j)),
            seq_tail(), sample_rows(), seq_tail(), sample_rows(),
        ],
        out_shape=[
            jax.ShapeDtypeStruct((n_tiles * tm, width), BF16),
            jax.ShapeDtypeStruct((n_seq, SUBLANES, width), F32),
            jax.ShapeDtypeStruct(hdr_s.shape, F32),
            jax.ShapeDtypeStruct((n_seq, SUBLANES, width), F32),
            jax.ShapeDtypeStruct(hdr_s.shape, F32),
        ],
        scratch_shapes=[
            pltpu.VMEM((2, d, CB), BF16),
            pltpu.VMEM((2, heads, lru_block, lru_block), BF16),
            pltpu.VMEM((tm + SUBLANES, CB), F32),
            pltpu.VMEM((seqs_per_tile, 2 * SUBLANES, CB), F32),
            pltpu.VMEM((SUBLANES, CB), F32),
            pltpu.VMEM((SUBLANES, CB), F32),
            pltpu.VMEM((SUBLANES, CB), F32),
            pltpu.VMEM((SUBLANES, CB), F32),
        ],
        compiler_params=_params(2),
        name="mixer_r",
    )(u, um, w_in, w_in, conv_w, conv_b, lru_wa, lru_ba, lru_wx, lru_bx, lru_lam, hdr_s, h0_s)


def _merge_kernel(u_ref, oa_ref, or_ref, wga_ref, wgr_ref, wba_ref, wbr_ref, mg_ref, wg_bf, wb_bf):
    @pl.when(pl.program_id(1) == 0)
    def _():
        wg_bf[0] = wga_ref[...].astype(BF16)
        wg_bf[1] = wgr_ref[...].astype(BF16)
        wb_bf[0] = wba_ref[...].astype(BF16)
        wb_bf[1] = wbr_ref[...].astype(BF16)

    u = u_ref[...]
    ga = jax.nn.sigmoid(_dot(u, wg_bf[0]))
    gr = jax.nn.sigmoid(_dot(u, wg_bf[1]))
    mg = ga * _dot(oa_ref[...], wb_bf[0]) + gr * _dot(or_ref[...], wb_bf[1])
    mg_ref[...] = mg.astype(BF16)


def _merge(u, out_a, out_r, w_in, w_br_a, w_br_r, *, seg_cols, n_tiles):
    d = u.shape[1]
    width = out_a.shape[1]
    nb = d // CB
    seg_a = seg_cols // CB

    return pl.pallas_call(
        _merge_kernel,
        grid=(nb, n_tiles),
        in_specs=[
            pl.BlockSpec((TM, d), lambda j, m: (m, 0)),
            pl.BlockSpec((TM, width), lambda j, m: (m, 0)),
            pl.BlockSpec((TM, width), lambda j, m: (m, 0)),
            pl.BlockSpec((d, CB), lambda j, m: (0, seg_a + j)),
            pl.BlockSpec((d, CB), lambda j, m: (0, seg_a + nb + j)),
            pl.BlockSpec((width, CB), lambda j, m: (0, j)),
            pl.BlockSpec((width, CB), lambda j, m: (0, j)),
        ],
        out_specs=pl.BlockSpec((TM, CB), lambda j, m: (m, j)),
        out_shape=jax.ShapeDtypeStruct((n_tiles * TM, d), BF16),
        scratch_shapes=[pltpu.VMEM((2, d, CB), BF16), pltpu.VMEM((2, width, CB), BF16)],
        compiler_params=_params(2),
        name="merge",
    )(u, out_a, out_r, w_in, w_in, w_br_a, w_br_r)


def _oproj_kernel(mg_ref, xp_ref, xs_ref, wo_ref, x1_ref, wbf_ref, *, n_prompt_tiles):
    m = pl.program_id(1)

    @pl.when(m == 0)
    def _():
        wbf_ref[...] = wo_ref[...].astype(BF16)

    y = _dot(mg_ref[...], wbf_ref[...])

    @pl.when(m < n_prompt_tiles)
    def _():
        x1_ref[...] = xp_ref[...] + y

    @pl.when(m >= n_prompt_tiles)
    def _():
        x1_ref[...] = xs_ref[...] + y


def _oproj(mg, xp, xs, w_o, *, n_prompt_tiles, n_tiles):
    d = mg.shape[1]
    cb = 2 * CB
    nb = d // cb
    return pl.pallas_call(
        functools.partial(_oproj_kernel, n_prompt_tiles=n_prompt_tiles),
        grid=(nb, n_tiles),
        in_specs=[
            pl.BlockSpec((TM, d), lambda j, m: (m, 0)),
            pl.BlockSpec((TM, cb), lambda j, m: (jnp.minimum(m, n_prompt_tiles - 1), j)),
            pl.BlockSpec((TM, cb), lambda j, m: (jnp.maximum(m - n_prompt_tiles, 0), j)),
            pl.BlockSpec((d, cb), lambda j, m: (0, j)),
        ],
        out_specs=pl.BlockSpec((TM, cb), lambda j, m: (m, j)),
        out_shape=jax.ShapeDtypeStruct((n_tiles * TM, d), F32),
        scratch_shapes=[pltpu.VMEM((d, cb), BF16)],
        compiler_params=_params(2),
        name="oproj",
    )(mg, xp, xs, w_o)


def _router_kernel(x1_ref, g_ref, wt_ref, bt_ref, xn_ref, eid_ref, wts_ref, *, n_groups):
    xn = _rmsnorm_rows(x1_ref[...], g_ref[...])
    xn_ref[...] = xn
    lg = lax.dot_general(wt_ref[...].astype(BF16), xn.astype(BF16), (((1,), (1,)), ((), ())),
                         preferred_element_type=F32) + bt_ref[...]
    rows = xn.shape[0]
    row8 = lax.broadcasted_iota(I32, (SUBLANES, rows), 0).astype(F32)

    def first_index_of(v, vmax):
        return jnp.min(jnp.where(v == vmax, row8, float(SUBLANES)), axis=0, keepdims=True)

    gl = jnp.where(row8 < float(n_groups), lg[0:SUBLANES, :], -jnp.inf)
    ge = jnp.exp(gl - jnp.max(gl, axis=0, keepdims=True))
    gprob = ge / jnp.sum(ge, axis=0, keepdims=True)
    g_p = jnp.max(gprob, axis=0, keepdims=True)
    g_idx = first_index_of(gprob, g_p)

    esel = jnp.zeros((SUBLANES, rows), F32)
    for k in range(n_groups):
        esel = jnp.where(g_idx == float(k), lg[(k + 1) * SUBLANES:(k + 2) * SUBLANES, :], esel)
    ee = jnp.exp(esel - jnp.max(esel, axis=0, keepdims=True))
    ep = ee / jnp.sum(ee, axis=0, keepdims=True)
    p1 = jnp.max(ep, axis=0, keepdims=True)
    i1 = first_index_of(ep, p1)
    ep_rest = jnp.where(row8 == i1, -1.0, ep)
    p2 = jnp.max(ep_rest, axis=0, keepdims=True)
    i2 = first_index_of(ep_rest, p2)

    den = p1 + p2
    wts_ref[0:1, :] = p1 / den * g_p
    wts_ref[1:2, :] = p2 / den * g_p
    eid_ref[0:1, :] = (g_idx * float(SUBLANES) + i1).astype(I32)
    eid_ref[1:2, :] = (g_idx * float(SUBLANES) + i2).astype(I32)


def _router(x1, g2, wt, bt, *, n_groups):
    m_rows, d = x1.shape
    return pl.pallas_call(
        functools.partial(_router_kernel, n_groups=n_groups),
        grid=(m_rows // TR,),
        in_specs=[
            pl.BlockSpec((TR, d), lambda i: (i, 0)),
            pl.BlockSpec((1, d), lambda i: (0, 0)),
            pl.BlockSpec(wt.shape, lambda i: (0, 0)),
            pl.BlockSpec(bt.shape, lambda i: (0, 0)),
        ],
        out_specs=[
            pl.BlockSpec((TR, d), lambda i: (i, 0)),
            pl.BlockSpec((TOP_K, TR), lambda i: (0, i)),
            pl.BlockSpec((TOP_K, TR), lambda i: (0, i)),
        ],
        out_shape=[
            jax.ShapeDtypeStruct((m_rows, d), F32),
            jax.ShapeDtypeStruct((TOP_K, m_rows), I32),
            jax.ShapeDtypeStruct((TOP_K, m_rows), F32),
        ],
        compiler_params=_params(1),
        name="router",
    )(x1, g2, wt, bt)


INFO_TILE_EXPERT, INFO_NUM_TILES, INFO_VALID_END, INFO_END_TILE, INFO_ROWS = 0, 1, 2, 3, SUBLANES


def _positions_kernel(eid_ref, pos_ref, info_ref, exc_ref, *, n_experts, tile_shift):
    m_rows = eid_ref.shape[1]
    chunk = 2 * LANES
    n_chunks = m_rows // chunk
    tile = 1 << tile_shift
    e_col = lax.broadcasted_iota(I32, (n_experts, chunk), 0)
    upper = (lax.broadcasted_iota(I32, (chunk, chunk), 0)
             < lax.broadcasted_iota(I32, (chunk, chunk), 1)).astype(BF16)

    def onehots(c):
        off = pl.multiple_of(c * chunk, chunk)
        ids = eid_ref[:, pl.ds(off, chunk)]
        return off, [(e_col == ids[k:k + 1, :]) for k in range(TOP_K)]

    def count(c, carry):
        off, hot = onehots(c)
        used = jnp.where(hot[0] | hot[1], 1.0, 0.0)
        exc_ref[:, pl.ds(off, chunk)] = _dot(used.astype(BF16), upper) + carry
        return carry + jnp.sum(used, axis=1, keepdims=True)

    counts = lax.fori_loop(0, n_chunks, count, jnp.zeros((n_experts, 1), F32))

    lane = lax.broadcasted_iota(I32, (n_experts, LANES), 1)
    sub = lax.broadcasted_iota(I32, (n_experts, LANES), 0)

    def to_lanes(col):
        return jnp.sum(jnp.where(lane == sub, col, 0.0), axis=0, keepdims=True)

    n_tiles_e = ((counts.astype(I32) + (tile - 1)) >> tile_shift).astype(F32)
    first_tile = jnp.sum(jnp.where(lane < sub, to_lanes(n_tiles_e), 0.0), axis=1, keepdims=True)
    end_tile = first_tile + n_tiles_e
    first_row = first_tile * float(tile)

    def place(c, _):
        off, hot = onehots(c)
        dest = exc_ref[:, pl.ds(off, chunk)] + first_row
        for k in range(TOP_K):
            pos_ref[k:k + 1, pl.ds(off, chunk)] = jnp.sum(
                jnp.where(hot[k], dest, 0.0), axis=0, keepdims=True).astype(I32)
        return 0

    lax.fori_loop(0, n_chunks, place, 0)

    tile_expert = jnp.sum(jnp.where(end_tile <= lane.astype(F32), 1.0, 0.0), axis=0, keepdims=True)
    num_tiles = jnp.max(end_tile, axis=0, keepdims=True)
    valid_end = to_lanes(first_row + counts)
    info_ref[...] = jnp.zeros(info_ref.shape, I32)
    info_ref[INFO_TILE_EXPERT:INFO_TILE_EXPERT + 1, :] = jnp.minimum(
        tile_expert, float(n_experts - 1)).astype(I32)
    info_ref[INFO_NUM_TILES:INFO_NUM_TILES + 1, :] = jnp.broadcast_to(num_tiles, (1, LANES)).astype(I32)
    info_ref[INFO_VALID_END:INFO_VALID_END + 1, :] = valid_end.astype(I32)
    info_ref[INFO_END_TILE:INFO_END_TILE + 1, :] = to_lanes(end_tile).astype(I32)


def _positions(eid, *, n_experts):
    m_rows = eid.shape[1]
    tile_shift = T_EXP.bit_length() - 1
    return pl.pallas_call(
        functools.partial(_positions_kernel, n_experts=n_experts, tile_shift=tile_shift),
        out_shape=[
            jax.ShapeDtypeStruct((TOP_K, m_rows), I32),
            jax.ShapeDtypeStruct((INFO_ROWS, LANES), I32),
        ],
        scratch_shapes=[pltpu.VMEM((n_experts, m_rows), F32)],
        compiler_params=pltpu.CompilerParams(vmem_limit_bytes=VMEM_LIMIT_BYTES),
        name="positions",
    )(eid)


def _info(info_ref, row, lane):
    return info_ref[row * LANES + lane]


def _tile_state(i, info_ref):
    n_used = _info(info_ref, INFO_NUM_TILES, 0)
    ic = jnp.minimum(i, n_used - 1)
    e = _info(info_ref, INFO_TILE_EXPERT, ic)
    prev = _info(info_ref, INFO_TILE_EXPERT, jnp.maximum(ic - 1, 0))
    first = jnp.logical_or(ic == 0, e != prev)
    n_valid = jnp.minimum(_info(info_ref, INFO_VALID_END, e) - ic * T_EXP, T_EXP)
    return i < n_used, ic, e, first, n_valid


def _stream_expert_weights(i, info_ref, e, first, w_refs, wf_ref, wbf_ref, wsem, slot_ref):
    n_used = _info(info_ref, INFO_NUM_TILES, 0)
    rows = wf_ref.shape[2]
    chunk = min(rows, 512)

    def copies(expert, slot):
        return [pltpu.make_async_copy(w.at[expert], wf_ref.at[slot, j], wsem.at[slot])
                for j, w in enumerate(w_refs)]

    @pl.when(i == 0)
    def _():
        slot_ref[0] = 0
        for cp in copies(e, 0):
            cp.start()

    @pl.when(jnp.logical_and(first, i > 0))
    def _():
        slot_ref[0] = 1 - slot_ref[0]

    @pl.when(first)
    def _():
        slot = slot_ref[0]
        for cp in copies(e, slot):
            cp.wait()
        for j in range(len(w_refs)):
            def cast(c, _, j=j):
                r0 = pl.multiple_of(c * chunk, chunk)
                wbf_ref[j, pl.ds(r0, chunk), :] = wf_ref[slot, j, pl.ds(r0, chunk), :].astype(BF16)
                return 0
            lax.fori_loop(0, rows // chunk, cast, 0)
        next_tile = _info(info_ref, INFO_END_TILE, e)

        @pl.when(next_tile < n_used)
        def _():
            for cp in copies(_info(info_ref, INFO_TILE_EXPERT, next_tile), 1 - slot):
                cp.start()


def _expert_up_kernel(info_ref, pos_ref, xn_ref, wg_ref, wu_ref, h_ref,
                      xbuf_ref, wf_ref, wbf_ref, inv_ref, slot_ref, xsem, wsem, *, m_rows):
    i = pl.program_id(0)
    used, _, e, first, n_valid = _tile_state(i, info_ref)
    n_used = _info(info_ref, INFO_NUM_TILES, 0)

    def gather(tile, slot, n_rows, start):
        n_groups = (n_rows + (SUBLANES - 1)) >> (SUBLANES.bit_length() - 1)
        if start:
            first_row = tile * T_EXP
            last_row = first_row + n_rows - 1

            def body(g, _):
                for j in range(SUBLANES):
                    r = g * SUBLANES + j
                    src = inv_ref[jnp.minimum(first_row + r, last_row)]
                    pltpu.make_async_copy(xn_ref.at[pl.ds(src, 1), :], xbuf_ref.at[slot, pl.ds(r, 1), :],
                                          xsem.at[slot]).start()
                return 0
            lax.fori_loop(0, n_groups, body, 0)
        else:
            for bit in range((T_EXP // SUBLANES).bit_length()):
                n = SUBLANES << bit

                @pl.when((n_groups & (1 << bit)) != 0)
                def _():
                    pltpu.make_async_copy(xn_ref.at[pl.ds(0, n), :], xbuf_ref.at[slot, pl.ds(0, n), :],
                                          xsem.at[slot]).wait()

    @pl.when(i == 0)
    def _():
        def invert(t, _):
            for k in range(TOP_K):
                inv_ref[pos_ref[k * m_rows + t]] = t
            return 0
        lax.fori_loop(0, m_rows, invert, 0, unroll=8)
        xbuf_ref[...] = jnp.zeros(xbuf_ref.shape, F32)
        gather(0, 0, n_valid, True)

    @pl.when(used)
    def _():
        _stream_expert_weights(i, info_ref, e, first, (wg_ref, wu_ref), wf_ref, wbf_ref, wsem, slot_ref)
        slot = i & 1
        gather(i, slot, n_valid, False)

        @pl.when(i + 1 < n_used)
        def _():
            gather(i + 1, 1 - slot, _tile_state(i + 1, info_ref)[4], True)

        x = xbuf_ref[slot].astype(BF16)
        h = jax.nn.silu(_dot(x, wbf_ref[0])) * _dot(x, wbf_ref[1])
        h_ref[...] = h.astype(BF16)

    @pl.when(jnp.logical_not(used))
    def _():
        h_ref[...] = jnp.zeros(h_ref.shape, BF16)


def _expert_up(info_flat, pos_flat, xn, w_gate, w_up, *, n_tiles):
    m_rows, d = xn.shape
    d_exp = w_gate.shape[-1]
    p_rows = n_tiles * T_EXP
    any_spec = pl.BlockSpec(memory_space=pl.ANY)
    return pl.pallas_call(
        functools.partial(_expert_up_kernel, m_rows=m_rows),
        grid_spec=pltpu.PrefetchScalarGridSpec(
            num_scalar_prefetch=2,
            grid=(n_tiles,),
            in_specs=[any_spec, any_spec, any_spec],
            out_specs=pl.BlockSpec((T_EXP, d_exp), lambda i, info, pos: (i, 0)),
            scratch_shapes=[
                pltpu.VMEM((2, T_EXP, d), F32),
                pltpu.VMEM((2, 2, d, d_exp), F32),
                pltpu.VMEM((2, d, d_exp), BF16),
                pltpu.SMEM((p_rows,), I32),
                pltpu.SMEM((1,), I32),
                pltpu.SemaphoreType.DMA((2,)),
                pltpu.SemaphoreType.DMA((2,)),
            ],
        ),
        out_shape=jax.ShapeDtypeStruct((p_rows, d_exp), BF16),
        compiler_params=_params(1),
        name="expert_up",
    )(info_flat, pos_flat, xn, w_gate, w_up)


def _expert_down_kernel(info_ref, h_ref, wd_ref, y_ref, wf_ref, wbf_ref, slot_ref, wsem):
    i = pl.program_id(0)
    used, _, e, first, _ = _tile_state(i, info_ref)

    @pl.when(used)
    def _():
        _stream_expert_weights(i, info_ref, e, first, (wd_ref,), wf_ref, wbf_ref, wsem, slot_ref)
        y_ref[...] = _dot(h_ref[...], wbf_ref[0])

    @pl.when(jnp.logical_not(used))
    def _():
        y_ref[...] = jnp.zeros(y_ref.shape, F32)


def _expert_down(info_flat, h, w_down, *, n_tiles):
    p_rows, d_exp = h.shape
    d = w_down.shape[-1]
    return pl.pallas_call(
        _expert_down_kernel,
        grid_spec=pltpu.PrefetchScalarGridSpec(
            num_scalar_prefetch=1,
            grid=(n_tiles,),
            in_specs=[
                pl.BlockSpec((T_EXP, d_exp), lambda i, info: (_tile_state(i, info)[1], 0)),
                pl.BlockSpec(memory_space=pl.ANY),
            ],
            out_specs=pl.BlockSpec((T_EXP, d), lambda i, info: (i, 0)),
            scratch_shapes=[
                pltpu.VMEM((2, 1, d_exp, d), F32),
                pltpu.VMEM((1, d_exp, d), BF16),
                pltpu.SMEM((1,), I32),
                pltpu.SemaphoreType.DMA((2,)),
            ],
        ),
        out_shape=jax.ShapeDtypeStruct((p_rows, d), F32),
        compiler_params=_params(1),
        name="expert_down",
    )(info_flat, h, w_down)


def _combine_kernel(pos_ref, x1_ref, w_ref, g_ref, y_ref, out_ref, ybuf_ref, sem, *, m_rows, row0, n_steps):
    i = pl.program_id(0)

    def gather(step, start):
        base = row0 + step * TR
        slot = step & 1

        if not start:
            for k in range(TOP_K):
                pltpu.make_async_copy(y_ref.at[pl.ds(0, TR), :], ybuf_ref.at[slot, k], sem.at[slot]).wait()
            return

        def body(r, _):
            for k in range(TOP_K):
                src = pos_ref[k * m_rows + base + r]
                pltpu.make_async_copy(y_ref.at[pl.ds(src, 1), :], ybuf_ref.at[slot, k, pl.ds(r, 1), :],
                                      sem.at[slot]).start()
            return 0
        lax.fori_loop(0, TR, body, 0, unroll=8)

    @pl.when(i == 0)
    def _():
        gather(0, True)

    @pl.when(i + 1 < n_steps)
    def _():
        gather(i + 1, True)

    gather(i, False)
    slot = i & 1
    w = w_ref[...]
    moe = w[:, 0:1] * ybuf_ref[slot, 0] + w[:, 1:2] * ybuf_ref[slot, 1]
    out_ref[...] = _rmsnorm_rows(x1_ref[...] + moe, g_ref[...])


def _combine(pos_flat, x1, wts_t, norm_f, y, *, row0, n_rows):
    m_rows, d = x1.shape
    tile0 = row0 // TR
    n_steps = n_rows // TR
    return pl.pallas_call(
        functools.partial(_combine_kernel, m_rows=m_rows, row0=row0, n_steps=n_steps),
        grid_spec=pltpu.PrefetchScalarGridSpec(
            num_scalar_prefetch=1,
            grid=(n_steps,),
            in_specs=[
                pl.BlockSpec((TR, d), lambda i, pos: (tile0 + i, 0)),
                pl.BlockSpec((TR, TOP_K), lambda i, pos: (tile0 + i, 0)),
                pl.BlockSpec((1, d), lambda i, pos: (0, 0)),
                pl.BlockSpec(memory_space=pl.ANY),
            ],
            out_specs=pl.BlockSpec((TR, d), lambda i, pos: (i, 0)),
            scratch_shapes=[pltpu.VMEM((2, TOP_K, TR, d), F32), pltpu.SemaphoreType.DMA((2,))],
        ),
        out_shape=jax.ShapeDtypeStruct((n_rows, d), F32),
        compiler_params=_params(1),
        name="combine",
    )(pos_flat, x1, wts_t, norm_f, y)


def _sample_headers(state, k_width):
    n, _, w = state.shape
    padded = jnp.pad(state, ((0, 0), (SUBLANES - (k_width - 1), 0), (0, 0)))
    return padded.reshape(n * SUBLANES, w)


def kernel(x_prompt, x_sample, state_conv_a, state_conv_r, state_h, meta_tokens, norm1, w_in, conv_a_w, conv_r_w,
           conv_r_b, lru_wa, lru_ba, lru_wx, lru_bx, lru_lam, w_br_a, w_br_r, w_o, norm2, w_group, b_group,
           w_router, b_router, w_gate, w_up, w_down, norm_f):
    batch, seq, d = x_prompt.shape
    dec_batch, dec_seq, _ = x_sample.shape
    depth = norm1.shape[0]
    width = w_br_a.shape[1]
    n_groups = w_group.shape[-1]
    n_experts = w_router.shape[-1]
    n_meta = meta_tokens.shape[0]
    assert depth == 1, "meta rows are folded into an initial state, which only carries one layer"
    assert dec_seq == SUBLANES and n_meta % SUBLANES == 0 and n_meta >= SUBLANES
    assert seq % TM == 0 and (dec_batch * dec_seq) % TM == 0
    assert n_groups <= SUBLANES and n_experts == n_groups * SUBLANES
    assert width % CB == 0 and d % (2 * CB) == 0 and CB % lru_wa.shape[-1] == 0

    n_prompt_rows = batch * seq
    n_sample_rows = dec_batch * dec_seq
    m_rows = n_prompt_rows + n_sample_rows
    n_prompt_tiles = n_prompt_rows // TM
    n_tiles = m_rows // TM
    tiles_per_seq = seq // TM
    n_exp_tiles = (TOP_K * m_rows) // T_EXP + n_experts
    assert n_exp_tiles <= LANES

    xp = x_prompt.reshape(n_prompt_rows, d)
    xs = x_sample.reshape(n_sample_rows, d)
    row = lambda v: v.reshape(1, -1)

    u, um = _norm1(xp, xs, meta_tokens, row(norm1[0]), n_prompt_tiles, n_tiles)

    def mixer_geometry(tm):
        assert seq % tm == 0 and n_sample_rows % tm == 0 and tm % MIX_CHUNK == 0 and MIX_CHUNK % SUBLANES == 0
        return dict(tm=tm, width=width, n_seq=batch, tiles_per_seq=seq // tm,
                    n_prompt_tiles=n_prompt_rows // tm, n_tiles=m_rows // tm)

    out_a, tail_a_p, tail_a_s = _mixer_a(
        u, um, w_in[0], conv_a_w[0], _sample_headers(state_conv_a[0], K_A), **mixer_geometry(TM_MIX_A))
    out_r, tail_r_p, tail_r_s, h_p, h_s = _mixer_r(
        u, um, w_in[0], conv_r_w[0], row(conv_r_b[0]), lru_wa[0], row(lru_ba[0]), lru_wx[0], row(lru_bx[0]),
        row(lru_lam[0]), _sample_headers(state_conv_r[0], K_R), state_h[0], seg0=3, **mixer_geometry(TM_MIX_R))

    merged = _merge(u, out_a, out_r, w_in[0], w_br_a[0], w_br_r[0], seg_cols=5 * width, n_tiles=n_tiles)
    x1 = _oproj(merged, xp, xs, w_o[0], n_prompt_tiles=n_prompt_tiles, n_tiles=n_tiles)

    wt = jnp.concatenate([jnp.pad(w_group[0].T, ((0, SUBLANES - n_groups), (0, 0))), w_router[0].T], axis=0)
    bt = jnp.concatenate([jnp.pad(b_group[0], (0, SUBLANES - n_groups)), b_router[0]]).reshape(-1, 1)
    xn, eid, wts = _router(x1, row(norm2[0]), wt, bt, n_groups=n_groups)

    pos, info = _positions(eid, n_experts=n_experts)
    pos_flat = pos.reshape(-1)
    info_flat = info.reshape(-1)
    hid = _expert_up(info_flat, pos_flat, xn, w_gate[0], w_up[0], n_tiles=n_exp_tiles)
    y = _expert_down(info_flat, hid, w_down[0], n_tiles=n_exp_tiles)

    wts_t = wts.T
    g_f = row(norm_f)
    y_prompt = _combine(pos_flat, x1, wts_t, g_f, y, row0=0, n_rows=n_prompt_rows)
    y_sample = _combine(pos_flat, x1, wts_t, g_f, y, row0=n_prompt_rows, n_rows=n_sample_rows)

    def sample_tail(t, k):
        return t.reshape(dec_batch, SUBLANES, width)[:, SUBLANES - k:, :][None]

    return (
        y_prompt.reshape(batch, seq, d),
        y_sample.reshape(dec_batch, dec_seq, d),
        tail_a_p[:, SUBLANES - (K_A - 1):, :][None],
        tail_r_p[:, SUBLANES - (K_R - 1):, :][None],
        h_p[:, SUBLANES - 1, :][None],
        sample_tail(tail_a_s, K_A - 1),
        sample_tail(tail_r_s, K_R - 1),
        sample_tail(h_s, 1)[:, :, 0, :],
    )
```

```python
import functools

import jax
import jax.numpy as jnp
from jax import lax
from jax.experimental import pallas as pl
from jax.experimental.pallas import tpu as pltpu

F32, BF16, I32 = jnp.float32, jnp.bfloat16, jnp.int32

EPS = 1e-6
C_RG = 8.0
K_A = 3
K_R = 4
LRU_HEADS = 16
TOP_K = 2

SUBLANES = 8
LANES = 128
VMEM_LIMIT_BYTES = 56 * 1024 * 1024

TM = 512
TM_OPROJ = 1024
TM_MIX_A = 512
TM_MIX_R = 1024
MIX_CHUNK = 256
CB = 256
T_EXP = 256
TR = 256


def _dot(a, b):
    return jnp.dot(a, b, preferred_element_type=F32)


def _rmsnorm_rows(x, g):
    y = x * lax.rsqrt(jnp.mean(x * x, axis=-1, keepdims=True) + EPS)
    return y * g


def _params(n_axes):
    return pltpu.CompilerParams(dimension_semantics=("arbitrary",) * n_axes,
                                vmem_limit_bytes=VMEM_LIMIT_BYTES)


def _norm1_kernel(xp_ref, xs_ref, meta_ref, g_ref, u_ref, um_ref, *, n_prompt_tiles):
    i = pl.program_id(0)
    g = g_ref[...]

    @pl.when(i < n_prompt_tiles)
    def _():
        u_ref[...] = _rmsnorm_rows(xp_ref[...], g).astype(BF16)

    @pl.when(i >= n_prompt_tiles)
    def _():
        u_ref[...] = _rmsnorm_rows(xs_ref[...], g).astype(BF16)

    @pl.when(i == 0)
    def _():
        um_ref[...] = _rmsnorm_rows(meta_ref[...], g).astype(BF16)


def _norm1(xp, xs, meta, g, n_prompt_tiles, n_tiles):
    d = xp.shape[1]
    return pl.pallas_call(
        functools.partial(_norm1_kernel, n_prompt_tiles=n_prompt_tiles),
        grid=(n_tiles,),
        in_specs=[
            pl.BlockSpec((TM, d), lambda i: (jnp.minimum(i, n_prompt_tiles - 1), 0)),
            pl.BlockSpec((TM, d), lambda i: (jnp.maximum(i - n_prompt_tiles, 0), 0)),
            pl.BlockSpec(meta.shape, lambda i: (0, 0)),
            pl.BlockSpec((1, d), lambda i: (0, 0)),
        ],
        out_specs=[
            pl.BlockSpec((TM, d), lambda i: (i, 0)),
            pl.BlockSpec(meta.shape, lambda i: (0, 0)),
        ],
        out_shape=[
            jax.ShapeDtypeStruct((n_tiles * TM, d), BF16),
            jax.ShapeDtypeStruct(meta.shape, BF16),
        ],
        compiler_params=_params(1),
        name="norm1",
    )(xp, xs, meta, g)


def _conv_taps(window, cw, k_width):
    acc = window(0) * cw[0:1, :]
    for k in range(1, k_width):
        acc = acc + window(k) * cw[k:k + 1, :]
    return acc


def _conv_flat(ext_ref, values, cw, k_width, r0):
    rows = values.shape[0]
    ext_ref[SUBLANES + r0:SUBLANES + r0 + rows, :] = values
    base = SUBLANES - (k_width - 1) + r0
    return _conv_taps(lambda k: ext_ref[pl.ds(base + k, rows), :], cw, k_width)


def _conv_grouped(ext3_ref, values, header, cw, k_width, s0):
    rows, cols = values.shape
    n_seq = rows // SUBLANES
    ext3_ref[s0:s0 + n_seq, 0:SUBLANES, :] = header.reshape(n_seq, SUBLANES, cols)
    ext3_ref[s0:s0 + n_seq, SUBLANES:, :] = values.reshape(n_seq, SUBLANES, cols)
    base = SUBLANES - (k_width - 1)
    return _conv_taps(
        lambda k: ext3_ref[s0:s0 + n_seq, base + k:base + k + SUBLANES, :].reshape(rows, cols), cw, k_width)


def _mixer_a_kernel(u_ref, um_ref, wb_ref, wc_ref, wv_ref, cw_ref, hdr_ref,
                    oa_ref, tailp_ref, tails_ref,
                    wbf_ref, ext_ref, ext3_ref, carry_ref, mhdr_ref,
                    *, tm, tiles_per_seq, n_prompt_tiles, n_meta):
    m = pl.program_id(1)
    cw = cw_ref[...]
    ch = MIX_CHUNK
    n_chunks = tm // ch

    @pl.when(m == 0)
    def _():
        wbf_ref[0] = wb_ref[...].astype(BF16)
        wbf_ref[1] = wc_ref[...].astype(BF16)
        wbf_ref[2] = wv_ref[...].astype(BF16)
        um = um_ref[...]
        cv_meta = _dot(um, wbf_ref[1]) * _dot(um, wbf_ref[2])
        mhdr_ref[...] = cv_meta[n_meta - SUBLANES:, :]

    def products(r0):
        u = u_ref[r0:r0 + ch, :]
        return _dot(u, wbf_ref[0]), _dot(u, wbf_ref[1]) * _dot(u, wbf_ref[2])

    @pl.when(m < n_prompt_tiles)
    def _():
        @pl.when((m % tiles_per_seq) == 0)
        def _():
            carry_ref[...] = mhdr_ref[...]

        ext_ref[0:SUBLANES, :] = carry_ref[...]
        for c in range(n_chunks):
            r0 = c * ch
            zb, cv = products(r0)
            conv = _conv_flat(ext_ref, cv, cw, K_A, r0)
            oa_ref[r0:r0 + ch, :] = (zb * conv).astype(BF16)
        carry_ref[...] = ext_ref[tm:, :]
        tailp_ref[0] = ext_ref[tm:, :]

    @pl.when(m >= n_prompt_tiles)
    def _():
        for c in range(n_chunks):
            r0 = c * ch
            zb, cv = products(r0)
            conv = _conv_grouped(ext3_ref, cv, hdr_ref[r0:r0 + ch, :], cw, K_A, r0 // SUBLANES)
            oa_ref[r0:r0 + ch, :] = (zb * conv).astype(BF16)
            tails_ref[r0:r0 + ch, :] = cv


def _mixer_a(u, um, w_in, conv_w, hdr_s, *, tm, width, n_seq, tiles_per_seq, n_prompt_tiles, n_tiles):
    d = u.shape[1]
    nb = width // CB
    n_meta = um.shape[0]
    last_seq = n_seq - 1

    def wspec(seg):
        return pl.BlockSpec((d, CB), lambda j, m, seg=seg: (0, seg * nb + j))

    return pl.pallas_call(
        functools.partial(_mixer_a_kernel, tm=tm, tiles_per_seq=tiles_per_seq,
                          n_prompt_tiles=n_prompt_tiles, n_meta=n_meta),
        grid=(nb, n_tiles),
        in_specs=[
            pl.BlockSpec((tm, d), lambda j, m: (m, 0)),
            pl.BlockSpec(um.shape, lambda j, m: (0, 0)),
            wspec(0), wspec(1), wspec(2),
            pl.BlockSpec((K_A, CB), lambda j, m: (0, j)),
            pl.BlockSpec((tm, CB), lambda j, m: (jnp.maximum(m - n_prompt_tiles, 0), j)),
        ],
        out_specs=[
            pl.BlockSpec((tm, CB), lambda j, m: (m, j)),
            pl.BlockSpec((1, SUBLANES, CB),
                         lambda j, m: (jnp.minimum(m // tiles_per_seq, last_seq), 0, j)),
            pl.BlockSpec((tm, CB), lambda j, m: (jnp.maximum(m - n_prompt_tiles, 0), j)),
        ],
        out_shape=[
            jax.ShapeDtypeStruct((n_tiles * tm, width), BF16),
            jax.ShapeDtypeStruct((n_seq, SUBLANES, width), F32),
            jax.ShapeDtypeStruct(hdr_s.shape, F32),
        ],
        scratch_shapes=[
            pltpu.VMEM((3, d, CB), BF16),
            pltpu.VMEM((tm + SUBLANES, CB), F32),
            pltpu.VMEM((tm // SUBLANES, 2 * SUBLANES, CB), F32),
            pltpu.VMEM((SUBLANES, CB), F32),
            pltpu.VMEM((SUBLANES, CB), F32),
        ],
        compiler_params=_params(2),
        name="mixer_a",
    )(u, um, w_in, w_in, w_in, conv_w, hdr_s)


def _softplus(x):
    return jnp.maximum(x, 0.0) + jnp.log1p(jnp.exp(-jnp.abs(x)))


def _lru_kernel(u_ref, um_ref, wx_ref, wy_ref, cw_ref, cbias_ref, wa_ref, ba_ref, wi_ref, bi_ref, lam_ref,
                hdr_ref, h0_ref,
                or_ref, tailp_ref, tails_ref, hp_ref, hs_ref,
                wbf_ref, gbf_ref, ext_ref, ext3_ref, carry_ref, hcarry_ref,
                mhdr_ref, mh_ref,
                *, tm, tiles_per_seq, n_prompt_tiles, n_meta, lru_block):
    m = pl.program_id(1)
    cw = cw_ref[...]
    cbias = cbias_ref[...]
    heads = CB // lru_block
    ch = MIX_CHUNK
    n_chunks = tm // ch
    row8 = lax.broadcasted_iota(I32, (SUBLANES, CB), 0)

    def gate_terms(xc):
        xb = xc.astype(BF16)
        ra, ri = [], []
        for hh in range(heads):
            xh = xb[:, hh * lru_block:(hh + 1) * lru_block]
            ra.append(_dot(xh, gbf_ref[0, hh]))
            ri.append(_dot(xh, gbf_ref[1, hh]))
        r = jax.nn.sigmoid(jnp.concatenate(ra, axis=-1) + ba_ref[...])
        i = jax.nn.sigmoid(jnp.concatenate(ri, axis=-1) + bi_ref[...])
        log_a = -C_RG * r * _softplus(-lam_ref[...])
        a = jnp.exp(log_a)
        b = jnp.sqrt((1.0 + a * a) * jnp.tanh(-log_a)) * i * xc
        return a, b

    def scan(a, b, h_prev, h0_row=None):
        out = []
        for g in range(a.shape[0] // SUBLANES):
            a8 = a[g * SUBLANES:(g + 1) * SUBLANES, :]
            b8 = b[g * SUBLANES:(g + 1) * SUBLANES, :]
            for dist in (1, 2, 4):
                keep = row8 >= dist
                b8 = jnp.where(keep, a8 * pltpu.roll(b8, dist, 0) + b8, b8)
                a8 = jnp.where(keep, a8 * pltpu.roll(a8, dist, 0), a8)
            h8 = a8 * (h_prev if h0_row is None else h0_row(g)) + b8
            out.append(h8)
            h_prev = h8[SUBLANES - 1:SUBLANES, :]
        return jnp.concatenate(out, axis=0), h_prev

    @pl.when(m == 0)
    def _():
        wbf_ref[0] = wx_ref[...].astype(BF16)
        wbf_ref[1] = wy_ref[...].astype(BF16)
        gbf_ref[0] = wa_ref[...].astype(BF16)
        gbf_ref[1] = wi_ref[...].astype(BF16)
        zx_meta = _dot(um_ref[...], wbf_ref[0])
        ext_ref[0:SUBLANES, :] = jnp.zeros((SUBLANES, CB), F32)
        a, b = gate_terms(_conv_flat(ext_ref, zx_meta, cw, K_R, 0) + cbias)
        _, h_meta = scan(a, b, jnp.zeros((1, CB), F32))
        mh_ref[...] = jnp.broadcast_to(h_meta, (SUBLANES, CB))
        mhdr_ref[...] = zx_meta[n_meta - SUBLANES:, :]

    def products(r0):
        u = u_ref[r0:r0 + ch, :]
        return _dot(u, wbf_ref[0]), _dot(u, wbf_ref[1])

    @pl.when(m < n_prompt_tiles)
    def _():
        @pl.when((m % tiles_per_seq) == 0)
        def _():
            carry_ref[...] = mhdr_ref[...]
            hcarry_ref[...] = mh_ref[...]

        ext_ref[0:SUBLANES, :] = carry_ref[...]
        h_prev = hcarry_ref[0:1, :]
        for c in range(n_chunks):
            r0 = c * ch
            zx, zy = products(r0)
            a, b = gate_terms(_conv_flat(ext_ref, zx, cw, K_R, r0) + cbias)
            h, h_prev = scan(a, b, h_prev)
            or_ref[r0:r0 + ch, :] = (h * jax.nn.gelu(zy)).astype(BF16)
        hcarry_ref[0:1, :] = h_prev
        carry_ref[...] = ext_ref[tm:, :]
        tailp_ref[0] = ext_ref[tm:, :]
        hp_ref[0] = h[ch - SUBLANES:, :]

    @pl.when(m >= n_prompt_tiles)
    def _():
        for c in range(n_chunks):
            r0 = c * ch
            s0 = r0 // SUBLANES
            zx, zy = products(r0)
            xc = _conv_grouped(ext3_ref, zx, hdr_ref[r0:r0 + ch, :], cw, K_R, s0) + cbias
            a, b = gate_terms(xc)
            h, _ = scan(a, b, None, lambda g, s0=s0: h0_ref[s0 + g:s0 + g + 1, :])
            or_ref[r0:r0 + ch, :] = (h * jax.nn.gelu(zy)).astype(BF16)
            tails_ref[r0:r0 + ch, :] = zx
            hs_ref[r0:r0 + ch, :] = h


def _mixer_r(u, um, w_in, conv_w, conv_b, lru_wa, lru_ba, lru_wx, lru_bx, lru_lam, hdr_s, h0_s,
             *, tm, width, seg0, n_seq, tiles_per_seq, n_prompt_tiles, n_tiles):
    d = u.shape[1]
    nb = width // CB
    n_meta = um.shape[0]
    last_seq = n_seq - 1
    lru_block = lru_wa.shape[-1]
    heads = CB // lru_block
    seqs_per_tile = tm // SUBLANES

    def wspec(seg):
        return pl.BlockSpec((d, CB), lambda j, m, seg=seg: (0, seg * nb + j))

    def vec():
        return pl.BlockSpec((1, CB), lambda j, m: (0, j))

    def gspec():
        return pl.BlockSpec((heads, lru_block, lru_block), lambda j, m: (j, 0, 0))

    def sample_rows():
        return pl.BlockSpec((tm, CB), lambda j, m: (jnp.maximum(m - n_prompt_tiles, 0), j))

    def seq_tail():
        return pl.BlockSpec((1, SUBLANES, CB),
                            lambda j, m: (jnp.minimum(m // tiles_per_seq, last_seq), 0, j))

    return pl.pallas_call(
        functools.partial(_lru_kernel, tm=tm, tiles_per_seq=tiles_per_seq, n_prompt_tiles=n_prompt_tiles,
                          n_meta=n_meta, lru_block=lru_block),
        grid=(nb, n_tiles),
        in_specs=[
            pl.BlockSpec((tm, d), lambda j, m: (m, 0)),
            pl.BlockSpec(um.shape, lambda j, m: (0, 0)),
            wspec(seg0), wspec(seg0 + 1),
            pl.BlockSpec((K_R, CB), lambda j, m: (0, j)),
            vec(),
            gspec(), vec(), gspec(), vec(), vec(),
            sample_rows(),
            pl.BlockSpec((seqs_per_tile, CB), lambda j, m: (jnp.maximum(m - n_prompt_tiles, 0), j)),
        ],
        out_specs=[
            pl.BlockSpec((tm, CB), lambda j, m: (m, j)),
            seq_tail(), sample_rows(), seq_tail(), sample_rows(),
        ],
        out_shape=[
            jax.ShapeDtypeStruct((n_tiles * tm, width), BF16),
            jax.ShapeDtypeStruct((n_seq, SUBLANES, width), F32),
            jax.ShapeDtypeStruct(hdr_s.shape, F32),
            jax.ShapeDtypeStruct((n_seq, SUBLANES, width), F32),
            jax.ShapeDtypeStruct(hdr_s.shape, F32),
        ],
        scratch_shapes=[
            pltpu.VMEM((2, d, CB), BF16),
            pltpu.VMEM((2, heads, lru_block, lru_block), BF16),
            pltpu.VMEM((tm + SUBLANES, CB), F32),
            pltpu.VMEM((seqs_per_tile, 2 * SUBLANES, CB), F32),
            pltpu.VMEM((SUBLANES, CB), F32),
            pltpu.VMEM((SUBLANES, CB), F32),
            pltpu.VMEM((SUBLANES, CB), F32),
            pltpu.VMEM((SUBLANES, CB), F32),
        ],
        compiler_params=_params(2),
        name="mixer_r",
    )(u, um, w_in, w_in, conv_w, conv_b, lru_wa, lru_ba, lru_wx, lru_bx, lru_lam, hdr_s, h0_s)


def _merge_kernel(u_ref, oa_ref, or_ref, wga_ref, wgr_ref, wba_ref, wbr_ref, mg_ref, wg_bf, wb_bf):
    @pl.when(pl.program_id(1) == 0)
    def _():
        wg_bf[0] = wga_ref[...].astype(BF16)
        wg_bf[1] = wgr_ref[...].astype(BF16)
        wb_bf[0] = wba_ref[...].astype(BF16)
        wb_bf[1] = wbr_ref[...].astype(BF16)

    u = u_ref[...]
    ga = jax.nn.sigmoid(_dot(u, wg_bf[0]))
    gr = jax.nn.sigmoid(_dot(u, wg_bf[1]))
    mg = ga * _dot(oa_ref[...], wb_bf[0]) + gr * _dot(or_ref[...], wb_bf[1])
    mg_ref[...] = mg.astype(BF16)


def _merge(u, out_a, out_r, w_in, w_br_a, w_br_r, *, seg_cols, n_tiles):
    d = u.shape[1]
    width = out_a.shape[1]
    nb = d // CB
    seg_a = seg_cols // CB

    return pl.pallas_call(
        _merge_kernel,
        grid=(nb, n_tiles),
        in_specs=[
            pl.BlockSpec((TM, d), lambda j, m: (m, 0)),
            pl.BlockSpec((TM, width), lambda j, m: (m, 0)),
            pl.BlockSpec((TM, width), lambda j, m: (m, 0)),
            pl.BlockSpec((d, CB), lambda j, m: (0, seg_a + j)),
            pl.BlockSpec((d, CB), lambda j, m: (0, seg_a + nb + j)),
            pl.BlockSpec((width, CB), lambda j, m: (0, j)),
            pl.BlockSpec((width, CB), lambda j, m: (0, j)),
        ],
        out_specs=pl.BlockSpec((TM, CB), lambda j, m: (m, j)),
        out_shape=jax.ShapeDtypeStruct((n_tiles * TM, d), BF16),
        scratch_shapes=[pltpu.VMEM((2, d, CB), BF16), pltpu.VMEM((2, width, CB), BF16)],
        compiler_params=_params(2),
        name="merge",
    )(u, out_a, out_r, w_in, w_in, w_br_a, w_br_r)


def _oproj_kernel(mg_ref, xp_ref, xs_ref, wo_ref, x1_ref, wbf_ref, *, n_prompt_tiles):
    m = pl.program_id(1)

    @pl.when(m == 0)
    def _():
        wbf_ref[...] = wo_ref[...].astype(BF16)

    y = _dot(mg_ref[...], wbf_ref[...])

    @pl.when(m < n_prompt_tiles)
    def _():
        x1_ref[...] = xp_ref[...] + y

    @pl.when(m >= n_prompt_tiles)
    def _():
        x1_ref[...] = xs_ref[...] + y


def _oproj(mg, xp, xs, w_o, *, tm):
    m_rows, d = mg.shape
    cb = 2 * CB
    nb = d // cb
    n_prompt_tiles = xp.shape[0] // tm
    n_tiles = m_rows // tm
    assert xp.shape[0] % tm == 0 and xs.shape[0] % tm == 0
    return pl.pallas_call(
        functools.partial(_oproj_kernel, n_prompt_tiles=n_prompt_tiles),
        grid=(nb, n_tiles),
        in_specs=[
            pl.BlockSpec((tm, d), lambda j, m: (m, 0)),
            pl.BlockSpec((tm, cb), lambda j, m: (jnp.minimum(m, n_prompt_tiles - 1), j)),
            pl.BlockSpec((tm, cb), lambda j, m: (jnp.maximum(m - n_prompt_tiles, 0), j)),
            pl.BlockSpec((d, cb), lambda j, m: (0, j)),
        ],
        out_specs=pl.BlockSpec((tm, cb), lambda j, m: (m, j)),
        out_shape=jax.ShapeDtypeStruct((m_rows, d), F32),
        scratch_shapes=[pltpu.VMEM((d, cb), BF16)],
        compiler_params=_params(2),
        name="oproj",
    )(mg, xp, xs, w_o)


def _router_kernel(x1_ref, g_ref, wt_ref, bt_ref, xn_ref, eid_ref, wts_ref, *, n_groups):
    xn = _rmsnorm_rows(x1_ref[...], g_ref[...])
    xn_ref[...] = xn
    lg = lax.dot_general(wt_ref[...].astype(BF16), xn.astype(BF16), (((1,), (1,)), ((), ())),
                         preferred_element_type=F32) + bt_ref[...]
    rows = xn.shape[0]
    row8 = lax.broadcasted_iota(I32, (SUBLANES, rows), 0).astype(F32)

    def first_index_of(v, vmax):
        return jnp.min(jnp.where(v == vmax, row8, float(SUBLANES)), axis=0, keepdims=True)

    gl = jnp.where(row8 < float(n_groups), lg[0:SUBLANES, :], -jnp.inf)
    ge = jnp.exp(gl - jnp.max(gl, axis=0, keepdims=True))
    gprob = ge / jnp.sum(ge, axis=0, keepdims=True)
    g_p = jnp.max(gprob, axis=0, keepdims=True)
    g_idx = first_index_of(gprob, g_p)

    esel = jnp.zeros((SUBLANES, rows), F32)
    for k in range(n_groups):
        esel = jnp.where(g_idx == float(k), lg[(k + 1) * SUBLANES:(k + 2) * SUBLANES, :], esel)
    ee = jnp.exp(esel - jnp.max(esel, axis=0, keepdims=True))
    ep = ee / jnp.sum(ee, axis=0, keepdims=True)
    p1 = jnp.max(ep, axis=0, keepdims=True)
    i1 = first_index_of(ep, p1)
    ep_rest = jnp.where(row8 == i1, -1.0, ep)
    p2 = jnp.max(ep_rest, axis=0, keepdims=True)
    i2 = first_index_of(ep_rest, p2)

    den = p1 + p2
    wts_ref[0:1, :] = p1 / den * g_p
    wts_ref[1:2, :] = p2 / den * g_p
    eid_ref[0:1, :] = (g_idx * float(SUBLANES) + i1).astype(I32)
    eid_ref[1:2, :] = (g_idx * float(SUBLANES) + i2).astype(I32)


def _router(x1, g2, wt, bt, *, n_groups):
    m_rows, d = x1.shape
    return pl.pallas_call(
        functools.partial(_router_kernel, n_groups=n_groups),
        grid=(m_rows // TR,),
        in_specs=[
            pl.BlockSpec((TR, d), lambda i: (i, 0)),
            pl.BlockSpec((1, d), lambda i: (0, 0)),
            pl.BlockSpec(wt.shape, lambda i: (0, 0)),
            pl.BlockSpec(bt.shape, lambda i: (0, 0)),
        ],
        out_specs=[
            pl.BlockSpec((TR, d), lambda i: (i, 0)),
            pl.BlockSpec((TOP_K, TR), lambda i: (0, i)),
            pl.BlockSpec((TOP_K, TR), lambda i: (0, i)),
        ],
        out_shape=[
            jax.ShapeDtypeStruct((m_rows, d), F32),
            jax.ShapeDtypeStruct((TOP_K, m_rows), I32),
            jax.ShapeDtypeStruct((TOP_K, m_rows), F32),
        ],
        compiler_params=_params(1),
        name="router",
    )(x1, g2, wt, bt)


INFO_TILE_EXPERT, INFO_NUM_TILES, INFO_VALID_END, INFO_END_TILE, INFO_ROWS = 0, 1, 2, 3, SUBLANES


def _positions_kernel(eid_ref, pos_ref, info_ref, exc_ref, *, n_experts, tile_shift):
    m_rows = eid_ref.shape[1]
    chunk = 2 * LANES
    n_chunks = m_rows // chunk
    tile = 1 << tile_shift
    e_col = lax.broadcasted_iota(I32, (n_experts, chunk), 0)
    upper = (lax.broadcasted_iota(I32, (chunk, chunk), 0)
             < lax.broadcasted_iota(I32, (chunk, chunk), 1)).astype(BF16)

    def onehots(c):
        off = pl.multiple_of(c * chunk, chunk)
        ids = eid_ref[:, pl.ds(off, chunk)]
        return off, [(e_col == ids[k:k + 1, :]) for k in range(TOP_K)]

    def count(c, carry):
        off, hot = onehots(c)
        used = jnp.where(hot[0] | hot[1], 1.0, 0.0)
        exc_ref[:, pl.ds(off, chunk)] = _dot(used.astype(BF16), upper) + carry
        return carry + jnp.sum(used, axis=1, keepdims=True)

    counts = lax.fori_loop(0, n_chunks, count, jnp.zeros((n_experts, 1), F32))

    lane = lax.broadcasted_iota(I32, (n_experts, LANES), 1)
    sub = lax.broadcasted_iota(I32, (n_experts, LANES), 0)

    def to_lanes(col):
        return jnp.sum(jnp.where(lane == sub, col, 0.0), axis=0, keepdims=True)

    n_tiles_e = ((counts.astype(I32) + (tile - 1)) >> tile_shift).astype(F32)
    first_tile = jnp.sum(jnp.where(lane < sub, to_lanes(n_tiles_e), 0.0), axis=1, keepdims=True)
    end_tile = first_tile + n_tiles_e
    first_row = first_tile * float(tile)

    def place(c, _):
        off, hot = onehots(c)
        dest = exc_ref[:, pl.ds(off, chunk)] + first_row
        for k in range(TOP_K):
            pos_ref[k:k + 1, pl.ds(off, chunk)] = jnp.sum(
                jnp.where(hot[k], dest, 0.0), axis=0, keepdims=True).astype(I32)
        return 0

    lax.fori_loop(0, n_chunks, place, 0)

    tile_expert = jnp.sum(jnp.where(end_tile <= lane.astype(F32), 1.0, 0.0), axis=0, keepdims=True)
    num_tiles = jnp.max(end_tile, axis=0, keepdims=True)
    valid_end = to_lanes(first_row + counts)
    info_ref[...] = jnp.zeros(info_ref.shape, I32)
    info_ref[INFO_TILE_EXPERT:INFO_TILE_EXPERT + 1, :] = jnp.minimum(
        tile_expert, float(n_experts - 1)).astype(I32)
    info_ref[INFO_NUM_TILES:INFO_NUM_TILES + 1, :] = jnp.broadcast_to(num_tiles, (1, LANES)).astype(I32)
    info_ref[INFO_VALID_END:INFO_VALID_END + 1, :] = valid_end.astype(I32)
    info_ref[INFO_END_TILE:INFO_END_TILE + 1, :] = to_lanes(end_tile).astype(I32)


def _positions(eid, *, n_experts):
    m_rows = eid.shape[1]
    tile_shift = T_EXP.bit_length() - 1
    return pl.pallas_call(
        functools.partial(_positions_kernel, n_experts=n_experts, tile_shift=tile_shift),
        out_shape=[
            jax.ShapeDtypeStruct((TOP_K, m_rows), I32),
            jax.ShapeDtypeStruct((INFO_ROWS, LANES), I32),
        ],
        scratch_shapes=[pltpu.VMEM((n_experts, m_rows), F32)],
        compiler_params=pltpu.CompilerParams(vmem_limit_bytes=VMEM_LIMIT_BYTES),
        name="positions",
    )(eid)


def _info(info_ref, row, lane):
    return info_ref[row * LANES + lane]


def _tile_state(i, info_ref):
    n_used = _info(info_ref, INFO_NUM_TILES, 0)
    ic = jnp.minimum(i, n_used - 1)
    e = _info(info_ref, INFO_TILE_EXPERT, ic)
    prev = _info(info_ref, INFO_TILE_EXPERT, jnp.maximum(ic - 1, 0))
    first = jnp.logical_or(ic == 0, e != prev)
    n_valid = jnp.minimum(_info(info_ref, INFO_VALID_END, e) - ic * T_EXP, T_EXP)
    return i < n_used, ic, e, first, n_valid


def _stream_expert_weights(i, info_ref, e, first, w_refs, wf_ref, wbf_ref, wsem, slot_ref):
    n_used = _info(info_ref, INFO_NUM_TILES, 0)
    rows = wf_ref.shape[2]
    chunk = min(rows, 512)

    def copies(expert, slot):
        return [pltpu.make_async_copy(w.at[expert], wf_ref.at[slot, j], wsem.at[slot])
                for j, w in enumerate(w_refs)]

    @pl.when(i == 0)
    def _():
        slot_ref[0] = 0
        for cp in copies(e, 0):
            cp.start()

    @pl.when(jnp.logical_and(first, i > 0))
    def _():
        slot_ref[0] = 1 - slot_ref[0]

    @pl.when(first)
    def _():
        slot = slot_ref[0]
        for cp in copies(e, slot):
            cp.wait()
        for j in range(len(w_refs)):
            def cast(c, _, j=j):
                r0 = pl.multiple_of(c * chunk, chunk)
                wbf_ref[j, pl.ds(r0, chunk), :] = wf_ref[slot, j, pl.ds(r0, chunk), :].astype(BF16)
                return 0
            lax.fori_loop(0, rows // chunk, cast, 0)
        next_tile = _info(info_ref, INFO_END_TILE, e)

        @pl.when(next_tile < n_used)
        def _():
            for cp in copies(_info(info_ref, INFO_TILE_EXPERT, next_tile), 1 - slot):
                cp.start()


def _expert_up_kernel(info_ref, pos_ref, xn_ref, wg_ref, wu_ref, h_ref,
                      xbuf_ref, wf_ref, wbf_ref, inv_ref, slot_ref, xsem, wsem, *, m_rows):
    i = pl.program_id(0)
    used, _, e, first, n_valid = _tile_state(i, info_ref)
    n_used = _info(info_ref, INFO_NUM_TILES, 0)

    def gather_start(tile, slot):
        _, ic, _, _, rows = _tile_state(tile, info_ref)
        first_row = ic * T_EXP
        last_row = first_row + rows - 1
        for r in range(T_EXP):
            src = inv_ref[jnp.minimum(first_row + r, last_row)]
            pltpu.make_async_copy(xn_ref.at[pl.ds(src, 1), :], xbuf_ref.at[slot, pl.ds(r, 1), :],
                                  xsem.at[slot]).start()

    def gather_wait(slot):
        pltpu.make_async_copy(xn_ref.at[pl.ds(0, T_EXP), :], xbuf_ref.at[slot], xsem.at[slot]).wait()

    @pl.when(i == 0)
    def _():
        def invert(t, _):
            for k in range(TOP_K):
                inv_ref[pos_ref[k * m_rows + t]] = t
            return 0
        lax.fori_loop(0, m_rows, invert, 0, unroll=8)
        gather_start(0, 0)

    @pl.when(used)
    def _():
        _stream_expert_weights(i, info_ref, e, first, (wg_ref, wu_ref), wf_ref, wbf_ref, wsem, slot_ref)
        slot = i & 1
        gather_wait(slot)
        x = xbuf_ref[slot].astype(BF16)
        h = jax.nn.silu(_dot(x, wbf_ref[0])) * _dot(x, wbf_ref[1])
        h_ref[...] = h.astype(BF16)
        gather_start(i + 1, 1 - slot)

        @pl.when(i + 1 >= n_used)
        def _():
            gather_wait(1 - slot)

    @pl.when(jnp.logical_not(used))
    def _():
        h_ref[...] = jnp.zeros(h_ref.shape, BF16)


def _expert_up(info_flat, pos_flat, xn, w_gate, w_up, *, n_tiles):
    m_rows, d = xn.shape
    d_exp = w_gate.shape[-1]
    p_rows = n_tiles * T_EXP
    any_spec = pl.BlockSpec(memory_space=pl.ANY)
    return pl.pallas_call(
        functools.partial(_expert_up_kernel, m_rows=m_rows),
        grid_spec=pltpu.PrefetchScalarGridSpec(
            num_scalar_prefetch=2,
            grid=(n_tiles,),
            in_specs=[any_spec, any_spec, any_spec],
            out_specs=pl.BlockSpec((T_EXP, d_exp), lambda i, info, pos: (i, 0)),
            scratch_shapes=[
                pltpu.VMEM((2, T_EXP, d), F32),
                pltpu.VMEM((2, 2, d, d_exp), F32),
                pltpu.VMEM((2, d, d_exp), BF16),
                pltpu.SMEM((p_rows,), I32),
                pltpu.SMEM((1,), I32),
                pltpu.SemaphoreType.DMA((2,)),
                pltpu.SemaphoreType.DMA((2,)),
            ],
        ),
        out_shape=jax.ShapeDtypeStruct((p_rows, d_exp), BF16),
        compiler_params=_params(1),
        name="expert_up",
    )(info_flat, pos_flat, xn, w_gate, w_up)


def _expert_down_kernel(info_ref, h_ref, wd_ref, y_ref, wf_ref, wbf_ref, slot_ref, wsem):
    i = pl.program_id(0)
    used, _, e, first, _ = _tile_state(i, info_ref)

    @pl.when(used)
    def _():
        _stream_expert_weights(i, info_ref, e, first, (wd_ref,), wf_ref, wbf_ref, wsem, slot_ref)
        y_ref[...] = _dot(h_ref[...], wbf_ref[0])

    @pl.when(jnp.logical_not(used))
    def _():
        y_ref[...] = jnp.zeros(y_ref.shape, F32)


def _expert_down(info_flat, h, w_down, *, n_tiles):
    p_rows, d_exp = h.shape
    d = w_down.shape[-1]
    return pl.pallas_call(
        _expert_down_kernel,
        grid_spec=pltpu.PrefetchScalarGridSpec(
            num_scalar_prefetch=1,
            grid=(n_tiles,),
            in_specs=[
                pl.BlockSpec((T_EXP, d_exp), lambda i, info: (_tile_state(i, info)[1], 0)),
                pl.BlockSpec(memory_space=pl.ANY),
            ],
            out_specs=pl.BlockSpec((T_EXP, d), lambda i, info: (i, 0)),
            scratch_shapes=[
                pltpu.VMEM((2, 1, d_exp, d), F32),
                pltpu.VMEM((1, d_exp, d), BF16),
                pltpu.SMEM((1,), I32),
                pltpu.SemaphoreType.DMA((2,)),
            ],
        ),
        out_shape=jax.ShapeDtypeStruct((p_rows, d), F32),
        compiler_params=_params(1),
        name="expert_down",
    )(info_flat, h, w_down)


def _combine_kernel(pos_ref, x1_ref, w_ref, g_ref, y_ref, out_ref, ybuf_ref, sem, *, m_rows, row0, n_steps):
    i = pl.program_id(0)

    def gather_start(step, slot):
        base = row0 + jnp.minimum(step, n_steps - 1) * TR
        for r in range(TR):
            for k in range(TOP_K):
                src = pos_ref[k * m_rows + base + r]
                pltpu.make_async_copy(y_ref.at[pl.ds(src, 1), :], ybuf_ref.at[slot, k, pl.ds(r, 1), :],
                                      sem.at[slot]).start()

    def gather_wait(slot):
        for k in range(TOP_K):
            pltpu.make_async_copy(y_ref.at[pl.ds(0, TR), :], ybuf_ref.at[slot, k], sem.at[slot]).wait()

    @pl.when(i == 0)
    def _():
        gather_start(0, 0)

    slot = i & 1
    gather_wait(slot)
    w = w_ref[...]
    moe = w[:, 0:1] * ybuf_ref[slot, 0] + w[:, 1:2] * ybuf_ref[slot, 1]
    out_ref[...] = _rmsnorm_rows(x1_ref[...] + moe, g_ref[...])
    gather_start(i + 1, 1 - slot)

    @pl.when(i + 1 >= n_steps)
    def _():
        gather_wait(1 - slot)


def _combine(pos_flat, x1, wts_t, norm_f, y, *, row0, n_rows):
    m_rows, d = x1.shape
    tile0 = row0 // TR
    n_steps = n_rows // TR
    return pl.pallas_call(
        functools.partial(_combine_kernel, m_rows=m_rows, row0=row0, n_steps=n_steps),
        grid_spec=pltpu.PrefetchScalarGridSpec(
            num_scalar_prefetch=1,
            grid=(n_steps,),
            in_specs=[
                pl.BlockSpec((TR, d), lambda i, pos: (tile0 + i, 0)),
                pl.BlockSpec((TR, TOP_K), lambda i, pos: (tile0 + i, 0)),
                pl.BlockSpec((1, d), lambda i, pos: (0, 0)),
                pl.BlockSpec(memory_space=pl.ANY),
            ],
            out_specs=pl.BlockSpec((TR, d), lambda i, pos: (i, 0)),
            scratch_shapes=[pltpu.VMEM((2, TOP_K, TR, d), F32), pltpu.SemaphoreType.DMA((2,))],
        ),
        out_shape=jax.ShapeDtypeStruct((n_rows, d), F32),
        compiler_params=_params(1),
        name="combine",
    )(pos_flat, x1, wts_t, norm_f, y)


def _sample_headers(state, k_width):
    n, _, w = state.shape
    padded = jnp.pad(state, ((0, 0), (SUBLANES - (k_width - 1), 0), (0, 0)))
    return padded.reshape(n * SUBLANES, w)


def kernel(x_prompt, x_sample, state_conv_a, state_conv_r, state_h, meta_tokens, norm1, w_in, conv_a_w, conv_r_w,
           conv_r_b, lru_wa, lru_ba, lru_wx, lru_bx, lru_lam, w_br_a, w_br_r, w_o, norm2, w_group, b_group,
           w_router, b_router, w_gate, w_up, w_down, norm_f):
    batch, seq, d = x_prompt.shape
    dec_batch, dec_seq, _ = x_sample.shape
    depth = norm1.shape[0]
    width = w_br_a.shape[1]
    n_groups = w_group.shape[-1]
    n_experts = w_router.shape[-1]
    n_meta = meta_tokens.shape[0]
    assert depth == 1, "meta rows are folded into an initial state, which only carries one layer"
    assert dec_seq == SUBLANES and n_meta % SUBLANES == 0 and n_meta >= SUBLANES
    assert seq % TM == 0 and (dec_batch * dec_seq) % TM == 0
    assert n_groups <= SUBLANES and n_experts == n_groups * SUBLANES
    assert width % CB == 0 and d % (2 * CB) == 0 and CB % lru_wa.shape[-1] == 0

    n_prompt_rows = batch * seq
    n_sample_rows = dec_batch * dec_seq
    m_rows = n_prompt_rows + n_sample_rows
    n_prompt_tiles = n_prompt_rows // TM
    n_tiles = m_rows // TM
    tiles_per_seq = seq // TM
    n_exp_tiles = (TOP_K * m_rows) // T_EXP + n_experts
    assert n_exp_tiles <= LANES

    xp = x_prompt.reshape(n_prompt_rows, d)
    xs = x_sample.reshape(n_sample_rows, d)
    row = lambda v: v.reshape(1, -1)

    u, um = _norm1(xp, xs, meta_tokens, row(norm1[0]), n_prompt_tiles, n_tiles)

    def mixer_geometry(tm):
        assert seq % tm == 0 and n_sample_rows % tm == 0 and tm % MIX_CHUNK == 0 and MIX_CHUNK % SUBLANES == 0
        return dict(tm=tm, width=width, n_seq=batch, tiles_per_seq=seq // tm,
                    n_prompt_tiles=n_prompt_rows // tm, n_tiles=m_rows // tm)

    out_a, tail_a_p, tail_a_s = _mixer_a(
        u, um, w_in[0], conv_a_w[0], _sample_headers(state_conv_a[0], K_A), **mixer_geometry(TM_MIX_A))
    out_r, tail_r_p, tail_r_s, h_p, h_s = _mixer_r(
        u, um, w_in[0], conv_r_w[0], row(conv_r_b[0]), lru_wa[0], row(lru_ba[0]), lru_wx[0], row(lru_bx[0]),
        row(lru_lam[0]), _sample_headers(state_conv_r[0], K_R), state_h[0], seg0=3, **mixer_geometry(TM_MIX_R))

    merged = _merge(u, out_a, out_r, w_in[0], w_br_a[0], w_br_r[0], seg_cols=5 * width, n_tiles=n_tiles)
    x1 = _oproj(merged, xp, xs, w_o[0], tm=TM_OPROJ)

    wt = jnp.concatenate([jnp.pad(w_group[0].T, ((0, SUBLANES - n_groups), (0, 0))), w_router[0].T], axis=0)
    bt = jnp.concatenate([jnp.pad(b_group[0], (0, SUBLANES - n_groups)), b_router[0]]).reshape(-1, 1)
    xn, eid, wts = _router(x1, row(norm2[0]), wt, bt, n_groups=n_groups)

    pos, info = _positions(eid, n_experts=n_experts)
    pos_flat = pos.reshape(-1)
    info_flat = info.reshape(-1)
    hid = _expert_up(info_flat, pos_flat, xn, w_gate[0], w_up[0], n_tiles=n_exp_tiles)
    y = _expert_down(info_flat, hid, w_down[0], n_tiles=n_exp_tiles)

    wts_t = wts.T
    g_f = row(norm_f)
    y_prompt = _combine(pos_flat, x1, wts_t, g_f, y, row0=0, n_rows=n_prompt_rows)
    y_sample = _combine(pos_flat, x1, wts_t, g_f, y, row0=n_prompt_rows, n_rows=n_sample_rows)

    def sample_tail(t, k):
        return t.reshape(dec_batch, SUBLANES, width)[:, SUBLANES - k:, :][None]

    return (
        y_prompt.reshape(batch, seq, d),
        y_sample.reshape(dec_batch, dec_seq, d),
        tail_a_p[:, SUBLANES - (K_A - 1):, :][None],
        tail_r_p[:, SUBLANES - (K_R - 1):, :][None],
        h_p[:, SUBLANES - 1, :][None],
        sample_tail(tail_a_s, K_A - 1),
        sample_tail(tail_r_s, K_R - 1),
        sample_tail(h_s, 1)[:, :, 0, :],
    )
```

```python
import functools

import jax
import jax.numpy as jnp
from jax import lax
from jax.experimental import pallas as pl
from jax.experimental.pallas import tpu as pltpu

F32, BF16, I32 = jnp.float32, jnp.bfloat16, jnp.int32

EPS = 1e-6
C_RG = 8.0
K_A = 3
K_R = 4
LRU_HEADS = 16
TOP_K = 2

SUBLANES = 8
LANES = 128
VMEM_LIMIT_BYTES = 56 * 1024 * 1024

TM = 512
TM_OPROJ = 1024
TM_MIX_A = 512
TM_MIX_R = 1024
MIX_CHUNK = 256
CB = 256
T_EXP = 256
TR = 256


def _dot(a, b):
    return jnp.dot(a, b, preferred_element_type=F32)


def _rmsnorm_rows(x, g):
    y = x * lax.rsqrt(jnp.mean(x * x, axis=-1, keepdims=True) + EPS)
    return y * g


def _params(n_axes):
    return pltpu.CompilerParams(dimension_semantics=("arbitrary",) * n_axes,
                                vmem_limit_bytes=VMEM_LIMIT_BYTES)


def _norm1_kernel(xp_ref, xs_ref, meta_ref, g_ref, u_ref, um_ref, *, n_prompt_tiles):
    i = pl.program_id(0)
    g = g_ref[...]

    @pl.when(i < n_prompt_tiles)
    def _():
        u_ref[...] = _rmsnorm_rows(xp_ref[...], g).astype(BF16)

    @pl.when(i >= n_prompt_tiles)
    def _():
        u_ref[...] = _rmsnorm_rows(xs_ref[...], g).astype(BF16)

    @pl.when(i == 0)
    def _():
        um_ref[...] = _rmsnorm_rows(meta_ref[...], g).astype(BF16)


def _norm1(xp, xs, meta, g, n_prompt_tiles, n_tiles):
    d = xp.shape[1]
    return pl.pallas_call(
        functools.partial(_norm1_kernel, n_prompt_tiles=n_prompt_tiles),
        grid=(n_tiles,),
        in_specs=[
            pl.BlockSpec((TM, d), lambda i: (jnp.minimum(i, n_prompt_tiles - 1), 0)),
            pl.BlockSpec((TM, d), lambda i: (jnp.maximum(i - n_prompt_tiles, 0), 0)),
            pl.BlockSpec(meta.shape, lambda i: (0, 0)),
            pl.BlockSpec((1, d), lambda i: (0, 0)),
        ],
        out_specs=[
            pl.BlockSpec((TM, d), lambda i: (i, 0)),
            pl.BlockSpec(meta.shape, lambda i: (0, 0)),
        ],
        out_shape=[
            jax.ShapeDtypeStruct((n_tiles * TM, d), BF16),
            jax.ShapeDtypeStruct(meta.shape, BF16),
        ],
        compiler_params=_params(1),
        name="norm1",
    )(xp, xs, meta, g)


def _conv_taps(window, cw, k_width):
    acc = window(0) * cw[0:1, :]
    for k in range(1, k_width):
        acc = acc + window(k) * cw[k:k + 1, :]
    return acc


def _conv_flat(ext_ref, values, cw, k_width, r0):
    rows = values.shape[0]
    ext_ref[SUBLANES + r0:SUBLANES + r0 + rows, :] = values
    base = SUBLANES - (k_width - 1) + r0
    return _conv_taps(lambda k: ext_ref[pl.ds(base + k, rows), :], cw, k_width)


def _conv_grouped(ext3_ref, values, header, cw, k_width, s0):
    rows, cols = values.shape
    n_seq = rows // SUBLANES
    ext3_ref[s0:s0 + n_seq, 0:SUBLANES, :] = header.reshape(n_seq, SUBLANES, cols)
    ext3_ref[s0:s0 + n_seq, SUBLANES:, :] = values.reshape(n_seq, SUBLANES, cols)
    base = SUBLANES - (k_width - 1)
    return _conv_taps(
        lambda k: ext3_ref[s0:s0 + n_seq, base + k:base + k + SUBLANES, :].reshape(rows, cols), cw, k_width)


def _mixer_a_kernel(u_ref, um_ref, wb_ref, wc_ref, wv_ref, cw_ref, hdr_ref,
                    oa_ref, tailp_ref, tails_ref,
                    wbf_ref, ext_ref, ext3_ref, carry_ref, mhdr_ref,
                    *, tm, tiles_per_seq, n_prompt_tiles, n_meta):
    m = pl.program_id(1)
    cw = cw_ref[...]
    ch = MIX_CHUNK
    n_chunks = tm // ch

    @pl.when(m == 0)
    def _():
        wbf_ref[0] = wb_ref[...].astype(BF16)
        wbf_ref[1] = wc_ref[...].astype(BF16)
        wbf_ref[2] = wv_ref[...].astype(BF16)
        um = um_ref[...]
        cv_meta = _dot(um, wbf_ref[1]) * _dot(um, wbf_ref[2])
        mhdr_ref[...] = cv_meta[n_meta - SUBLANES:, :]

    def products(r0):
        u = u_ref[r0:r0 + ch, :]
        return _dot(u, wbf_ref[0]), _dot(u, wbf_ref[1]) * _dot(u, wbf_ref[2])

    @pl.when(m < n_prompt_tiles)
    def _():
        @pl.when((m % tiles_per_seq) == 0)
        def _():
            carry_ref[...] = mhdr_ref[...]

        ext_ref[0:SUBLANES, :] = carry_ref[...]
        for c in range(n_chunks):
            r0 = c * ch
            zb, cv = products(r0)
            conv = _conv_flat(ext_ref, cv, cw, K_A, r0)
            oa_ref[r0:r0 + ch, :] = (zb * conv).astype(BF16)
        carry_ref[...] = ext_ref[tm:, :]
        tailp_ref[0] = ext_ref[tm:, :]

    @pl.when(m >= n_prompt_tiles)
    def _():
        for c in range(n_chunks):
            r0 = c * ch
            zb, cv = products(r0)
            conv = _conv_grouped(ext3_ref, cv, hdr_ref[r0:r0 + ch, :], cw, K_A, r0 // SUBLANES)
            oa_ref[r0:r0 + ch, :] = (zb * conv).astype(BF16)
            tails_ref[r0:r0 + ch, :] = cv


def _mixer_a(u, um, w_in, conv_w, hdr_s, *, tm, width, n_seq, tiles_per_seq, n_prompt_tiles, n_tiles):
    d = u.shape[1]
    nb = width // CB
    n_meta = um.shape[0]
    last_seq = n_seq - 1

    def wspec(seg):
        return pl.BlockSpec((d, CB), lambda j, m, seg=seg: (0, seg * nb + j))

    return pl.pallas_call(
        functools.partial(_mixer_a_kernel, tm=tm, tiles_per_seq=tiles_per_seq,
                          n_prompt_tiles=n_prompt_tiles, n_meta=n_meta),
        grid=(nb, n_tiles),
        in_specs=[
            pl.BlockSpec((tm, d), lambda j, m: (m, 0)),
            pl.BlockSpec(um.shape, lambda j, m: (0, 0)),
            wspec(0), wspec(1), wspec(2),
            pl.BlockSpec((K_A, CB), lambda j, m: (0, j)),
            pl.BlockSpec((tm, CB), lambda j, m: (jnp.maximum(m - n_prompt_tiles, 0), j)),
        ],
        out_specs=[
            pl.BlockSpec((tm, CB), lambda j, m: (m, j)),
            pl.BlockSpec((1, SUBLANES, CB),
                         lambda j, m: (jnp.minimum(m // tiles_per_seq, last_seq), 0, j)),
            pl.BlockSpec((tm, CB), lambda j, m: (jnp.maximum(m - n_prompt_tiles, 0), j)),
        ],
        out_shape=[
            jax.ShapeDtypeStruct((n_tiles * tm, width), BF16),
            jax.ShapeDtypeStruct((n_seq, SUBLANES, width), F32),
            jax.ShapeDtypeStruct(hdr_s.shape, F32),
        ],
        scratch_shapes=[
            pltpu.VMEM((3, d, CB), BF16),
            pltpu.VMEM((tm + SUBLANES, CB), F32),
            pltpu.VMEM((tm // SUBLANES, 2 * SUBLANES, CB), F32),
            pltpu.VMEM((SUBLANES, CB), F32),
            pltpu.VMEM((SUBLANES, CB), F32),
        ],
        compiler_params=_params(2),
        name="mixer_a",
    )(u, um, w_in, w_in, w_in, conv_w, hdr_s)


def _softplus(x):
    return jnp.maximum(x, 0.0) + jnp.log1p(jnp.exp(-jnp.abs(x)))


def _lru_kernel(u_ref, um_ref, wx_ref, wy_ref, cw_ref, cbias_ref, wa_ref, ba_ref, wi_ref, bi_ref, lam_ref,
                hdr_ref, h0_ref,
                or_ref, tailp_ref, tails_ref, hp_ref, hs_ref,
                wbf_ref, gbf_ref, ext_ref, ext3_ref, carry_ref, hcarry_ref,
                mhdr_ref, mh_ref,
                *, tm, tiles_per_seq, n_prompt_tiles, n_meta, lru_block):
    m = pl.program_id(1)
    cw = cw_ref[...]
    cbias = cbias_ref[...]
    heads = CB // lru_block
    ch = MIX_CHUNK
    n_chunks = tm // ch
    row8 = lax.broadcasted_iota(I32, (SUBLANES, CB), 0)

    def gate_terms(xc):
        xb = xc.astype(BF16)
        ra, ri = [], []
        for hh in range(heads):
            xh = xb[:, hh * lru_block:(hh + 1) * lru_block]
            ra.append(_dot(xh, gbf_ref[0, hh]))
            ri.append(_dot(xh, gbf_ref[1, hh]))
        r = jax.nn.sigmoid(jnp.concatenate(ra, axis=-1) + ba_ref[...])
        i = jax.nn.sigmoid(jnp.concatenate(ri, axis=-1) + bi_ref[...])
        log_a = -C_RG * r * _softplus(-lam_ref[...])
        a = jnp.exp(log_a)
        b = jnp.sqrt((1.0 + a * a) * jnp.tanh(-log_a)) * i * xc
        return a, b

    def scan(a, b, h_prev, h0_row=None):
        out = []
        for g in range(a.shape[0] // SUBLANES):
            a8 = a[g * SUBLANES:(g + 1) * SUBLANES, :]
            b8 = b[g * SUBLANES:(g + 1) * SUBLANES, :]
            for dist in (1, 2, 4):
                keep = row8 >= dist
                b8 = jnp.where(keep, a8 * pltpu.roll(b8, dist, 0) + b8, b8)
                a8 = jnp.where(keep, a8 * pltpu.roll(a8, dist, 0), a8)
            h8 = a8 * (h_prev if h0_row is None else h0_row(g)) + b8
            out.append(h8)
            h_prev = h8[SUBLANES - 1:SUBLANES, :]
        return jnp.concatenate(out, axis=0), h_prev

    @pl.when(m == 0)
    def _():
        wbf_ref[0] = wx_ref[...].astype(BF16)
        wbf_ref[1] = wy_ref[...].astype(BF16)
        gbf_ref[0] = wa_ref[...].astype(BF16)
        gbf_ref[1] = wi_ref[...].astype(BF16)
        zx_meta = _dot(um_ref[...], wbf_ref[0])
        ext_ref[0:SUBLANES, :] = jnp.zeros((SUBLANES, CB), F32)
        a, b = gate_terms(_conv_flat(ext_ref, zx_meta, cw, K_R, 0) + cbias)
        _, h_meta = scan(a, b, jnp.zeros((1, CB), F32))
        mh_ref[...] = jnp.broadcast_to(h_meta, (SUBLANES, CB))
        mhdr_ref[...] = zx_meta[n_meta - SUBLANES:, :]

    def products(r0):
        u = u_ref[r0:r0 + ch, :]
        return _dot(u, wbf_ref[0]), _dot(u, wbf_ref[1])

    @pl.when(m < n_prompt_tiles)
    def _():
        @pl.when((m % tiles_per_seq) == 0)
        def _():
            carry_ref[...] = mhdr_ref[...]
            hcarry_ref[...] = mh_ref[...]

        ext_ref[0:SUBLANES, :] = carry_ref[...]
        h_prev = hcarry_ref[0:1, :]
        for c in range(n_chunks):
            r0 = c * ch
            zx, zy = products(r0)
            a, b = gate_terms(_conv_flat(ext_ref, zx, cw, K_R, r0) + cbias)
            h, h_prev = scan(a, b, h_prev)
            or_ref[r0:r0 + ch, :] = (h * jax.nn.gelu(zy)).astype(BF16)
        hcarry_ref[0:1, :] = h_prev
        carry_ref[...] = ext_ref[tm:, :]
        tailp_ref[0] = ext_ref[tm:, :]
        hp_ref[0] = h[ch - SUBLANES:, :]

    @pl.when(m >= n_prompt_tiles)
    def _():
        for c in range(n_chunks):
            r0 = c * ch
            s0 = r0 // SUBLANES
            zx, zy = products(r0)
            xc = _conv_grouped(ext3_ref, zx, hdr_ref[r0:r0 + ch, :], cw, K_R, s0) + cbias
            a, b = gate_terms(xc)
            h, _ = scan(a, b, None, lambda g, s0=s0: h0_ref[s0 + g:s0 + g + 1, :])
            or_ref[r0:r0 + ch, :] = (h * jax.nn.gelu(zy)).astype(BF16)
            tails_ref[r0:r0 + ch, :] = zx
            hs_ref[r0:r0 + ch, :] = h


def _mixer_r(u, um, w_in, conv_w, conv_b, lru_wa, lru_ba, lru_wx, lru_bx, lru_lam, hdr_s, h0_s,
             *, tm, width, seg0, n_seq, tiles_per_seq, n_prompt_tiles, n_tiles):
    d = u.shape[1]
    nb = width // CB
    n_meta = um.shape[0]
    last_seq = n_seq - 1
    lru_block = lru_wa.shape[-1]
    heads = CB // lru_block
    seqs_per_tile = tm // SUBLANES

    def wspec(seg):
        return pl.BlockSpec((d, CB), lambda j, m, seg=seg: (0, seg * nb + j))

    def vec():
        return pl.BlockSpec((1, CB), lambda j, m: (0, j))

    def gspec():
        return pl.BlockSpec((heads, lru_block, lru_block), lambda j, m: (j, 0, 0))

    def sample_rows():
        return pl.BlockSpec((tm, CB), lambda j, m: (jnp.maximum(m - n_prompt_tiles, 0), j))

    def seq_tail():
        return pl.BlockSpec((1, SUBLANES, CB),
                            lambda j, m: (jnp.minimum(m // tiles_per_seq, last_seq), 0, j))

    return pl.pallas_call(
        functools.partial(_lru_kernel, tm=tm, tiles_per_seq=tiles_per_seq, n_prompt_tiles=n_prompt_tiles,
                          n_meta=n_meta, lru_block=lru_block),
        grid=(nb, n_tiles),
        in_specs=[
            pl.BlockSpec((tm, d), lambda j, m: (m, 0)),
            pl.BlockSpec(um.shape, lambda j, m: (0, 0)),
            wspec(seg0), wspec(seg0 + 1),
            pl.BlockSpec((K_R, CB), lambda j, m: (0, j)),
            vec(),
            gspec(), vec(), gspec(), vec(), vec(),
            sample_rows(),
            pl.BlockSpec((seqs_per_tile, CB), lambda j, m: (jnp.maximum(m - n_prompt_tiles, 0), j)),
        ],
        out_specs=[
            pl.BlockSpec((tm, CB), lambda j, m: (m, j)),
            seq_tail(), sample_rows(), seq_tail(), sample_rows(),
        ],
        out_shape=[
            jax.ShapeDtypeStruct((n_tiles * tm, width), BF16),
            jax.ShapeDtypeStruct((n_seq, SUBLANES, width), F32),
            jax.ShapeDtypeStruct(hdr_s.shape, F32),
            jax.ShapeDtypeStruct((n_seq, SUBLANES, width), F32),
            jax.ShapeDtypeStruct(hdr_s.shape, F32),
        ],
        scratch_shapes=[
            pltpu.VMEM((2, d, CB), BF16),
            pltpu.VMEM((2, heads, lru_block, lru_block), BF16),
            pltpu.VMEM((tm + SUBLANES, CB), F32),
            pltpu.VMEM((seqs_per_tile, 2 * SUBLANES, CB), F32),
            pltpu.VMEM((SUBLANES, CB), F32),
            pltpu.VMEM((SUBLANES, CB), F32),
            pltpu.VMEM((SUBLANES, CB), F32),
            pltpu.VMEM((SUBLANES, CB), F32),
        ],
        compiler_params=_params(2),
        name="mixer_r",
    )(u, um, w_in, w_in, conv_w, conv_b, lru_wa, lru_ba, lru_wx, lru_bx, lru_lam, hdr_s, h0_s)


def _gates_kernel(u_ref, w_ref, g_ref, wbf_ref):
    @pl.when(pl.program_id(1) == 0)
    def _():
        wbf_ref[...] = w_ref[...].astype(BF16)

    g_ref[...] = jax.nn.sigmoid(_dot(u_ref[...], wbf_ref[...]))


def _gates(u, w_in, *, col0, n_cols, tm):
    m_rows, d = u.shape
    cb = 2 * CB
    return pl.pallas_call(
        _gates_kernel,
        grid=(n_cols // cb, m_rows // tm),
        in_specs=[
            pl.BlockSpec((tm, d), lambda j, m: (m, 0)),
            pl.BlockSpec((d, cb), lambda j, m: (0, col0 // cb + j)),
        ],
        out_specs=pl.BlockSpec((tm, cb), lambda j, m: (m, j)),
        out_shape=jax.ShapeDtypeStruct((m_rows, n_cols), F32),
        scratch_shapes=[pltpu.VMEM((d, cb), BF16)],
        compiler_params=_params(2),
        name="gates",
    )(u, w_in)


def _merge_kernel(ga_ref, gr_ref, oa_ref, or_ref, wba_ref, wbr_ref, mg_ref, wb_bf):
    @pl.when(pl.program_id(1) == 0)
    def _():
        wb_bf[0] = wba_ref[...].astype(BF16)
        wb_bf[1] = wbr_ref[...].astype(BF16)

    mg = ga_ref[...] * _dot(oa_ref[...], wb_bf[0]) + gr_ref[...] * _dot(or_ref[...], wb_bf[1])
    mg_ref[...] = mg.astype(BF16)


def _merge(gates, out_a, out_r, w_br_a, w_br_r, *, tm):
    m_rows, width = out_a.shape
    d = w_br_a.shape[1]
    cb = 2 * CB
    nb = d // cb
    return pl.pallas_call(
        _merge_kernel,
        grid=(nb, m_rows // tm),
        in_specs=[
            pl.BlockSpec((tm, cb), lambda j, m: (m, j)),
            pl.BlockSpec((tm, cb), lambda j, m: (m, nb + j)),
            pl.BlockSpec((tm, width), lambda j, m: (m, 0)),
            pl.BlockSpec((tm, width), lambda j, m: (m, 0)),
            pl.BlockSpec((width, cb), lambda j, m: (0, j)),
            pl.BlockSpec((width, cb), lambda j, m: (0, j)),
        ],
        out_specs=pl.BlockSpec((tm, cb), lambda j, m: (m, j)),
        out_shape=jax.ShapeDtypeStruct((m_rows, d), BF16),
        scratch_shapes=[pltpu.VMEM((2, width, cb), BF16)],
        compiler_params=_params(2),
        name="merge",
    )(gates, gates, out_a, out_r, w_br_a, w_br_r)


def _oproj_kernel(mg_ref, xp_ref, xs_ref, wo_ref, x1_ref, wbf_ref, *, n_prompt_tiles):
    m = pl.program_id(1)

    @pl.when(m == 0)
    def _():
        wbf_ref[...] = wo_ref[...].astype(BF16)

    y = _dot(mg_ref[...], wbf_ref[...])

    @pl.when(m < n_prompt_tiles)
    def _():
        x1_ref[...] = xp_ref[...] + y

    @pl.when(m >= n_prompt_tiles)
    def _():
        x1_ref[...] = xs_ref[...] + y


def _oproj(mg, xp, xs, w_o, *, tm):
    m_rows, d = mg.shape
    cb = 2 * CB
    nb = d // cb
    n_prompt_tiles = xp.shape[0] // tm
    n_tiles = m_rows // tm
    assert xp.shape[0] % tm == 0 and xs.shape[0] % tm == 0
    return pl.pallas_call(
        functools.partial(_oproj_kernel, n_prompt_tiles=n_prompt_tiles),
        grid=(nb, n_tiles),
        in_specs=[
            pl.BlockSpec((tm, d), lambda j, m: (m, 0)),
            pl.BlockSpec((tm, cb), lambda j, m: (jnp.minimum(m, n_prompt_tiles - 1), j)),
            pl.BlockSpec((tm, cb), lambda j, m: (jnp.maximum(m - n_prompt_tiles, 0), j)),
            pl.BlockSpec((d, cb), lambda j, m: (0, j)),
        ],
        out_specs=pl.BlockSpec((tm, cb), lambda j, m: (m, j)),
        out_shape=jax.ShapeDtypeStruct((m_rows, d), F32),
        scratch_shapes=[pltpu.VMEM((d, cb), BF16)],
        compiler_params=_params(2),
        name="oproj",
    )(mg, xp, xs, w_o)


def _router_kernel(x1_ref, g_ref, wt_ref, bt_ref, xn_ref, eid_ref, wts_ref, *, n_groups):
    xn = _rmsnorm_rows(x1_ref[...], g_ref[...])
    xn_ref[...] = xn
    lg = lax.dot_general(wt_ref[...].astype(BF16), xn.astype(BF16), (((1,), (1,)), ((), ())),
                         preferred_element_type=F32) + bt_ref[...]
    rows = xn.shape[0]
    row8 = lax.broadcasted_iota(I32, (SUBLANES, rows), 0).astype(F32)

    def first_index_of(v, vmax):
        return jnp.min(jnp.where(v == vmax, row8, float(SUBLANES)), axis=0, keepdims=True)

    gl = jnp.where(row8 < float(n_groups), lg[0:SUBLANES, :], -jnp.inf)
    ge = jnp.exp(gl - jnp.max(gl, axis=0, keepdims=True))
    gprob = ge / jnp.sum(ge, axis=0, keepdims=True)
    g_p = jnp.max(gprob, axis=0, keepdims=True)
    g_idx = first_index_of(gprob, g_p)

    esel = jnp.zeros((SUBLANES, rows), F32)
    for k in range(n_groups):
        esel = jnp.where(g_idx == float(k), lg[(k + 1) * SUBLANES:(k + 2) * SUBLANES, :], esel)
    ee = jnp.exp(esel - jnp.max(esel, axis=0, keepdims=True))
    ep = ee / jnp.sum(ee, axis=0, keepdims=True)
    p1 = jnp.max(ep, axis=0, keepdims=True)
    i1 = first_index_of(ep, p1)
    ep_rest = jnp.where(row8 == i1, -1.0, ep)
    p2 = jnp.max(ep_rest, axis=0, keepdims=True)
    i2 = first_index_of(ep_rest, p2)

    den = p1 + p2
    wts_ref[0:1, :] = p1 / den * g_p
    wts_ref[1:2, :] = p2 / den * g_p
    eid_ref[0:1, :] = (g_idx * float(SUBLANES) + i1).astype(I32)
    eid_ref[1:2, :] = (g_idx * float(SUBLANES) + i2).astype(I32)


def _router(x1, g2, wt, bt, *, n_groups):
    m_rows, d = x1.shape
    return pl.pallas_call(
        functools.partial(_router_kernel, n_groups=n_groups),
        grid=(m_rows // TR,),
        in_specs=[
            pl.BlockSpec((TR, d), lambda i: (i, 0)),
            pl.BlockSpec((1, d), lambda i: (0, 0)),
            pl.BlockSpec(wt.shape, lambda i: (0, 0)),
            pl.BlockSpec(bt.shape, lambda i: (0, 0)),
        ],
        out_specs=[
            pl.BlockSpec((TR, d), lambda i: (i, 0)),
            pl.BlockSpec((TOP_K, TR), lambda i: (0, i)),
            pl.BlockSpec((TOP_K, TR), lambda i: (0, i)),
        ],
        out_shape=[
            jax.ShapeDtypeStruct((m_rows, d), F32),
            jax.ShapeDtypeStruct((TOP_K, m_rows), I32),
            jax.ShapeDtypeStruct((TOP_K, m_rows), F32),
        ],
        compiler_params=_params(1),
        name="router",
    )(x1, g2, wt, bt)


INFO_TILE_EXPERT, INFO_NUM_TILES, INFO_VALID_END, INFO_END_TILE, INFO_ROWS = 0, 1, 2, 3, SUBLANES


def _positions_kernel(eid_ref, pos_ref, info_ref, exc_ref, *, n_experts, tile_shift):
    m_rows = eid_ref.shape[1]
    chunk = 2 * LANES
    n_chunks = m_rows // chunk
    tile = 1 << tile_shift
    e_col = lax.broadcasted_iota(I32, (n_experts, chunk), 0)
    upper = (lax.broadcasted_iota(I32, (chunk, chunk), 0)
             < lax.broadcasted_iota(I32, (chunk, chunk), 1)).astype(BF16)

    def onehots(c):
        off = pl.multiple_of(c * chunk, chunk)
        ids = eid_ref[:, pl.ds(off, chunk)]
        return off, [(e_col == ids[k:k + 1, :]) for k in range(TOP_K)]

    def count(c, carry):
        off, hot = onehots(c)
        used = jnp.where(hot[0] | hot[1], 1.0, 0.0)
        exc_ref[:, pl.ds(off, chunk)] = _dot(used.astype(BF16), upper) + carry
        return carry + jnp.sum(used, axis=1, keepdims=True)

    counts = lax.fori_loop(0, n_chunks, count, jnp.zeros((n_experts, 1), F32))

    lane = lax.broadcasted_iota(I32, (n_experts, LANES), 1)
    sub = lax.broadcasted_iota(I32, (n_experts, LANES), 0)

    def to_lanes(col):
        return jnp.sum(jnp.where(lane == sub, col, 0.0), axis=0, keepdims=True)

    n_tiles_e = ((counts.astype(I32) + (tile - 1)) >> tile_shift).astype(F32)
    first_tile = jnp.sum(jnp.where(lane < sub, to_lanes(n_tiles_e), 0.0), axis=1, keepdims=True)
    end_tile = first_tile + n_tiles_e
    first_row = first_tile * float(tile)

    def place(c, _):
        off, hot = onehots(c)
        dest = exc_ref[:, pl.ds(off, chunk)] + first_row
        for k in range(TOP_K):
            pos_ref[k:k + 1, pl.ds(off, chunk)] = jnp.sum(
                jnp.where(hot[k], dest, 0.0), axis=0, keepdims=True).astype(I32)
        return 0

    lax.fori_loop(0, n_chunks, place, 0)

    tile_expert = jnp.sum(jnp.where(end_tile <= lane.astype(F32), 1.0, 0.0), axis=0, keepdims=True)
    num_tiles = jnp.max(end_tile, axis=0, keepdims=True)
    valid_end = to_lanes(first_row + counts)
    info_ref[...] = jnp.zeros(info_ref.shape, I32)
    info_ref[INFO_TILE_EXPERT:INFO_TILE_EXPERT + 1, :] = jnp.minimum(
        tile_expert, float(n_experts - 1)).astype(I32)
    info_ref[INFO_NUM_TILES:INFO_NUM_TILES + 1, :] = jnp.broadcast_to(num_tiles, (1, LANES)).astype(I32)
    info_ref[INFO_VALID_END:INFO_VALID_END + 1, :] = valid_end.astype(I32)
    info_ref[INFO_END_TILE:INFO_END_TILE + 1, :] = to_lanes(end_tile).astype(I32)


def _positions(eid, *, n_experts):
    m_rows = eid.shape[1]
    tile_shift = T_EXP.bit_length() - 1
    return pl.pallas_call(
        functools.partial(_positions_kernel, n_experts=n_experts, tile_shift=tile_shift),
        out_shape=[
            jax.ShapeDtypeStruct((TOP_K, m_rows), I32),
            jax.ShapeDtypeStruct((INFO_ROWS, LANES), I32),
        ],
        scratch_shapes=[pltpu.VMEM((n_experts, m_rows), F32)],
        compiler_params=pltpu.CompilerParams(vmem_limit_bytes=VMEM_LIMIT_BYTES),
        name="positions",
    )(eid)


def _info(info_ref, row, lane):
    return info_ref[row * LANES + lane]


def _tile_state(i, info_ref):
    n_used = _info(info_ref, INFO_NUM_TILES, 0)
    ic = jnp.minimum(i, n_used - 1)
    e = _info(info_ref, INFO_TILE_EXPERT, ic)
    prev = _info(info_ref, INFO_TILE_EXPERT, jnp.maximum(ic - 1, 0))
    first = jnp.logical_or(ic == 0, e != prev)
    n_valid = jnp.minimum(_info(info_ref, INFO_VALID_END, e) - ic * T_EXP, T_EXP)
    return i < n_used, ic, e, first, n_valid


def _stream_expert_weights(i, info_ref, e, first, w_refs, wf_ref, wbf_ref, wsem, slot_ref):
    n_used = _info(info_ref, INFO_NUM_TILES, 0)
    rows = wf_ref.shape[2]
    chunk = min(rows, 512)

    def copies(expert, slot):
        return [pltpu.make_async_copy(w.at[expert], wf_ref.at[slot, j], wsem.at[slot])
                for j, w in enumerate(w_refs)]

    @pl.when(i == 0)
    def _():
        slot_ref[0] = 0
        for cp in copies(e, 0):
            cp.start(priority=1)

    @pl.when(jnp.logical_and(first, i > 0))
    def _():
        slot_ref[0] = 1 - slot_ref[0]

    @pl.when(first)
    def _():
        slot = slot_ref[0]
        for cp in copies(e, slot):
            cp.wait()
        for j in range(len(w_refs)):
            def cast(c, _, j=j):
                r0 = pl.multiple_of(c * chunk, chunk)
                wbf_ref[j, pl.ds(r0, chunk), :] = wf_ref[slot, j, pl.ds(r0, chunk), :].astype(BF16)
                return 0
            lax.fori_loop(0, rows // chunk, cast, 0)
        next_tile = _info(info_ref, INFO_END_TILE, e)

        @pl.when(next_tile < n_used)
        def _():
            for cp in copies(_info(info_ref, INFO_TILE_EXPERT, next_tile), 1 - slot):
                cp.start(priority=1)


def _expert_up_kernel(info_ref, pos_ref, xn_ref, wg_ref, wu_ref, h_ref,
                      xbuf_ref, wf_ref, wbf_ref, inv_ref, slot_ref, xsem, wsem, *, m_rows):
    i = pl.program_id(0)
    used, _, e, first, n_valid = _tile_state(i, info_ref)
    n_used = _info(info_ref, INFO_NUM_TILES, 0)

    def gather(tile, slot, n_rows, start):
        n_groups = (n_rows + (SUBLANES - 1)) >> (SUBLANES.bit_length() - 1)
        if start:
            first_row = tile * T_EXP
            last_row = first_row + n_rows - 1

            def body(g, _):
                for j in range(SUBLANES):
                    r = g * SUBLANES + j
                    src = inv_ref[jnp.minimum(first_row + r, last_row)]
                    pltpu.make_async_copy(xn_ref.at[pl.ds(src, 1), :], xbuf_ref.at[slot, pl.ds(r, 1), :],
                                          xsem.at[slot]).start()
                return 0
            lax.fori_loop(0, n_groups, body, 0)
        else:
            for bit in range((T_EXP // SUBLANES).bit_length()):
                n = SUBLANES << bit

                @pl.when((n_groups & (1 << bit)) != 0)
                def _():
                    pltpu.make_async_copy(xn_ref.at[pl.ds(0, n), :], xbuf_ref.at[slot, pl.ds(0, n), :],
                                          xsem.at[slot]).wait()

    @pl.when(i == 0)
    def _():
        def invert(t, _):
            for k in range(TOP_K):
                inv_ref[pos_ref[k * m_rows + t]] = t
            return 0
        lax.fori_loop(0, m_rows, invert, 0, unroll=8)
        xbuf_ref[...] = jnp.zeros(xbuf_ref.shape, F32)
        gather(0, 0, n_valid, True)

    @pl.when(used)
    def _():
        _stream_expert_weights(i, info_ref, e, first, (wg_ref, wu_ref), wf_ref, wbf_ref, wsem, slot_ref)
        slot = i & 1
        gather(i, slot, n_valid, False)

        @pl.when(i + 1 < n_used)
        def _():
            gather(i + 1, 1 - slot, _tile_state(i + 1, info_ref)[4], True)

        x = xbuf_ref[slot].astype(BF16)
        h = jax.nn.silu(_dot(x, wbf_ref[0])) * _dot(x, wbf_ref[1])
        h_ref[...] = h.astype(BF16)

    @pl.when(jnp.logical_not(used))
    def _():
        h_ref[...] = jnp.zeros(h_ref.shape, BF16)


def _expert_up(info_flat, pos_flat, xn, w_gate, w_up, *, n_tiles):
    m_rows, d = xn.shape
    d_exp = w_gate.shape[-1]
    p_rows = n_tiles * T_EXP
    any_spec = pl.BlockSpec(memory_space=pl.ANY)
    return pl.pallas_call(
        functools.partial(_expert_up_kernel, m_rows=m_rows),
        grid_spec=pltpu.PrefetchScalarGridSpec(
            num_scalar_prefetch=2,
            grid=(n_tiles,),
            in_specs=[any_spec, any_spec, any_spec],
            out_specs=pl.BlockSpec((T_EXP, d_exp), lambda i, info, pos: (i, 0)),
            scratch_shapes=[
                pltpu.VMEM((2, T_EXP, d), F32),
                pltpu.VMEM((2, 2, d, d_exp), F32),
                pltpu.VMEM((2, d, d_exp), BF16),
                pltpu.SMEM((p_rows,), I32),
                pltpu.SMEM((1,), I32),
                pltpu.SemaphoreType.DMA((2,)),
                pltpu.SemaphoreType.DMA((2,)),
            ],
        ),
        out_shape=jax.ShapeDtypeStruct((p_rows, d_exp), BF16),
        compiler_params=_params(1),
        name="expert_up",
    )(info_flat, pos_flat, xn, w_gate, w_up)


def _expert_down_kernel(info_ref, h_ref, wd_ref, y_ref, wf_ref, wbf_ref, slot_ref, wsem):
    i = pl.program_id(0)
    used, _, e, first, _ = _tile_state(i, info_ref)

    @pl.when(used)
    def _():
        _stream_expert_weights(i, info_ref, e, first, (wd_ref,), wf_ref, wbf_ref, wsem, slot_ref)
        y_ref[...] = _dot(h_ref[...], wbf_ref[0])

    @pl.when(jnp.logical_not(used))
    def _():
        y_ref[...] = jnp.zeros(y_ref.shape, F32)


def _expert_down(info_flat, h, w_down, *, n_tiles):
    p_rows, d_exp = h.shape
    d = w_down.shape[-1]
    return pl.pallas_call(
        _expert_down_kernel,
        grid_spec=pltpu.PrefetchScalarGridSpec(
            num_scalar_prefetch=1,
            grid=(n_tiles,),
            in_specs=[
                pl.BlockSpec((T_EXP, d_exp), lambda i, info: (_tile_state(i, info)[1], 0)),
                pl.BlockSpec(memory_space=pl.ANY),
            ],
            out_specs=pl.BlockSpec((T_EXP, d), lambda i, info: (i, 0)),
            scratch_shapes=[
                pltpu.VMEM((2, 1, d_exp, d), F32),
                pltpu.VMEM((1, d_exp, d), BF16),
                pltpu.SMEM((1,), I32),
                pltpu.SemaphoreType.DMA((2,)),
            ],
        ),
        out_shape=jax.ShapeDtypeStruct((p_rows, d), F32),
        compiler_params=_params(1),
        name="expert_down",
    )(info_flat, h, w_down)


def _combine_kernel(pos_ref, x1_ref, w_ref, g_ref, y_ref, out_ref, ybuf_ref, sem, *, m_rows, row0, n_steps):
    i = pl.program_id(0)

    def gather_start(step):
        base = row0 + step * TR
        slot = step & 1

        def body(r, _):
            for k in range(TOP_K):
                src = pos_ref[k * m_rows + base + r]
                pltpu.make_async_copy(y_ref.at[pl.ds(src, 1), :], ybuf_ref.at[slot, k, pl.ds(r, 1), :],
                                      sem.at[slot]).start(priority=k % 2)
            return 0
        lax.fori_loop(0, TR, body, 0, unroll=8)

    def gather_wait(slot):
        for k in range(TOP_K):
            pltpu.make_async_copy(y_ref.at[pl.ds(0, TR), :], ybuf_ref.at[slot, k], sem.at[slot]).wait()

    @pl.when(i == 0)
    def _():
        gather_start(0)

    @pl.when(i + 1 < n_steps)
    def _():
        gather_start(i + 1)

    slot = i & 1
    gather_wait(slot)
    w = w_ref[...]
    moe = w[:, 0:1] * ybuf_ref[slot, 0] + w[:, 1:2] * ybuf_ref[slot, 1]
    out_ref[...] = _rmsnorm_rows(x1_ref[...] + moe, g_ref[...])


def _combine(pos_flat, x1, wts_t, norm_f, y, *, row0, n_rows):
    m_rows, d = x1.shape
    tile0 = row0 // TR
    n_steps = n_rows // TR
    return pl.pallas_call(
        functools.partial(_combine_kernel, m_rows=m_rows, row0=row0, n_steps=n_steps),
        grid_spec=pltpu.PrefetchScalarGridSpec(
            num_scalar_prefetch=1,
            grid=(n_steps,),
            in_specs=[
                pl.BlockSpec((TR, d), lambda i, pos: (tile0 + i, 0)),
                pl.BlockSpec((TR, TOP_K), lambda i, pos: (tile0 + i, 0)),
                pl.BlockSpec((1, d), lambda i, pos: (0, 0)),
                pl.BlockSpec(memory_space=pl.ANY),
            ],
            out_specs=pl.BlockSpec((TR, d), lambda i, pos: (i, 0)),
            scratch_shapes=[pltpu.VMEM((2, TOP_K, TR, d), F32), pltpu.SemaphoreType.DMA((2,))],
        ),
        out_shape=jax.ShapeDtypeStruct((n_rows, d), F32),
        compiler_params=_params(1),
        name="combine",
    )(pos_flat, x1, wts_t, norm_f, y)


def _sample_headers(state, k_width):
    n, _, w = state.shape
    padded = jnp.pad(state, ((0, 0), (SUBLANES - (k_width - 1), 0), (0, 0)))
    return padded.reshape(n * SUBLANES, w)


def kernel(x_prompt, x_sample, state_conv_a, state_conv_r, state_h, meta_tokens, norm1, w_in, conv_a_w, conv_r_w,
           conv_r_b, lru_wa, lru_ba, lru_wx, lru_bx, lru_lam, w_br_a, w_br_r, w_o, norm2, w_group, b_group,
           w_router, b_router, w_gate, w_up, w_down, norm_f):
    batch, seq, d = x_prompt.shape
    dec_batch, dec_seq, _ = x_sample.shape
    depth = norm1.shape[0]
    width = w_br_a.shape[1]
    n_groups = w_group.shape[-1]
    n_experts = w_router.shape[-1]
    n_meta = meta_tokens.shape[0]
    assert depth == 1, "meta rows are folded into an initial state, which only carries one layer"
    assert dec_seq == SUBLANES and n_meta % SUBLANES == 0 and n_meta >= SUBLANES
    assert seq % TM == 0 and (dec_batch * dec_seq) % TM == 0
    assert n_groups <= SUBLANES and n_experts == n_groups * SUBLANES
    assert width % CB == 0 and d % (2 * CB) == 0 and CB % lru_wa.shape[-1] == 0

    n_prompt_rows = batch * seq
    n_sample_rows = dec_batch * dec_seq
    m_rows = n_prompt_rows + n_sample_rows
    n_prompt_tiles = n_prompt_rows // TM
    n_tiles = m_rows // TM
    tiles_per_seq = seq // TM
    n_exp_tiles = (TOP_K * m_rows) // T_EXP + n_experts
    assert n_exp_tiles <= LANES

    xp = x_prompt.reshape(n_prompt_rows, d)
    xs = x_sample.reshape(n_sample_rows, d)
    row = lambda v: v.reshape(1, -1)

    u, um = _norm1(xp, xs, meta_tokens, row(norm1[0]), n_prompt_tiles, n_tiles)

    def mixer_geometry(tm):
        assert seq % tm == 0 and n_sample_rows % tm == 0 and tm % MIX_CHUNK == 0 and MIX_CHUNK % SUBLANES == 0
        return dict(tm=tm, width=width, n_seq=batch, tiles_per_seq=seq // tm,
                    n_prompt_tiles=n_prompt_rows // tm, n_tiles=m_rows // tm)

    out_a, tail_a_p, tail_a_s = _mixer_a(
        u, um, w_in[0], conv_a_w[0], _sample_headers(state_conv_a[0], K_A), **mixer_geometry(TM_MIX_A))
    out_r, tail_r_p, tail_r_s, h_p, h_s = _mixer_r(
        u, um, w_in[0], conv_r_w[0], row(conv_r_b[0]), lru_wa[0], row(lru_ba[0]), lru_wx[0], row(lru_bx[0]),
        row(lru_lam[0]), _sample_headers(state_conv_r[0], K_R), state_h[0], seg0=3, **mixer_geometry(TM_MIX_R))

    gates = _gates(u, w_in[0], col0=5 * width, n_cols=2 * d, tm=TM_OPROJ)
    merged = _merge(gates, out_a, out_r, w_br_a[0], w_br_r[0], tm=TM_OPROJ)
    x1 = _oproj(merged, xp, xs, w_o[0], tm=TM_OPROJ)

    wt = jnp.concatenate([jnp.pad(w_group[0].T, ((0, SUBLANES - n_groups), (0, 0))), w_router[0].T], axis=0)
    bt = jnp.concatenate([jnp.pad(b_group[0], (0, SUBLANES - n_groups)), b_router[0]]).reshape(-1, 1)
    xn, eid, wts = _router(x1, row(norm2[0]), wt, bt, n_groups=n_groups)

    pos, info = _positions(eid, n_experts=n_experts)
    pos_flat = pos.reshape(-1)
    info_flat = info.reshape(-1)
    hid = _expert_up(info_flat, pos_flat, xn, w_gate[0], w_up[0], n_tiles=n_exp_tiles)
    y = _expert_down(info_flat, hid, w_down[0], n_tiles=n_exp_tiles)

    wts_t = wts.T
    g_f = row(norm_f)
    y_prompt = _combine(pos_flat, x1, wts_t, g_f, y, row0=0, n_rows=n_prompt_rows)
    y_sample = _combine(pos_flat, x1, wts_t, g_f, y, row0=n_prompt_rows, n_rows=n_sample_rows)

    def sample_tail(t, k):
        return t.reshape(dec_batch, SUBLANES, width)[:, SUBLANES - k:, :][None]

    return (
        y_prompt.reshape(batch, seq, d),
        y_sample.reshape(dec_batch, dec_seq, d),
        tail_a_p[:, SUBLANES - (K_A - 1):, :][None],
        tail_r_p[:, SUBLANES - (K_R - 1):, :][None],
        h_p[:, SUBLANES - 1, :][None],
        sample_tail(tail_a_s, K_A - 1),
        sample_tail(tail_r_s, K_R - 1),
        sample_tail(h_s, 1)[:, :, 0, :],
    )
```

```python
import functools

import jax
import jax.numpy as jnp
from jax import lax
from jax.experimental import pallas as pl
from jax.experimental.pallas import tpu as pltpu

F32, BF16, I32 = jnp.float32, jnp.bfloat16, jnp.int32

EPS = 1e-6
C_RG = 8.0
K_A = 3
K_R = 4
LRU_HEADS = 16
TOP_K = 2

SUBLANES = 8
LANES = 128
VMEM_LIMIT_BYTES = 56 * 1024 * 1024

TM = 512
TM_OPROJ = 1024
TM_MIX_A = 512
TM_MIX_R = 1024
MIX_CHUNK = 256
CB = 256
T_EXP = 256
TR = 256


def _dot(a, b):
    return jnp.dot(a, b, preferred_element_type=F32)


def _pack_halves(x):
    half = x.shape[1] // 2
    return pltpu.pack_elementwise([x[:, :half], x[:, half:]], packed_dtype=BF16)


def _unpack_halves(words):
    return [pltpu.unpack_elementwise(words, index=k, packed_dtype=BF16, unpacked_dtype=F32) for k in range(2)]


def _rmsnorm_rows(x, g):
    y = x * lax.rsqrt(jnp.mean(x * x, axis=-1, keepdims=True) + EPS)
    return y * g


def _params(n_axes):
    return pltpu.CompilerParams(dimension_semantics=("arbitrary",) * n_axes,
                                vmem_limit_bytes=VMEM_LIMIT_BYTES)


def _norm1_kernel(xp_ref, xs_ref, meta_ref, g_ref, u_ref, um_ref, *, n_prompt_tiles):
    i = pl.program_id(0)
    g = g_ref[...]

    @pl.when(i < n_prompt_tiles)
    def _():
        u_ref[...] = _rmsnorm_rows(xp_ref[...], g).astype(BF16)

    @pl.when(i >= n_prompt_tiles)
    def _():
        u_ref[...] = _rmsnorm_rows(xs_ref[...], g).astype(BF16)

    @pl.when(i == 0)
    def _():
        um_ref[...] = _rmsnorm_rows(meta_ref[...], g).astype(BF16)


def _norm1(xp, xs, meta, g, n_prompt_tiles, n_tiles):
    d = xp.shape[1]
    return pl.pallas_call(
        functools.partial(_norm1_kernel, n_prompt_tiles=n_prompt_tiles),
        grid=(n_tiles,),
        in_specs=[
            pl.BlockSpec((TM, d), lambda i: (jnp.minimum(i, n_prompt_tiles - 1), 0)),
            pl.BlockSpec((TM, d), lambda i: (jnp.maximum(i - n_prompt_tiles, 0), 0)),
            pl.BlockSpec(meta.shape, lambda i: (0, 0)),
            pl.BlockSpec((1, d), lambda i: (0, 0)),
        ],
        out_specs=[
            pl.BlockSpec((TM, d), lambda i: (i, 0)),
            pl.BlockSpec(meta.shape, lambda i: (0, 0)),
        ],
        out_shape=[
            jax.ShapeDtypeStruct((n_tiles * TM, d), BF16),
            jax.ShapeDtypeStruct(meta.shape, BF16),
        ],
        compiler_params=_params(1),
        name="norm1",
    )(xp, xs, meta, g)


def _conv_taps(window, cw, k_width):
    acc = window(0) * cw[0:1, :]
    for k in range(1, k_width):
        acc = acc + window(k) * cw[k:k + 1, :]
    return acc


def _conv_flat(ext_ref, values, cw, k_width, r0):
    rows = values.shape[0]
    ext_ref[SUBLANES + r0:SUBLANES + r0 + rows, :] = values
    base = SUBLANES - (k_width - 1) + r0
    return _conv_taps(lambda k: ext_ref[pl.ds(base + k, rows), :], cw, k_width)


def _conv_grouped(ext3_ref, values, header, cw, k_width, s0):
    rows, cols = values.shape
    n_seq = rows // SUBLANES
    ext3_ref[s0:s0 + n_seq, 0:SUBLANES, :] = header.reshape(n_seq, SUBLANES, cols)
    ext3_ref[s0:s0 + n_seq, SUBLANES:, :] = values.reshape(n_seq, SUBLANES, cols)
    base = SUBLANES - (k_width - 1)
    return _conv_taps(
        lambda k: ext3_ref[s0:s0 + n_seq, base + k:base + k + SUBLANES, :].reshape(rows, cols), cw, k_width)


def _mixer_a_kernel(u_ref, um_ref, wb_ref, wc_ref, wv_ref, cw_ref, hdr_ref,
                    oa_ref, tailp_ref, tails_ref,
                    wbf_ref, ext_ref, ext3_ref, carry_ref, mhdr_ref,
                    *, tm, tiles_per_seq, n_prompt_tiles, n_meta):
    m = pl.program_id(1)
    cw = cw_ref[...]
    ch = MIX_CHUNK
    n_chunks = tm // ch

    @pl.when(m == 0)
    def _():
        wbf_ref[0] = wb_ref[...].astype(BF16)
        wbf_ref[1] = wc_ref[...].astype(BF16)
        wbf_ref[2] = wv_ref[...].astype(BF16)
        um = um_ref[...]
        cv_meta = _dot(um, wbf_ref[1]) * _dot(um, wbf_ref[2])
        mhdr_ref[...] = cv_meta[n_meta - SUBLANES:, :]

    def products(r0):
        u = u_ref[r0:r0 + ch, :]
        return _dot(u, wbf_ref[0]), _dot(u, wbf_ref[1]) * _dot(u, wbf_ref[2])

    @pl.when(m < n_prompt_tiles)
    def _():
        @pl.when((m % tiles_per_seq) == 0)
        def _():
            carry_ref[...] = mhdr_ref[...]

        ext_ref[0:SUBLANES, :] = carry_ref[...]
        for c in range(n_chunks):
            r0 = c * ch
            zb, cv = products(r0)
            conv = _conv_flat(ext_ref, cv, cw, K_A, r0)
            oa_ref[r0:r0 + ch, :] = (zb * conv).astype(BF16)
        carry_ref[...] = ext_ref[tm:, :]
        tailp_ref[0] = ext_ref[tm:, :]

    @pl.when(m >= n_prompt_tiles)
    def _():
        for c in range(n_chunks):
            r0 = c * ch
            zb, cv = products(r0)
            conv = _conv_grouped(ext3_ref, cv, hdr_ref[r0:r0 + ch, :], cw, K_A, r0 // SUBLANES)
            oa_ref[r0:r0 + ch, :] = (zb * conv).astype(BF16)
            tails_ref[r0:r0 + ch, :] = cv


def _mixer_a(u, um, w_in, conv_w, hdr_s, *, tm, width, n_seq, tiles_per_seq, n_prompt_tiles, n_tiles):
    d = u.shape[1]
    nb = width // CB
    n_meta = um.shape[0]
    last_seq = n_seq - 1

    def wspec(seg):
        return pl.BlockSpec((d, CB), lambda j, m, seg=seg: (0, seg * nb + j))

    return pl.pallas_call(
        functools.partial(_mixer_a_kernel, tm=tm, tiles_per_seq=tiles_per_seq,
                          n_prompt_tiles=n_prompt_tiles, n_meta=n_meta),
        grid=(nb, n_tiles),
        in_specs=[
            pl.BlockSpec((tm, d), lambda j, m: (m, 0)),
            pl.BlockSpec(um.shape, lambda j, m: (0, 0)),
            wspec(0), wspec(1), wspec(2),
            pl.BlockSpec((K_A, CB), lambda j, m: (0, j)),
            pl.BlockSpec((tm, CB), lambda j, m: (jnp.maximum(m - n_prompt_tiles, 0), j)),
        ],
        out_specs=[
            pl.BlockSpec((tm, CB), lambda j, m: (m, j)),
            pl.BlockSpec((1, SUBLANES, CB),
                         lambda j, m: (jnp.minimum(m // tiles_per_seq, last_seq), 0, j)),
            pl.BlockSpec((tm, CB), lambda j, m: (jnp.maximum(m - n_prompt_tiles, 0), j)),
        ],
        out_shape=[
            jax.ShapeDtypeStruct((n_tiles * tm, width), BF16),
            jax.ShapeDtypeStruct((n_seq, SUBLANES, width), F32),
            jax.ShapeDtypeStruct(hdr_s.shape, F32),
        ],
        scratch_shapes=[
            pltpu.VMEM((3, d, CB), BF16),
            pltpu.VMEM((tm + SUBLANES, CB), F32),
            pltpu.VMEM((tm // SUBLANES, 2 * SUBLANES, CB), F32),
            pltpu.VMEM((SUBLANES, CB), F32),
            pltpu.VMEM((SUBLANES, CB), F32),
        ],
        compiler_params=_params(2),
        name="mixer_a",
    )(u, um, w_in, w_in, w_in, conv_w, hdr_s)


def _softplus(x):
    return jnp.maximum(x, 0.0) + jnp.log1p(jnp.exp(-jnp.abs(x)))


def _lru_kernel(u_ref, um_ref, wx_ref, wy_ref, cw_ref, cbias_ref, wa_ref, ba_ref, wi_ref, bi_ref, lam_ref,
                hdr_ref, h0_ref,
                or_ref, tailp_ref, tails_ref, hp_ref, hs_ref,
                wbf_ref, gbf_ref, ext_ref, ext3_ref, carry_ref, hcarry_ref,
                mhdr_ref, mh_ref,
                *, tm, tiles_per_seq, n_prompt_tiles, n_meta, lru_block):
    m = pl.program_id(1)
    cw = cw_ref[...]
    cbias = cbias_ref[...]
    heads = CB // lru_block
    ch = MIX_CHUNK
    n_chunks = tm // ch
    row8 = lax.broadcasted_iota(I32, (SUBLANES, CB), 0)

    def gate_terms(xc):
        xb = xc.astype(BF16)
        ra, ri = [], []
        for hh in range(heads):
            xh = xb[:, hh * lru_block:(hh + 1) * lru_block]
            ra.append(_dot(xh, gbf_ref[0, hh]))
            ri.append(_dot(xh, gbf_ref[1, hh]))
        r = jax.nn.sigmoid(jnp.concatenate(ra, axis=-1) + ba_ref[...])
        i = jax.nn.sigmoid(jnp.concatenate(ri, axis=-1) + bi_ref[...])
        log_a = -C_RG * r * _softplus(-lam_ref[...])
        a = jnp.exp(log_a)
        b = jnp.sqrt((1.0 + a * a) * jnp.tanh(-log_a)) * i * xc
        return a, b

    def scan(a, b, h_prev, h0_row=None):
        out = []
        for g in range(a.shape[0] // SUBLANES):
            a8 = a[g * SUBLANES:(g + 1) * SUBLANES, :]
            b8 = b[g * SUBLANES:(g + 1) * SUBLANES, :]
            for dist in (1, 2, 4):
                keep = row8 >= dist
                b8 = jnp.where(keep, a8 * pltpu.roll(b8, dist, 0) + b8, b8)
                a8 = jnp.where(keep, a8 * pltpu.roll(a8, dist, 0), a8)
            h8 = a8 * (h_prev if h0_row is None else h0_row(g)) + b8
            out.append(h8)
            h_prev = h8[SUBLANES - 1:SUBLANES, :]
        return jnp.concatenate(out, axis=0), h_prev

    @pl.when(m == 0)
    def _():
        wbf_ref[0] = wx_ref[...].astype(BF16)
        wbf_ref[1] = wy_ref[...].astype(BF16)
        gbf_ref[0] = wa_ref[...].astype(BF16)
        gbf_ref[1] = wi_ref[...].astype(BF16)
        zx_meta = _dot(um_ref[...], wbf_ref[0])
        ext_ref[0:SUBLANES, :] = jnp.zeros((SUBLANES, CB), F32)
        a, b = gate_terms(_conv_flat(ext_ref, zx_meta, cw, K_R, 0) + cbias)
        _, h_meta = scan(a, b, jnp.zeros((1, CB), F32))
        mh_ref[...] = jnp.broadcast_to(h_meta, (SUBLANES, CB))
        mhdr_ref[...] = zx_meta[n_meta - SUBLANES:, :]

    def products(r0):
        u = u_ref[r0:r0 + ch, :]
        return _dot(u, wbf_ref[0]), _dot(u, wbf_ref[1])

    @pl.when(m < n_prompt_tiles)
    def _():
        @pl.when((m % tiles_per_seq) == 0)
        def _():
            carry_ref[...] = mhdr_ref[...]
            hcarry_ref[...] = mh_ref[...]

        ext_ref[0:SUBLANES, :] = carry_ref[...]
        h_prev = hcarry_ref[0:1, :]
        for c in range(n_chunks):
            r0 = c * ch
            zx, zy = products(r0)
            a, b = gate_terms(_conv_flat(ext_ref, zx, cw, K_R, r0) + cbias)
            h, h_prev = scan(a, b, h_prev)
            or_ref[r0:r0 + ch, :] = (h * jax.nn.gelu(zy)).astype(BF16)
        hcarry_ref[0:1, :] = h_prev
        carry_ref[...] = ext_ref[tm:, :]
        tailp_ref[0] = ext_ref[tm:, :]
        hp_ref[0] = h[ch - SUBLANES:, :]

    @pl.when(m >= n_prompt_tiles)
    def _():
        for c in range(n_chunks):
            r0 = c * ch
            s0 = r0 // SUBLANES
            zx, zy = products(r0)
            xc = _conv_grouped(ext3_ref, zx, hdr_ref[r0:r0 + ch, :], cw, K_R, s0) + cbias
            a, b = gate_terms(xc)
            h, _ = scan(a, b, None, lambda g, s0=s0: h0_ref[s0 + g:s0 + g + 1, :])
            or_ref[r0:r0 + ch, :] = (h * jax.nn.gelu(zy)).astype(BF16)
            tails_ref[r0:r0 + ch, :] = zx
            hs_ref[r0:r0 + ch, :] = h


def _mixer_r(u, um, w_in, conv_w, conv_b, lru_wa, lru_ba, lru_wx, lru_bx, lru_lam, hdr_s, h0_s,
             *, tm, width, seg0, n_seq, tiles_per_seq, n_prompt_tiles, n_tiles):
    d = u.shape[1]
    nb = width // CB
    n_meta = um.shape[0]
    last_seq = n_seq - 1
    lru_block = lru_wa.shape[-1]
    heads = CB // lru_block
    seqs_per_tile = tm // SUBLANES

    def wspec(seg):
        return pl.BlockSpec((d, CB), lambda j, m, seg=seg: (0, seg * nb + j))

    def vec():
        return pl.BlockSpec((1, CB), lambda j, m: (0, j))

    def gspec():
        return pl.BlockSpec((heads, lru_block, lru_block), lambda j, m: (j, 0, 0))

    def sample_rows():
        return pl.BlockSpec((tm, CB), lambda j, m: (jnp.maximum(m - n_prompt_tiles, 0), j))

    def seq_tail():
        return pl.BlockSpec((1, SUBLANES, CB),
                            lambda j, m: (jnp.minimum(m // tiles_per_seq, last_seq), 0, j))

    return pl.pallas_call(
        functools.partial(_lru_kernel, tm=tm, tiles_per_seq=tiles_per_seq, n_prompt_tiles=n_prompt_tiles,
                          n_meta=n_meta, lru_block=lru_block),
        grid=(nb, n_tiles),
        in_specs=[
            pl.BlockSpec((tm, d), lambda j, m: (m, 0)),
            pl.BlockSpec(um.shape, lambda j, m: (0, 0)),
            wspec(seg0), wspec(seg0 + 1),
            pl.BlockSpec((K_R, CB), lambda j, m: (0, j)),
            vec(),
            gspec(), vec(), gspec(), vec(), vec(),
            sample_rows(),
            pl.BlockSpec((seqs_per_tile, CB), lambda j, m: (jnp.maximum(m - n_prompt_tiles, 0), j)),
        ],
        out_specs=[
            pl.BlockSpec((tm, CB), lambda j, m: (m, j)),
            seq_tail(), sample_rows(), seq_tail(), sample_rows(),
        ],
        out_shape=[
            jax.ShapeDtypeStruct((n_tiles * tm, width), BF16),
            jax.ShapeDtypeStruct((n_seq, SUBLANES, width), F32),
            jax.ShapeDtypeStruct(hdr_s.shape, F32),
            jax.ShapeDtypeStruct((n_seq, SUBLANES, width), F32),
            jax.ShapeDtypeStruct(hdr_s.shape, F32),
        ],
        scratch_shapes=[
            pltpu.VMEM((2, d, CB), BF16),
            pltpu.VMEM((2, heads, lru_block, lru_block), BF16),
            pltpu.VMEM((tm + SUBLANES, CB), F32),
            pltpu.VMEM((seqs_per_tile, 2 * SUBLANES, CB), F32),
            pltpu.VMEM((SUBLANES, CB), F32),
            pltpu.VMEM((SUBLANES, CB), F32),
            pltpu.VMEM((SUBLANES, CB), F32),
            pltpu.VMEM((SUBLANES, CB), F32),
        ],
        compiler_params=_params(2),
        name="mixer_r",
    )(u, um, w_in, w_in, conv_w, conv_b, lru_wa, lru_ba, lru_wx, lru_bx, lru_lam, hdr_s, h0_s)


def _gates_kernel(u_ref, w_ref, g_ref, wbf_ref):
    @pl.when(pl.program_id(1) == 0)
    def _():
        wbf_ref[...] = w_ref[...].astype(BF16)

    g_ref[...] = jax.nn.sigmoid(_dot(u_ref[...], wbf_ref[...]))


def _gates(u, w_in, *, col0, n_cols, tm):
    m_rows, d = u.shape
    cb = 2 * CB
    return pl.pallas_call(
        _gates_kernel,
        grid=(n_cols // cb, m_rows // tm),
        in_specs=[
            pl.BlockSpec((tm, d), lambda j, m: (m, 0)),
            pl.BlockSpec((d, cb), lambda j, m: (0, col0 // cb + j)),
        ],
        out_specs=pl.BlockSpec((tm, cb), lambda j, m: (m, j)),
        out_shape=jax.ShapeDtypeStruct((m_rows, n_cols), F32),
        scratch_shapes=[pltpu.VMEM((d, cb), BF16)],
        compiler_params=_params(2),
        name="gates",
    )(u, w_in)


def _merge_kernel(ga_ref, gr_ref, oa_ref, or_ref, wba_ref, wbr_ref, mg_ref, wb_bf):
    @pl.when(pl.program_id(1) == 0)
    def _():
        wb_bf[0] = wba_ref[...].astype(BF16)
        wb_bf[1] = wbr_ref[...].astype(BF16)

    mg = ga_ref[...] * _dot(oa_ref[...], wb_bf[0]) + gr_ref[...] * _dot(or_ref[...], wb_bf[1])
    mg_ref[...] = mg.astype(BF16)


def _merge(gates, out_a, out_r, w_br_a, w_br_r, *, tm):
    m_rows, width = out_a.shape
    d = w_br_a.shape[1]
    cb = 2 * CB
    nb = d // cb
    return pl.pallas_call(
        _merge_kernel,
        grid=(nb, m_rows // tm),
        in_specs=[
            pl.BlockSpec((tm, cb), lambda j, m: (m, j)),
            pl.BlockSpec((tm, cb), lambda j, m: (m, nb + j)),
            pl.BlockSpec((tm, width), lambda j, m: (m, 0)),
            pl.BlockSpec((tm, width), lambda j, m: (m, 0)),
            pl.BlockSpec((width, cb), lambda j, m: (0, j)),
            pl.BlockSpec((width, cb), lambda j, m: (0, j)),
        ],
        out_specs=pl.BlockSpec((tm, cb), lambda j, m: (m, j)),
        out_shape=jax.ShapeDtypeStruct((m_rows, d), BF16),
        scratch_shapes=[pltpu.VMEM((2, width, cb), BF16)],
        compiler_params=_params(2),
        name="merge",
    )(gates, gates, out_a, out_r, w_br_a, w_br_r)


def _oproj_kernel(mg_ref, xp_ref, xs_ref, wo_ref, x1_ref, wbf_ref, *, n_prompt_tiles):
    m = pl.program_id(1)

    @pl.when(m == 0)
    def _():
        wbf_ref[...] = wo_ref[...].astype(BF16)

    y = _dot(mg_ref[...], wbf_ref[...])

    @pl.when(m < n_prompt_tiles)
    def _():
        x1_ref[...] = xp_ref[...] + y

    @pl.when(m >= n_prompt_tiles)
    def _():
        x1_ref[...] = xs_ref[...] + y


def _oproj(mg, xp, xs, w_o, *, tm):
    m_rows, d = mg.shape
    cb = 2 * CB
    nb = d // cb
    n_prompt_tiles = xp.shape[0] // tm
    n_tiles = m_rows // tm
    assert xp.shape[0] % tm == 0 and xs.shape[0] % tm == 0
    return pl.pallas_call(
        functools.partial(_oproj_kernel, n_prompt_tiles=n_prompt_tiles),
        grid=(nb, n_tiles),
        in_specs=[
            pl.BlockSpec((tm, d), lambda j, m: (m, 0)),
            pl.BlockSpec((tm, cb), lambda j, m: (jnp.minimum(m, n_prompt_tiles - 1), j)),
            pl.BlockSpec((tm, cb), lambda j, m: (jnp.maximum(m - n_prompt_tiles, 0), j)),
            pl.BlockSpec((d, cb), lambda j, m: (0, j)),
        ],
        out_specs=pl.BlockSpec((tm, cb), lambda j, m: (m, j)),
        out_shape=jax.ShapeDtypeStruct((m_rows, d), F32),
        scratch_shapes=[pltpu.VMEM((d, cb), BF16)],
        compiler_params=_params(2),
        name="oproj",
    )(mg, xp, xs, w_o)


def _router_kernel(x1_ref, g_ref, wt_ref, bt_ref, xn_ref, eid_ref, wts_ref, *, n_groups):
    xn = _rmsnorm_rows(x1_ref[...], g_ref[...])
    xn_ref[...] = _pack_halves(xn)
    lg = lax.dot_general(wt_ref[...].astype(BF16), xn.astype(BF16), (((1,), (1,)), ((), ())),
                         preferred_element_type=F32) + bt_ref[...]
    rows = xn.shape[0]
    row8 = lax.broadcasted_iota(I32, (SUBLANES, rows), 0).astype(F32)

    def first_index_of(v, vmax):
        return jnp.min(jnp.where(v == vmax, row8, float(SUBLANES)), axis=0, keepdims=True)

    gl = jnp.where(row8 < float(n_groups), lg[0:SUBLANES, :], -jnp.inf)
    ge = jnp.exp(gl - jnp.max(gl, axis=0, keepdims=True))
    gprob = ge / jnp.sum(ge, axis=0, keepdims=True)
    g_p = jnp.max(gprob, axis=0, keepdims=True)
    g_idx = first_index_of(gprob, g_p)

    esel = jnp.zeros((SUBLANES, rows), F32)
    for k in range(n_groups):
        esel = jnp.where(g_idx == float(k), lg[(k + 1) * SUBLANES:(k + 2) * SUBLANES, :], esel)
    ee = jnp.exp(esel - jnp.max(esel, axis=0, keepdims=True))
    ep = ee / jnp.sum(ee, axis=0, keepdims=True)
    p1 = jnp.max(ep, axis=0, keepdims=True)
    i1 = first_index_of(ep, p1)
    ep_rest = jnp.where(row8 == i1, -1.0, ep)
    p2 = jnp.max(ep_rest, axis=0, keepdims=True)
    i2 = first_index_of(ep_rest, p2)

    den = p1 + p2
    wts_ref[0:1, :] = p1 / den * g_p
    wts_ref[1:2, :] = p2 / den * g_p
    eid_ref[0:1, :] = (g_idx * float(SUBLANES) + i1).astype(I32)
    eid_ref[1:2, :] = (g_idx * float(SUBLANES) + i2).astype(I32)


def _router(x1, g2, wt, bt, *, n_groups):
    m_rows, d = x1.shape
    return pl.pallas_call(
        functools.partial(_router_kernel, n_groups=n_groups),
        grid=(m_rows // TR,),
        in_specs=[
            pl.BlockSpec((TR, d), lambda i: (i, 0)),
            pl.BlockSpec((1, d), lambda i: (0, 0)),
            pl.BlockSpec(wt.shape, lambda i: (0, 0)),
            pl.BlockSpec(bt.shape, lambda i: (0, 0)),
        ],
        out_specs=[
            pl.BlockSpec((TR, d // 2), lambda i: (i, 0)),
            pl.BlockSpec((TOP_K, TR), lambda i: (0, i)),
            pl.BlockSpec((TOP_K, TR), lambda i: (0, i)),
        ],
        out_shape=[
            jax.ShapeDtypeStruct((m_rows, d // 2), jnp.uint32),
            jax.ShapeDtypeStruct((TOP_K, m_rows), I32),
            jax.ShapeDtypeStruct((TOP_K, m_rows), F32),
        ],
        compiler_params=_params(1),
        name="router",
    )(x1, g2, wt, bt)


INFO_TILE_EXPERT, INFO_NUM_TILES, INFO_VALID_END, INFO_END_TILE, INFO_ROWS = 0, 1, 2, 3, SUBLANES


def _positions_kernel(eid_ref, pos_ref, info_ref, exc_ref, *, n_experts, tile_shift):
    m_rows = eid_ref.shape[1]
    chunk = 2 * LANES
    n_chunks = m_rows // chunk
    tile = 1 << tile_shift
    e_col = lax.broadcasted_iota(I32, (n_experts, chunk), 0)
    upper = (lax.broadcasted_iota(I32, (chunk, chunk), 0)
             < lax.broadcasted_iota(I32, (chunk, chunk), 1)).astype(BF16)

    def onehots(c):
        off = pl.multiple_of(c * chunk, chunk)
        ids = eid_ref[:, pl.ds(off, chunk)]
        return off, [(e_col == ids[k:k + 1, :]) for k in range(TOP_K)]

    def count(c, carry):
        off, hot = onehots(c)
        used = jnp.where(hot[0] | hot[1], 1.0, 0.0)
        exc_ref[:, pl.ds(off, chunk)] = _dot(used.astype(BF16), upper) + carry
        return carry + jnp.sum(used, axis=1, keepdims=True)

    counts = lax.fori_loop(0, n_chunks, count, jnp.zeros((n_experts, 1), F32))

    lane = lax.broadcasted_iota(I32, (n_experts, LANES), 1)
    sub = lax.broadcasted_iota(I32, (n_experts, LANES), 0)

    def to_lanes(col):
        return jnp.sum(jnp.where(lane == sub, col, 0.0), axis=0, keepdims=True)

    n_tiles_e = ((counts.astype(I32) + (tile - 1)) >> tile_shift).astype(F32)
    first_tile = jnp.sum(jnp.where(lane < sub, to_lanes(n_tiles_e), 0.0), axis=1, keepdims=True)
    end_tile = first_tile + n_tiles_e
    first_row = first_tile * float(tile)

    def place(c, _):
        off, hot = onehots(c)
        dest = exc_ref[:, pl.ds(off, chunk)] + first_row
        for k in range(TOP_K):
            pos_ref[k:k + 1, pl.ds(off, chunk)] = jnp.sum(
                jnp.where(hot[k], dest, 0.0), axis=0, keepdims=True).astype(I32)
        return 0

    lax.fori_loop(0, n_chunks, place, 0)

    tile_expert = jnp.sum(jnp.where(end_tile <= lane.astype(F32), 1.0, 0.0), axis=0, keepdims=True)
    num_tiles = jnp.max(end_tile, axis=0, keepdims=True)
    valid_end = to_lanes(first_row + counts)
    info_ref[...] = jnp.zeros(info_ref.shape, I32)
    info_ref[INFO_TILE_EXPERT:INFO_TILE_EXPERT + 1, :] = jnp.minimum(
        tile_expert, float(n_experts - 1)).astype(I32)
    info_ref[INFO_NUM_TILES:INFO_NUM_TILES + 1, :] = jnp.broadcast_to(num_tiles, (1, LANES)).astype(I32)
    info_ref[INFO_VALID_END:INFO_VALID_END + 1, :] = valid_end.astype(I32)
    info_ref[INFO_END_TILE:INFO_END_TILE + 1, :] = to_lanes(end_tile).astype(I32)


def _positions(eid, *, n_experts):
    m_rows = eid.shape[1]
    tile_shift = T_EXP.bit_length() - 1
    return pl.pallas_call(
        functools.partial(_positions_kernel, n_experts=n_experts, tile_shift=tile_shift),
        out_shape=[
            jax.ShapeDtypeStruct((TOP_K, m_rows), I32),
            jax.ShapeDtypeStruct((INFO_ROWS, LANES), I32),
        ],
        scratch_shapes=[pltpu.VMEM((n_experts, m_rows), F32)],
        compiler_params=pltpu.CompilerParams(vmem_limit_bytes=VMEM_LIMIT_BYTES),
        name="positions",
    )(eid)


def _info(info_ref, row, lane):
    return info_ref[row * LANES + lane]


def _tile_state(i, info_ref):
    n_used = _info(info_ref, INFO_NUM_TILES, 0)
    ic = jnp.minimum(i, n_used - 1)
    e = _info(info_ref, INFO_TILE_EXPERT, ic)
    prev = _info(info_ref, INFO_TILE_EXPERT, jnp.maximum(ic - 1, 0))
    first = jnp.logical_or(ic == 0, e != prev)
    n_valid = jnp.minimum(_info(info_ref, INFO_VALID_END, e) - ic * T_EXP, T_EXP)
    return i < n_used, ic, e, first, n_valid


def _stream_expert_weights(i, info_ref, e, first, w_refs, wf_ref, wbf_ref, wsem, slot_ref):
    n_used = _info(info_ref, INFO_NUM_TILES, 0)
    rows = wf_ref.shape[2]
    chunk = min(rows, 512)

    def copies(expert, slot):
        return [pltpu.make_async_copy(w.at[expert], wf_ref.at[slot, j], wsem.at[slot])
                for j, w in enumerate(w_refs)]

    @pl.when(i == 0)
    def _():
        slot_ref[0] = 0
        for cp in copies(e, 0):
            cp.start(priority=1)

    @pl.when(jnp.logical_and(first, i > 0))
    def _():
        slot_ref[0] = 1 - slot_ref[0]

    @pl.when(first)
    def _():
        slot = slot_ref[0]
        for cp in copies(e, slot):
            cp.wait()
        for j in range(len(w_refs)):
            def cast(c, _, j=j):
                r0 = pl.multiple_of(c * chunk, chunk)
                wbf_ref[j, pl.ds(r0, chunk), :] = wf_ref[slot, j, pl.ds(r0, chunk), :].astype(BF16)
                return 0
            lax.fori_loop(0, rows // chunk, cast, 0)
        next_tile = _info(info_ref, INFO_END_TILE, e)

        @pl.when(next_tile < n_used)
        def _():
            for cp in copies(_info(info_ref, INFO_TILE_EXPERT, next_tile), 1 - slot):
                cp.start(priority=1)


def _expert_up_kernel(info_ref, pos_ref, xn_ref, wg_ref, wu_ref, h_ref,
                      xbuf_ref, wf_ref, wbf_ref, inv_ref, slot_ref, xsem, wsem, *, m_rows):
    i = pl.program_id(0)
    used, _, e, first, n_valid = _tile_state(i, info_ref)
    n_used = _info(info_ref, INFO_NUM_TILES, 0)

    def gather(tile, slot, n_rows, start):
        n_groups = (n_rows + (SUBLANES - 1)) >> (SUBLANES.bit_length() - 1)
        if start:
            first_row = tile * T_EXP
            last_row = first_row + n_rows - 1

            def body(g, _):
                for j in range(SUBLANES):
                    r = g * SUBLANES + j
                    src = inv_ref[jnp.minimum(first_row + r, last_row)]
                    pltpu.make_async_copy(xn_ref.at[pl.ds(src, 1), :], xbuf_ref.at[slot, pl.ds(r, 1), :],
                                          xsem.at[slot]).start()
                return 0
            lax.fori_loop(0, n_groups, body, 0)
        else:
            for bit in range((T_EXP // SUBLANES).bit_length()):
                n = SUBLANES << bit

                @pl.when((n_groups & (1 << bit)) != 0)
                def _():
                    pltpu.make_async_copy(xn_ref.at[pl.ds(0, n), :], xbuf_ref.at[slot, pl.ds(0, n), :],
                                          xsem.at[slot]).wait()

    @pl.when(i == 0)
    def _():
        def invert(t, _):
            for k in range(TOP_K):
                inv_ref[pos_ref[k * m_rows + t]] = t
            return 0
        lax.fori_loop(0, m_rows, invert, 0, unroll=8)
        xbuf_ref[...] = jnp.zeros(xbuf_ref.shape, jnp.uint32)
        gather(0, 0, n_valid, True)

    @pl.when(used)
    def _():
        _stream_expert_weights(i, info_ref, e, first, (wg_ref, wu_ref), wf_ref, wbf_ref, wsem, slot_ref)
        slot = i & 1
        gather(i, slot, n_valid, False)

        @pl.when(i + 1 < n_used)
        def _():
            gather(i + 1, 1 - slot, _tile_state(i + 1, info_ref)[4], True)

        x_lo, x_hi = [x.astype(BF16) for x in _unpack_halves(xbuf_ref[slot])]
        half = x_lo.shape[1]

        def proj(j):
            return _dot(x_lo, wbf_ref[j, 0:half, :]) + _dot(x_hi, wbf_ref[j, half:, :])

        h = jax.nn.silu(proj(0)) * proj(1)
        h_ref[...] = h.astype(BF16)

    @pl.when(jnp.logical_not(used))
    def _():
        h_ref[...] = jnp.zeros(h_ref.shape, BF16)


def _expert_up(info_flat, pos_flat, xn, w_gate, w_up, *, n_tiles):
    m_rows, half = xn.shape
    _, d, d_exp = w_gate.shape
    assert d == 2 * half
    p_rows = n_tiles * T_EXP
    any_spec = pl.BlockSpec(memory_space=pl.ANY)
    return pl.pallas_call(
        functools.partial(_expert_up_kernel, m_rows=m_rows),
        grid_spec=pltpu.PrefetchScalarGridSpec(
            num_scalar_prefetch=2,
            grid=(n_tiles,),
            in_specs=[any_spec, any_spec, any_spec],
            out_specs=pl.BlockSpec((T_EXP, d_exp), lambda i, info, pos: (i, 0)),
            scratch_shapes=[
                pltpu.VMEM((2, T_EXP, half), jnp.uint32),
                pltpu.VMEM((2, 2, d, d_exp), F32),
                pltpu.VMEM((2, d, d_exp), BF16),
                pltpu.SMEM((p_rows,), I32),
                pltpu.SMEM((1,), I32),
                pltpu.SemaphoreType.DMA((2,)),
                pltpu.SemaphoreType.DMA((2,)),
            ],
        ),
        out_shape=jax.ShapeDtypeStruct((p_rows, d_exp), BF16),
        compiler_params=_params(1),
        name="expert_up",
    )(info_flat, pos_flat, xn, w_gate, w_up)


def _expert_down_kernel(info_ref, h_ref, wd_ref, y_ref, wf_ref, wbf_ref, slot_ref, wsem):
    i = pl.program_id(0)
    used, _, e, first, _ = _tile_state(i, info_ref)

    @pl.when(used)
    def _():
        _stream_expert_weights(i, info_ref, e, first, (wd_ref,), wf_ref, wbf_ref, wsem, slot_ref)
        y_ref[...] = _pack_halves(_dot(h_ref[...], wbf_ref[0]))

    @pl.when(jnp.logical_not(used))
    def _():
        y_ref[...] = jnp.zeros(y_ref.shape, jnp.uint32)


def _expert_down(info_flat, h, w_down, *, n_tiles):
    p_rows, d_exp = h.shape
    d = w_down.shape[-1]
    return pl.pallas_call(
        _expert_down_kernel,
        grid_spec=pltpu.PrefetchScalarGridSpec(
            num_scalar_prefetch=1,
            grid=(n_tiles,),
            in_specs=[
                pl.BlockSpec((T_EXP, d_exp), lambda i, info: (_tile_state(i, info)[1], 0)),
                pl.BlockSpec(memory_space=pl.ANY),
            ],
            out_specs=pl.BlockSpec((T_EXP, d // 2), lambda i, info: (i, 0)),
            scratch_shapes=[
                pltpu.VMEM((2, 1, d_exp, d), F32),
                pltpu.VMEM((1, d_exp, d), BF16),
                pltpu.SMEM((1,), I32),
                pltpu.SemaphoreType.DMA((2,)),
            ],
        ),
        out_shape=jax.ShapeDtypeStruct((p_rows, d // 2), jnp.uint32),
        compiler_params=_params(1),
        name="expert_down",
    )(info_flat, h, w_down)


def _combine_kernel(pos_ref, x1_ref, w_ref, g_ref, y_ref, out_ref, ybuf_ref, sem, *, m_rows, row0, n_steps):
    i = pl.program_id(0)

    def gather_start(step):
        base = row0 + step * TR
        slot = step & 1

        def body(r, _):
            for k in range(TOP_K):
                src = pos_ref[k * m_rows + base + r]
                pltpu.make_async_copy(y_ref.at[pl.ds(src, 1), :], ybuf_ref.at[slot, k, pl.ds(r, 1), :],
                                      sem.at[slot]).start(priority=k % 2)
            return 0
        lax.fori_loop(0, TR, body, 0, unroll=8)

    def gather_wait(slot):
        for k in range(TOP_K):
            pltpu.make_async_copy(y_ref.at[pl.ds(0, TR), :], ybuf_ref.at[slot, k], sem.at[slot]).wait()

    @pl.when(i == 0)
    def _():
        gather_start(0)

    @pl.when(i + 1 < n_steps)
    def _():
        gather_start(i + 1)

    slot = i & 1
    gather_wait(slot)
    w = w_ref[...]
    y0, y1 = _unpack_halves(ybuf_ref[slot, 0]), _unpack_halves(ybuf_ref[slot, 1])
    moe = jnp.concatenate([w[:, 0:1] * y0[k] + w[:, 1:2] * y1[k] for k in range(2)], axis=-1)
    out_ref[...] = _rmsnorm_rows(x1_ref[...] + moe, g_ref[...])


def _combine(pos_flat, x1, wts_t, norm_f, y, *, row0, n_rows):
    m_rows, d = x1.shape
    tile0 = row0 // TR
    n_steps = n_rows // TR
    return pl.pallas_call(
        functools.partial(_combine_kernel, m_rows=m_rows, row0=row0, n_steps=n_steps),
        grid_spec=pltpu.PrefetchScalarGridSpec(
            num_scalar_prefetch=1,
            grid=(n_steps,),
            in_specs=[
                pl.BlockSpec((TR, d), lambda i, pos: (tile0 + i, 0)),
                pl.BlockSpec((TR, TOP_K), lambda i, pos: (tile0 + i, 0)),
                pl.BlockSpec((1, d), lambda i, pos: (0, 0)),
                pl.BlockSpec(memory_space=pl.ANY),
            ],
            out_specs=pl.BlockSpec((TR, d), lambda i, pos: (i, 0)),
            scratch_shapes=[pltpu.VMEM((2, TOP_K, TR, d // 2), jnp.uint32), pltpu.SemaphoreType.DMA((2,))],
        ),
        out_shape=jax.ShapeDtypeStruct((n_rows, d), F32),
        compiler_params=_params(1),
        name="combine",
    )(pos_flat, x1, wts_t, norm_f, y)


def _sample_headers(state, k_width):
    n, _, w = state.shape
    padded = jnp.pad(state, ((0, 0), (SUBLANES - (k_width - 1), 0), (0, 0)))
    return padded.reshape(n * SUBLANES, w)


def kernel(x_prompt, x_sample, state_conv_a, state_conv_r, state_h, meta_tokens, norm1, w_in, conv_a_w, conv_r_w,
           conv_r_b, lru_wa, lru_ba, lru_wx, lru_bx, lru_lam, w_br_a, w_br_r, w_o, norm2, w_group, b_group,
           w_router, b_router, w_gate, w_up, w_down, norm_f):
    batch, seq, d = x_prompt.shape
    dec_batch, dec_seq, _ = x_sample.shape
    depth = norm1.shape[0]
    width = w_br_a.shape[1]
    n_groups = w_group.shape[-1]
    n_experts = w_router.shape[-1]
    n_meta = meta_tokens.shape[0]
    assert depth == 1, "meta rows are folded into an initial state, which only carries one layer"
    assert dec_seq == SUBLANES and n_meta % SUBLANES == 0 and n_meta >= SUBLANES
    assert seq % TM == 0 and (dec_batch * dec_seq) % TM == 0
    assert n_groups <= SUBLANES and n_experts == n_groups * SUBLANES
    assert width % CB == 0 and d % (2 * CB) == 0 and CB % lru_wa.shape[-1] == 0

    n_prompt_rows = batch * seq
    n_sample_rows = dec_batch * dec_seq
    m_rows = n_prompt_rows + n_sample_rows
    n_prompt_tiles = n_prompt_rows // TM
    n_tiles = m_rows // TM
    tiles_per_seq = seq // TM
    n_exp_tiles = (TOP_K * m_rows) // T_EXP + n_experts
    assert n_exp_tiles <= LANES

    xp = x_prompt.reshape(n_prompt_rows, d)
    xs = x_sample.reshape(n_sample_rows, d)
    row = lambda v: v.reshape(1, -1)

    u, um = _norm1(xp, xs, meta_tokens, row(norm1[0]), n_prompt_tiles, n_tiles)

    def mixer_geometry(tm):
        assert seq % tm == 0 and n_sample_rows % tm == 0 and tm % MIX_CHUNK == 0 and MIX_CHUNK % SUBLANES == 0
        return dict(tm=tm, width=width, n_seq=batch, tiles_per_seq=seq // tm,
                    n_prompt_tiles=n_prompt_rows // tm, n_tiles=m_rows // tm)

    out_a, tail_a_p, tail_a_s = _mixer_a(
        u, um, w_in[0], conv_a_w[0], _sample_headers(state_conv_a[0], K_A), **mixer_geometry(TM_MIX_A))
    out_r, tail_r_p, tail_r_s, h_p, h_s = _mixer_r(
        u, um, w_in[0], conv_r_w[0], row(conv_r_b[0]), lru_wa[0], row(lru_ba[0]), lru_wx[0], row(lru_bx[0]),
        row(lru_lam[0]), _sample_headers(state_conv_r[0], K_R), state_h[0], seg0=3, **mixer_geometry(TM_MIX_R))

    gates = _gates(u, w_in[0], col0=5 * width, n_cols=2 * d, tm=TM_OPROJ)
    merged = _merge(gates, out_a, out_r, w_br_a[0], w_br_r[0], tm=TM_OPROJ)
    x1 = _oproj(merged, xp, xs, w_o[0], tm=TM_OPROJ)

    wt = jnp.concatenate([jnp.pad(w_group[0].T, ((0, SUBLANES - n_groups), (0, 0))), w_router[0].T], axis=0)
    bt = jnp.concatenate([jnp.pad(b_group[0], (0, SUBLANES - n_groups)), b_router[0]]).reshape(-1, 1)
    xn, eid, wts = _router(x1, row(norm2[0]), wt, bt, n_groups=n_groups)

    pos, info = _positions(eid, n_experts=n_experts)
    pos_flat = pos.reshape(-1)
    info_flat = info.reshape(-1)
    hid = _expert_up(info_flat, pos_flat, xn, w_gate[0], w_up[0], n_tiles=n_exp_tiles)
    y = _expert_down(info_flat, hid, w_down[0], n_tiles=n_exp_tiles)

    wts_t = wts.T
    g_f = row(norm_f)
    y_prompt = _combine(pos_flat, x1, wts_t, g_f, y, row0=0, n_rows=n_prompt_rows)
    y_sample = _combine(pos_flat, x1, wts_t, g_f, y, row0=n_prompt_rows, n_rows=n_sample_rows)

    def sample_tail(t, k):
        return t.reshape(dec_batch, SUBLANES, width)[:, SUBLANES - k:, :][None]

    return (
        y_prompt.reshape(batch, seq, d),
        y_sample.reshape(dec_batch, dec_seq, d),
        tail_a_p[:, SUBLANES - (K_A - 1):, :][None],
        tail_r_p[:, SUBLANES - (K_R - 1):, :][None],
        h_p[:, SUBLANES - 1, :][None],
        sample_tail(tail_a_s, K_A - 1),
        sample_tail(tail_r_s, K_R - 1),
        sample_tail(h_s, 1)[:, :, 0, :],
    )
```

```python
import functools

import jax
import jax.numpy as jnp
from jax import lax
from jax.experimental import pallas as pl
from jax.experimental.pallas import tpu as pltpu

F32, BF16, I32 = jnp.float32, jnp.bfloat16, jnp.int32

EPS = 1e-6
C_RG = 8.0
K_A = 3
K_R = 4
LRU_HEADS = 16
TOP_K = 2

SUBLANES = 8
LANES = 128
VMEM_LIMIT_BYTES = 56 * 1024 * 1024

TM = 512
TM_OPROJ = 1024
TM_MIX_A = 512
TM_MIX_R = 1024
MIX_CHUNK = 256
CB = 256
T_EXP = 256
TR = 256


def _dot(a, b):
    return jnp.dot(a, b, preferred_element_type=F32)


ROW_PAIR = 2


def _pack_halves(x):
    half = x.shape[1] // 2
    words = pltpu.pack_elementwise([x[:, :half], x[:, half:]], packed_dtype=BF16)
    return pltpu.bitcast(words, BF16)


def _unpack_halves(pairs):
    words = pltpu.bitcast(pairs, jnp.uint32)
    return [pltpu.unpack_elementwise(words, index=k, packed_dtype=BF16, unpacked_dtype=F32) for k in range(2)]


def _pair_rows(ref, token, n_tokens=1):
    start = ROW_PAIR * token
    if not isinstance(token, int):
        start = pl.multiple_of(start, ROW_PAIR)
    return ref.at[pl.ds(start, ROW_PAIR * n_tokens), :]


def _rmsnorm_rows(x, g):
    y = x * lax.rsqrt(jnp.mean(x * x, axis=-1, keepdims=True) + EPS)
    return y * g


def _params(n_axes):
    return pltpu.CompilerParams(dimension_semantics=("arbitrary",) * n_axes,
                                vmem_limit_bytes=VMEM_LIMIT_BYTES)


def _norm1_kernel(xp_ref, xs_ref, meta_ref, g_ref, u_ref, um_ref, *, n_prompt_tiles):
    i = pl.program_id(0)
    g = g_ref[...]

    @pl.when(i < n_prompt_tiles)
    def _():
        u_ref[...] = _rmsnorm_rows(xp_ref[...], g).astype(BF16)

    @pl.when(i >= n_prompt_tiles)
    def _():
        u_ref[...] = _rmsnorm_rows(xs_ref[...], g).astype(BF16)

    @pl.when(i == 0)
    def _():
        um_ref[...] = _rmsnorm_rows(meta_ref[...], g).astype(BF16)


def _norm1(xp, xs, meta, g, n_prompt_tiles, n_tiles):
    d = xp.shape[1]
    return pl.pallas_call(
        functools.partial(_norm1_kernel, n_prompt_tiles=n_prompt_tiles),
        grid=(n_tiles,),
        in_specs=[
            pl.BlockSpec((TM, d), lambda i: (jnp.minimum(i, n_prompt_tiles - 1), 0)),
            pl.BlockSpec((TM, d), lambda i: (jnp.maximum(i - n_prompt_tiles, 0), 0)),
            pl.BlockSpec(meta.shape, lambda i: (0, 0)),
            pl.BlockSpec((1, d), lambda i: (0, 0)),
        ],
        out_specs=[
            pl.BlockSpec((TM, d), lambda i: (i, 0)),
            pl.BlockSpec(meta.shape, lambda i: (0, 0)),
        ],
        out_shape=[
            jax.ShapeDtypeStruct((n_tiles * TM, d), BF16),
            jax.ShapeDtypeStruct(meta.shape, BF16),
        ],
        compiler_params=_params(1),
        name="norm1",
    )(xp, xs, meta, g)


def _conv_taps(window, cw, k_width):
    acc = window(0) * cw[0:1, :]
    for k in range(1, k_width):
        acc = acc + window(k) * cw[k:k + 1, :]
    return acc


def _conv_flat(ext_ref, values, cw, k_width, r0):
    rows = values.shape[0]
    ext_ref[SUBLANES + r0:SUBLANES + r0 + rows, :] = values
    base = SUBLANES - (k_width - 1) + r0
    return _conv_taps(lambda k: ext_ref[pl.ds(base + k, rows), :], cw, k_width)


def _conv_grouped(ext3_ref, values, header, cw, k_width, s0):
    rows, cols = values.shape
    n_seq = rows // SUBLANES
    ext3_ref[s0:s0 + n_seq, 0:SUBLANES, :] = header.reshape(n_seq, SUBLANES, cols)
    ext3_ref[s0:s0 + n_seq, SUBLANES:, :] = values.reshape(n_seq, SUBLANES, cols)
    base = SUBLANES - (k_width - 1)
    return _conv_taps(
        lambda k: ext3_ref[s0:s0 + n_seq, base + k:base + k + SUBLANES, :].reshape(rows, cols), cw, k_width)


def _mixer_a_kernel(u_ref, um_ref, wb_ref, wc_ref, wv_ref, cw_ref, hdr_ref,
                    oa_ref, tailp_ref, tails_ref,
                    wbf_ref, ext_ref, ext3_ref, carry_ref, mhdr_ref,
                    *, tm, tiles_per_seq, n_prompt_tiles, n_meta):
    m = pl.program_id(1)
    cw = cw_ref[...]
    ch = MIX_CHUNK
    n_chunks = tm // ch

    @pl.when(m == 0)
    def _():
        wbf_ref[0] = wb_ref[...].astype(BF16)
        wbf_ref[1] = wc_ref[...].astype(BF16)
        wbf_ref[2] = wv_ref[...].astype(BF16)
        um = um_ref[...]
        cv_meta = _dot(um, wbf_ref[1]) * _dot(um, wbf_ref[2])
        mhdr_ref[...] = cv_meta[n_meta - SUBLANES:, :]

    def products(r0):
        u = u_ref[r0:r0 + ch, :]
        return _dot(u, wbf_ref[0]), _dot(u, wbf_ref[1]) * _dot(u, wbf_ref[2])

    @pl.when(m < n_prompt_tiles)
    def _():
        @pl.when((m % tiles_per_seq) == 0)
        def _():
            carry_ref[...] = mhdr_ref[...]

        ext_ref[0:SUBLANES, :] = carry_ref[...]
        for c in range(n_chunks):
            r0 = c * ch
            zb, cv = products(r0)
            conv = _conv_flat(ext_ref, cv, cw, K_A, r0)
            oa_ref[r0:r0 + ch, :] = (zb * conv).astype(BF16)
        carry_ref[...] = ext_ref[tm:, :]
        tailp_ref[0] = ext_ref[tm:, :]

    @pl.when(m >= n_prompt_tiles)
    def _():
        for c in range(n_chunks):
            r0 = c * ch
            zb, cv = products(r0)
            conv = _conv_grouped(ext3_ref, cv, hdr_ref[r0:r0 + ch, :], cw, K_A, r0 // SUBLANES)
            oa_ref[r0:r0 + ch, :] = (zb * conv).astype(BF16)
            tails_ref[r0:r0 + ch, :] = cv


def _mixer_a(u, um, w_in, conv_w, hdr_s, *, tm, width, n_seq, tiles_per_seq, n_prompt_tiles, n_tiles):
    d = u.shape[1]
    nb = width // CB
    n_meta = um.shape[0]
    last_seq = n_seq - 1

    def wspec(seg):
        return pl.BlockSpec((d, CB), lambda j, m, seg=seg: (0, seg * nb + j))

    return pl.pallas_call(
        functools.partial(_mixer_a_kernel, tm=tm, tiles_per_seq=tiles_per_seq,
                          n_prompt_tiles=n_prompt_tiles, n_meta=n_meta),
        grid=(nb, n_tiles),
        in_specs=[
            pl.BlockSpec((tm, d), lambda j, m: (m, 0)),
            pl.BlockSpec(um.shape, lambda j, m: (0, 0)),
            wspec(0), wspec(1), wspec(2),
            pl.BlockSpec((K_A, CB), lambda j, m: (0, j)),
            pl.BlockSpec((tm, CB), lambda j, m: (jnp.maximum(m - n_prompt_tiles, 0), j)),
        ],
        out_specs=[
            pl.BlockSpec((tm, CB), lambda j, m: (m, j)),
            pl.BlockSpec((1, SUBLANES, CB),
                         lambda j, m: (jnp.minimum(m // tiles_per_seq, last_seq), 0, j)),
            pl.BlockSpec((tm, CB), lambda j, m: (jnp.maximum(m - n_prompt_tiles, 0), j)),
        ],
        out_shape=[
            jax.ShapeDtypeStruct((n_tiles * tm, width), BF16),
            jax.ShapeDtypeStruct((n_seq, SUBLANES, width), F32),
            jax.ShapeDtypeStruct(hdr_s.shape, F32),
        ],
        scratch_shapes=[
            pltpu.VMEM((3, d, CB), BF16),
            pltpu.VMEM((tm + SUBLANES, CB), F32),
            pltpu.VMEM((tm // SUBLANES, 2 * SUBLANES, CB), F32),
            pltpu.VMEM((SUBLANES, CB), F32),
            pltpu.VMEM((SUBLANES, CB), F32),
        ],
        compiler_params=_params(2),
        name="mixer_a",
    )(u, um, w_in, w_in, w_in, conv_w, hdr_s)


def _softplus(x):
    return jnp.maximum(x, 0.0) + jnp.log1p(jnp.exp(-jnp.abs(x)))


def _lru_kernel(u_ref, um_ref, wx_ref, wy_ref, cw_ref, cbias_ref, wa_ref, ba_ref, wi_ref, bi_ref, lam_ref,
                hdr_ref, h0_ref,
                or_ref, tailp_ref, tails_ref, hp_ref, hs_ref,
                wbf_ref, gbf_ref, ext_ref, ext3_ref, carry_ref, hcarry_ref,
                mhdr_ref, mh_ref,
                *, tm, tiles_per_seq, n_prompt_tiles, n_meta, lru_block):
    m = pl.program_id(1)
    cw = cw_ref[...]
    cbias = cbias_ref[...]
    heads = CB // lru_block
    ch = MIX_CHUNK
    n_chunks = tm // ch
    row8 = lax.broadcasted_iota(I32, (SUBLANES, CB), 0)

    def gate_terms(xc):
        xb = xc.astype(BF16)
        ra, ri = [], []
        for hh in range(heads):
            xh = xb[:, hh * lru_block:(hh + 1) * lru_block]
            ra.append(_dot(xh, gbf_ref[0, hh]))
            ri.append(_dot(xh, gbf_ref[1, hh]))
        r = jax.nn.sigmoid(jnp.concatenate(ra, axis=-1) + ba_ref[...])
        i = jax.nn.sigmoid(jnp.concatenate(ri, axis=-1) + bi_ref[...])
        log_a = -C_RG * r * _softplus(-lam_ref[...])
        a = jnp.exp(log_a)
        b = jnp.sqrt((1.0 + a * a) * jnp.tanh(-log_a)) * i * xc
        return a, b

    def scan(a, b, h_prev, h0_row=None):
        out = []
        for g in range(a.shape[0] // SUBLANES):
            a8 = a[g * SUBLANES:(g + 1) * SUBLANES, :]
            b8 = b[g * SUBLANES:(g + 1) * SUBLANES, :]
            for dist in (1, 2, 4):
                keep = row8 >= dist
                b8 = jnp.where(keep, a8 * pltpu.roll(b8, dist, 0) + b8, b8)
                a8 = jnp.where(keep, a8 * pltpu.roll(a8, dist, 0), a8)
            h8 = a8 * (h_prev if h0_row is None else h0_row(g)) + b8
            out.append(h8)
            h_prev = h8[SUBLANES - 1:SUBLANES, :]
        return jnp.concatenate(out, axis=0), h_prev

    @pl.when(m == 0)
    def _():
        wbf_ref[0] = wx_ref[...].astype(BF16)
        wbf_ref[1] = wy_ref[...].astype(BF16)
        gbf_ref[0] = wa_ref[...].astype(BF16)
        gbf_ref[1] = wi_ref[...].astype(BF16)
        zx_meta = _dot(um_ref[...], wbf_ref[0])
        ext_ref[0:SUBLANES, :] = jnp.zeros((SUBLANES, CB), F32)
        a, b = gate_terms(_conv_flat(ext_ref, zx_meta, cw, K_R, 0) + cbias)
        _, h_meta = scan(a, b, jnp.zeros((1, CB), F32))
        mh_ref[...] = jnp.broadcast_to(h_meta, (SUBLANES, CB))
        mhdr_ref[...] = zx_meta[n_meta - SUBLANES:, :]

    def products(r0):
        u = u_ref[r0:r0 + ch, :]
        return _dot(u, wbf_ref[0]), _dot(u, wbf_ref[1])

    @pl.when(m < n_prompt_tiles)
    def _():
        @pl.when((m % tiles_per_seq) == 0)
        def _():
            carry_ref[...] = mhdr_ref[...]
            hcarry_ref[...] = mh_ref[...]

        ext_ref[0:SUBLANES, :] = carry_ref[...]
        h_prev = hcarry_ref[0:1, :]
        for c in range(n_chunks):
            r0 = c * ch
            zx, zy = products(r0)
            a, b = gate_terms(_conv_flat(ext_ref, zx, cw, K_R, r0) + cbias)
            h, h_prev = scan(a, b, h_prev)
            or_ref[r0:r0 + ch, :] = (h * jax.nn.gelu(zy)).astype(BF16)
        hcarry_ref[0:1, :] = h_prev
        carry_ref[...] = ext_ref[tm:, :]
        tailp_ref[0] = ext_ref[tm:, :]
        hp_ref[0] = h[ch - SUBLANES:, :]

    @pl.when(m >= n_prompt_tiles)
    def _():
        for c in range(n_chunks):
            r0 = c * ch
            s0 = r0 // SUBLANES
            zx, zy = products(r0)
            xc = _conv_grouped(ext3_ref, zx, hdr_ref[r0:r0 + ch, :], cw, K_R, s0) + cbias
            a, b = gate_terms(xc)
            h, _ = scan(a, b, None, lambda g, s0=s0: h0_ref[s0 + g:s0 + g + 1, :])
            or_ref[r0:r0 + ch, :] = (h * jax.nn.gelu(zy)).astype(BF16)
            tails_ref[r0:r0 + ch, :] = zx
            hs_ref[r0:r0 + ch, :] = h


def _mixer_r(u, um, w_in, conv_w, conv_b, lru_wa, lru_ba, lru_wx, lru_bx, lru_lam, hdr_s, h0_s,
             *, tm, width, seg0, n_seq, tiles_per_seq, n_prompt_tiles, n_tiles):
    d = u.shape[1]
    nb = width // CB
    n_meta = um.shape[0]
    last_seq = n_seq - 1
    lru_block = lru_wa.shape[-1]
    heads = CB // lru_block
    seqs_per_tile = tm // SUBLANES

    def wspec(seg):
        return pl.BlockSpec((d, CB), lambda j, m, seg=seg: (0, seg * nb + j))

    def vec():
        return pl.BlockSpec((1, CB), lambda j, m: (0, j))

    def gspec():
        return pl.BlockSpec((heads, lru_block, lru_block), lambda j, m: (j, 0, 0))

    def sample_rows():
        return pl.BlockSpec((tm, CB), lambda j, m: (jnp.maximum(m - n_prompt_tiles, 0), j))

    def seq_tail():
        return pl.BlockSpec((1, SUBLANES, CB),
                            lambda j, m: (jnp.minimum(m // tiles_per_seq, last_seq), 0, j))

    return pl.pallas_call(
        functools.partial(_lru_kernel, tm=tm, tiles_per_seq=tiles_per_seq, n_prompt_tiles=n_prompt_tiles,
                          n_meta=n_meta, lru_block=lru_block),
        grid=(nb, n_tiles),
        in_specs=[
            pl.BlockSpec((tm, d), lambda j, m: (m, 0)),
            pl.BlockSpec(um.shape, lambda j, m: (0, 0)),
            wspec(seg0), wspec(seg0 + 1),
            pl.BlockSpec((K_R, CB), lambda j, m: (0, j)),
            vec(),
            gspec(), vec(), gspec(), vec(), vec(),
            sample_rows(),
            pl.BlockSpec((seqs_per_tile, CB), lambda j, m: (jnp.maximum(m - n_prompt_tiles, 0), j)),
        ],
        out_specs=[
            pl.BlockSpec((tm, CB), lambda j, m: (m, j)),
            seq_tail(), sample_rows(), seq_tail(), sample_rows(),
        ],
        out_shape=[
            jax.ShapeDtypeStruct((n_tiles * tm, width), BF16),
            jax.ShapeDtypeStruct((n_seq, SUBLANES, width), F32),
            jax.ShapeDtypeStruct(hdr_s.shape, F32),
            jax.ShapeDtypeStruct((n_seq, SUBLANES, width), F32),
            jax.ShapeDtypeStruct(hdr_s.shape, F32),
        ],
        scratch_shapes=[
            pltpu.VMEM((2, d, CB), BF16),
            pltpu.VMEM((2, heads, lru_block, lru_block), BF16),
            pltpu.VMEM((tm + SUBLANES, CB), F32),
            pltpu.VMEM((seqs_per_tile, 2 * SUBLANES, CB), F32),
            pltpu.VMEM((SUBLANES, CB), F32),
            pltpu.VMEM((SUBLANES, CB), F32),
            pltpu.VMEM((SUBLANES, CB), F32),
            pltpu.VMEM((SUBLANES, CB), F32),
        ],
        compiler_params=_params(2),
        name="mixer_r",
    )(u, um, w_in, w_in, conv_w, conv_b, lru_wa, lru_ba, lru_wx, lru_bx, lru_lam, hdr_s, h0_s)


def _gates_kernel(u_ref, w_ref, g_ref, wbf_ref):
    @pl.when(pl.program_id(1) == 0)
    def _():
        wbf_ref[...] = w_ref[...].astype(BF16)

    g_ref[...] = jax.nn.sigmoid(_dot(u_ref[...], wbf_ref[...]))


def _gates(u, w_in, *, col0, n_cols, tm):
    m_rows, d = u.shape
    cb = 2 * CB
    return pl.pallas_call(
        _gates_kernel,
        grid=(n_cols // cb, m_rows // tm),
        in_specs=[
            pl.BlockSpec((tm, d), lambda j, m: (m, 0)),
            pl.BlockSpec((d, cb), lambda j, m: (0, col0 // cb + j)),
        ],
        out_specs=pl.BlockSpec((tm, cb), lambda j, m: (m, j)),
        out_shape=jax.ShapeDtypeStruct((m_rows, n_cols), F32),
        scratch_shapes=[pltpu.VMEM((d, cb), BF16)],
        compiler_params=_params(2),
        name="gates",
    )(u, w_in)


def _merge_kernel(ga_ref, gr_ref, oa_ref, or_ref, wba_ref, wbr_ref, mg_ref, wb_bf):
    @pl.when(pl.program_id(1) == 0)
    def _():
        wb_bf[0] = wba_ref[...].astype(BF16)
        wb_bf[1] = wbr_ref[...].astype(BF16)

    mg = ga_ref[...] * _dot(oa_ref[...], wb_bf[0]) + gr_ref[...] * _dot(or_ref[...], wb_bf[1])
    mg_ref[...] = mg.astype(BF16)


def _merge(gates, out_a, out_r, w_br_a, w_br_r, *, tm):
    m_rows, width = out_a.shape
    d = w_br_a.shape[1]
    cb = 2 * CB
    nb = d // cb
    return pl.pallas_call(
        _merge_kernel,
        grid=(nb, m_rows // tm),
        in_specs=[
            pl.BlockSpec((tm, cb), lambda j, m: (m, j)),
            pl.BlockSpec((tm, cb), lambda j, m: (m, nb + j)),
            pl.BlockSpec((tm, width), lambda j, m: (m, 0)),
            pl.BlockSpec((tm, width), lambda j, m: (m, 0)),
            pl.BlockSpec((width, cb), lambda j, m: (0, j)),
            pl.BlockSpec((width, cb), lambda j, m: (0, j)),
        ],
        out_specs=pl.BlockSpec((tm, cb), lambda j, m: (m, j)),
        out_shape=jax.ShapeDtypeStruct((m_rows, d), BF16),
        scratch_shapes=[pltpu.VMEM((2, width, cb), BF16)],
        compiler_params=_params(2),
        name="merge",
    )(gates, gates, out_a, out_r, w_br_a, w_br_r)


def _oproj_kernel(mg_ref, xp_ref, xs_ref, wo_ref, x1_ref, wbf_ref, *, n_prompt_tiles):
    m = pl.program_id(1)

    @pl.when(m == 0)
    def _():
        wbf_ref[...] = wo_ref[...].astype(BF16)

    y = _dot(mg_ref[...], wbf_ref[...])

    @pl.when(m < n_prompt_tiles)
    def _():
        x1_ref[...] = xp_ref[...] + y

    @pl.when(m >= n_prompt_tiles)
    def _():
        x1_ref[...] = xs_ref[...] + y


def _oproj(mg, xp, xs, w_o, *, tm):
    m_rows, d = mg.shape
    cb = 2 * CB
    nb = d // cb
    n_prompt_tiles = xp.shape[0] // tm
    n_tiles = m_rows // tm
    assert xp.shape[0] % tm == 0 and xs.shape[0] % tm == 0
    return pl.pallas_call(
        functools.partial(_oproj_kernel, n_prompt_tiles=n_prompt_tiles),
        grid=(nb, n_tiles),
        in_specs=[
            pl.BlockSpec((tm, d), lambda j, m: (m, 0)),
            pl.BlockSpec((tm, cb), lambda j, m: (jnp.minimum(m, n_prompt_tiles - 1), j)),
            pl.BlockSpec((tm, cb), lambda j, m: (jnp.maximum(m - n_prompt_tiles, 0), j)),
            pl.BlockSpec((d, cb), lambda j, m: (0, j)),
        ],
        out_specs=pl.BlockSpec((tm, cb), lambda j, m: (m, j)),
        out_shape=jax.ShapeDtypeStruct((m_rows, d), F32),
        scratch_shapes=[pltpu.VMEM((d, cb), BF16)],
        compiler_params=_params(2),
        name="oproj",
    )(mg, xp, xs, w_o)


def _router_kernel(x1_ref, g_ref, wt_ref, bt_ref, xn_ref, eid_ref, wts_ref, *, n_groups):
    xn = _rmsnorm_rows(x1_ref[...], g_ref[...])
    xn_ref[...] = _pack_halves(xn)
    lg = lax.dot_general(wt_ref[...].astype(BF16), xn.astype(BF16), (((1,), (1,)), ((), ())),
                         preferred_element_type=F32) + bt_ref[...]
    rows = xn.shape[0]
    row8 = lax.broadcasted_iota(I32, (SUBLANES, rows), 0).astype(F32)

    def first_index_of(v, vmax):
        return jnp.min(jnp.where(v == vmax, row8, float(SUBLANES)), axis=0, keepdims=True)

    gl = jnp.where(row8 < float(n_groups), lg[0:SUBLANES, :], -jnp.inf)
    ge = jnp.exp(gl - jnp.max(gl, axis=0, keepdims=True))
    gprob = ge / jnp.sum(ge, axis=0, keepdims=True)
    g_p = jnp.max(gprob, axis=0, keepdims=True)
    g_idx = first_index_of(gprob, g_p)

    esel = jnp.zeros((SUBLANES, rows), F32)
    for k in range(n_groups):
        esel = jnp.where(g_idx == float(k), lg[(k + 1) * SUBLANES:(k + 2) * SUBLANES, :], esel)
    ee = jnp.exp(esel - jnp.max(esel, axis=0, keepdims=True))
    ep = ee / jnp.sum(ee, axis=0, keepdims=True)
    p1 = jnp.max(ep, axis=0, keepdims=True)
    i1 = first_index_of(ep, p1)
    ep_rest = jnp.where(row8 == i1, -1.0, ep)
    p2 = jnp.max(ep_rest, axis=0, keepdims=True)
    i2 = first_index_of(ep_rest, p2)

    den = p1 + p2
    wts_ref[0:1, :] = p1 / den * g_p
    wts_ref[1:2, :] = p2 / den * g_p
    eid_ref[0:1, :] = (g_idx * float(SUBLANES) + i1).astype(I32)
    eid_ref[1:2, :] = (g_idx * float(SUBLANES) + i2).astype(I32)


def _router(x1, g2, wt, bt, *, n_groups):
    m_rows, d = x1.shape
    return pl.pallas_call(
        functools.partial(_router_kernel, n_groups=n_groups),
        grid=(m_rows // TR,),
        in_specs=[
            pl.BlockSpec((TR, d), lambda i: (i, 0)),
            pl.BlockSpec((1, d), lambda i: (0, 0)),
            pl.BlockSpec(wt.shape, lambda i: (0, 0)),
            pl.BlockSpec(bt.shape, lambda i: (0, 0)),
        ],
        out_specs=[
            pl.BlockSpec((ROW_PAIR * TR, d // 2), lambda i: (i, 0)),
            pl.BlockSpec((TOP_K, TR), lambda i: (0, i)),
            pl.BlockSpec((TOP_K, TR), lambda i: (0, i)),
        ],
        out_shape=[
            jax.ShapeDtypeStruct((ROW_PAIR * m_rows, d // 2), BF16),
            jax.ShapeDtypeStruct((TOP_K, m_rows), I32),
            jax.ShapeDtypeStruct((TOP_K, m_rows), F32),
        ],
        compiler_params=_params(1),
        name="router",
    )(x1, g2, wt, bt)


INFO_TILE_EXPERT, INFO_NUM_TILES, INFO_VALID_END, INFO_END_TILE, INFO_ROWS = 0, 1, 2, 3, SUBLANES


def _positions_kernel(eid_ref, pos_ref, info_ref, exc_ref, *, n_experts, tile_shift):
    m_rows = eid_ref.shape[1]
    chunk = 2 * LANES
    n_chunks = m_rows // chunk
    tile = 1 << tile_shift
    e_col = lax.broadcasted_iota(I32, (n_experts, chunk), 0)
    upper = (lax.broadcasted_iota(I32, (chunk, chunk), 0)
             < lax.broadcasted_iota(I32, (chunk, chunk), 1)).astype(BF16)

    def onehots(c):
        off = pl.multiple_of(c * chunk, chunk)
        ids = eid_ref[:, pl.ds(off, chunk)]
        return off, [(e_col == ids[k:k + 1, :]) for k in range(TOP_K)]

    def count(c, carry):
        off, hot = onehots(c)
        used = jnp.where(hot[0] | hot[1], 1.0, 0.0)
        exc_ref[:, pl.ds(off, chunk)] = _dot(used.astype(BF16), upper) + carry
        return carry + jnp.sum(used, axis=1, keepdims=True)

    counts = lax.fori_loop(0, n_chunks, count, jnp.zeros((n_experts, 1), F32))

    lane = lax.broadcasted_iota(I32, (n_experts, LANES), 1)
    sub = lax.broadcasted_iota(I32, (n_experts, LANES), 0)

    def to_lanes(col):
        return jnp.sum(jnp.where(lane == sub, col, 0.0), axis=0, keepdims=True)

    n_tiles_e = ((counts.astype(I32) + (tile - 1)) >> tile_shift).astype(F32)
    first_tile = jnp.sum(jnp.where(lane < sub, to_lanes(n_tiles_e), 0.0), axis=1, keepdims=True)
    end_tile = first_tile + n_tiles_e
    first_row = first_tile * float(tile)

    def place(c, _):
        off, hot = onehots(c)
        dest = exc_ref[:, pl.ds(off, chunk)] + first_row
        for k in range(TOP_K):
            pos_ref[k:k + 1, pl.ds(off, chunk)] = jnp.sum(
                jnp.where(hot[k], dest, 0.0), axis=0, keepdims=True).astype(I32)
        return 0

    lax.fori_loop(0, n_chunks, place, 0)

    tile_expert = jnp.sum(jnp.where(end_tile <= lane.astype(F32), 1.0, 0.0), axis=0, keepdims=True)
    num_tiles = jnp.max(end_tile, axis=0, keepdims=True)
    valid_end = to_lanes(first_row + counts)
    info_ref[...] = jnp.zeros(info_ref.shape, I32)
    info_ref[INFO_TILE_EXPERT:INFO_TILE_EXPERT + 1, :] = jnp.minimum(
        tile_expert, float(n_experts - 1)).astype(I32)
    info_ref[INFO_NUM_TILES:INFO_NUM_TILES + 1, :] = jnp.broadcast_to(num_tiles, (1, LANES)).astype(I32)
    info_ref[INFO_VALID_END:INFO_VALID_END + 1, :] = valid_end.astype(I32)
    info_ref[INFO_END_TILE:INFO_END_TILE + 1, :] = to_lanes(end_tile).astype(I32)


def _positions(eid, *, n_experts):
    m_rows = eid.shape[1]
    tile_shift = T_EXP.bit_length() - 1
    return pl.pallas_call(
        functools.partial(_positions_kernel, n_experts=n_experts, tile_shift=tile_shift),
        out_shape=[
            jax.ShapeDtypeStruct((TOP_K, m_rows), I32),
            jax.ShapeDtypeStruct((INFO_ROWS, LANES), I32),
        ],
        scratch_shapes=[pltpu.VMEM((n_experts, m_rows), F32)],
        compiler_params=pltpu.CompilerParams(vmem_limit_bytes=VMEM_LIMIT_BYTES),
        name="positions",
    )(eid)


def _info(info_ref, row, lane):
    return info_ref[row * LANES + lane]


def _tile_state(i, info_ref):
    n_used = _info(info_ref, INFO_NUM_TILES, 0)
    ic = jnp.minimum(i, n_used - 1)
    e = _info(info_ref, INFO_TILE_EXPERT, ic)
    prev = _info(info_ref, INFO_TILE_EXPERT, jnp.maximum(ic - 1, 0))
    first = jnp.logical_or(ic == 0, e != prev)
    n_valid = jnp.minimum(_info(info_ref, INFO_VALID_END, e) - ic * T_EXP, T_EXP)
    return i < n_used, ic, e, first, n_valid


def _stream_expert_weights(i, info_ref, e, first, w_refs, wf_ref, wbf_ref, wsem, slot_ref):
    n_used = _info(info_ref, INFO_NUM_TILES, 0)
    rows = wf_ref.shape[2]
    chunk = min(rows, 512)

    def copies(expert, slot):
        return [pltpu.make_async_copy(w.at[expert], wf_ref.at[slot, j], wsem.at[slot])
                for j, w in enumerate(w_refs)]

    @pl.when(i == 0)
    def _():
        slot_ref[0] = 0
        for cp in copies(e, 0):
            cp.start(priority=1)

    @pl.when(jnp.logical_and(first, i > 0))
    def _():
        slot_ref[0] = 1 - slot_ref[0]

    @pl.when(first)
    def _():
        slot = slot_ref[0]
        for cp in copies(e, slot):
            cp.wait()
        for j in range(len(w_refs)):
            def cast(c, _, j=j):
                r0 = pl.multiple_of(c * chunk, chunk)
                wbf_ref[j, pl.ds(r0, chunk), :] = wf_ref[slot, j, pl.ds(r0, chunk), :].astype(BF16)
                return 0
            lax.fori_loop(0, rows // chunk, cast, 0)
        next_tile = _info(info_ref, INFO_END_TILE, e)

        @pl.when(next_tile < n_used)
        def _():
            for cp in copies(_info(info_ref, INFO_TILE_EXPERT, next_tile), 1 - slot):
                cp.start(priority=1)


def _expert_up_kernel(info_ref, pos_ref, xn_ref, wg_ref, wu_ref, h_ref,
                      xbuf_ref, wf_ref, wbf_ref, inv_ref, slot_ref, xsem, wsem, *, m_rows):
    i = pl.program_id(0)
    used, _, e, first, n_valid = _tile_state(i, info_ref)
    n_used = _info(info_ref, INFO_NUM_TILES, 0)

    def gather(tile, slot, n_rows, start):
        n_groups = (n_rows + (SUBLANES - 1)) >> (SUBLANES.bit_length() - 1)
        if start:
            first_row = tile * T_EXP
            last_row = first_row + n_rows - 1

            def body(g, _):
                for j in range(SUBLANES):
                    r = g * SUBLANES + j
                    src = inv_ref[jnp.minimum(first_row + r, last_row)]
                    pltpu.make_async_copy(_pair_rows(xn_ref, src), _pair_rows(xbuf_ref.at[slot], r),
                                          xsem.at[slot]).start()
                return 0
            lax.fori_loop(0, n_groups, body, 0)
        else:
            for bit in range((T_EXP // SUBLANES).bit_length()):
                n = SUBLANES << bit

                @pl.when((n_groups & (1 << bit)) != 0)
                def _():
                    pltpu.make_async_copy(_pair_rows(xn_ref, 0, n), _pair_rows(xbuf_ref.at[slot], 0, n),
                                          xsem.at[slot]).wait()

    @pl.when(i == 0)
    def _():
        def invert(t, _):
            for k in range(TOP_K):
                inv_ref[pos_ref[k * m_rows + t]] = t
            return 0
        lax.fori_loop(0, m_rows, invert, 0, unroll=8)
        xbuf_ref[...] = jnp.zeros(xbuf_ref.shape, BF16)
        gather(0, 0, n_valid, True)

    @pl.when(used)
    def _():
        _stream_expert_weights(i, info_ref, e, first, (wg_ref, wu_ref), wf_ref, wbf_ref, wsem, slot_ref)
        slot = i & 1
        gather(i, slot, n_valid, False)

        @pl.when(i + 1 < n_used)
        def _():
            gather(i + 1, 1 - slot, _tile_state(i + 1, info_ref)[4], True)

        x_lo, x_hi = [x.astype(BF16) for x in _unpack_halves(xbuf_ref[slot])]
        half = x_lo.shape[1]

        def proj(j):
            return _dot(x_lo, wbf_ref[j, 0:half, :]) + _dot(x_hi, wbf_ref[j, half:, :])

        h = jax.nn.silu(proj(0)) * proj(1)
        h_ref[...] = h.astype(BF16)

    @pl.when(jnp.logical_not(used))
    def _():
        h_ref[...] = jnp.zeros(h_ref.shape, BF16)


def _expert_up(info_flat, pos_flat, xn, w_gate, w_up, *, n_tiles):
    m_rows, half = xn.shape[0] // ROW_PAIR, xn.shape[1]
    _, d, d_exp = w_gate.shape
    assert d == 2 * half
    p_rows = n_tiles * T_EXP
    any_spec = pl.BlockSpec(memory_space=pl.ANY)
    return pl.pallas_call(
        functools.partial(_expert_up_kernel, m_rows=m_rows),
        grid_spec=pltpu.PrefetchScalarGridSpec(
            num_scalar_prefetch=2,
            grid=(n_tiles,),
            in_specs=[any_spec, any_spec, any_spec],
            out_specs=pl.BlockSpec((T_EXP, d_exp), lambda i, info, pos: (i, 0)),
            scratch_shapes=[
                pltpu.VMEM((2, ROW_PAIR * T_EXP, half), BF16),
                pltpu.VMEM((2, 2, d, d_exp), F32),
                pltpu.VMEM((2, d, d_exp), BF16),
                pltpu.SMEM((p_rows,), I32),
                pltpu.SMEM((1,), I32),
                pltpu.SemaphoreType.DMA((2,)),
                pltpu.SemaphoreType.DMA((2,)),
            ],
        ),
        out_shape=jax.ShapeDtypeStruct((p_rows, d_exp), BF16),
        compiler_params=_params(1),
        name="expert_up",
    )(info_flat, pos_flat, xn, w_gate, w_up)


def _expert_down_kernel(info_ref, h_ref, wd_ref, y_ref, wf_ref, wbf_ref, slot_ref, wsem):
    i = pl.program_id(0)
    used, _, e, first, _ = _tile_state(i, info_ref)

    @pl.when(used)
    def _():
        _stream_expert_weights(i, info_ref, e, first, (wd_ref,), wf_ref, wbf_ref, wsem, slot_ref)
        y_ref[...] = _pack_halves(_dot(h_ref[...], wbf_ref[0]))

    @pl.when(jnp.logical_not(used))
    def _():
        y_ref[...] = jnp.zeros(y_ref.shape, BF16)


def _expert_down(info_flat, h, w_down, *, n_tiles):
    p_rows, d_exp = h.shape
    d = w_down.shape[-1]
    return pl.pallas_call(
        _expert_down_kernel,
        grid_spec=pltpu.PrefetchScalarGridSpec(
            num_scalar_prefetch=1,
            grid=(n_tiles,),
            in_specs=[
                pl.BlockSpec((T_EXP, d_exp), lambda i, info: (_tile_state(i, info)[1], 0)),
                pl.BlockSpec(memory_space=pl.ANY),
            ],
            out_specs=pl.BlockSpec((ROW_PAIR * T_EXP, d // 2), lambda i, info: (i, 0)),
            scratch_shapes=[
                pltpu.VMEM((2, 1, d_exp, d), F32),
                pltpu.VMEM((1, d_exp, d), BF16),
                pltpu.SMEM((1,), I32),
                pltpu.SemaphoreType.DMA((2,)),
            ],
        ),
        out_shape=jax.ShapeDtypeStruct((ROW_PAIR * p_rows, d // 2), BF16),
        compiler_params=_params(1),
        name="expert_down",
    )(info_flat, h, w_down)


def _combine_kernel(pos_ref, x1_ref, w_ref, g_ref, y_ref, out_ref, ybuf_ref, sem, *, m_rows, row0, n_steps):
    i = pl.program_id(0)

    def gather_start(step):
        base = row0 + step * TR
        slot = step & 1

        def body(r, _):
            for k in range(TOP_K):
                src = pos_ref[k * m_rows + base + r]
                pltpu.make_async_copy(_pair_rows(y_ref, src), _pair_rows(ybuf_ref.at[slot, k], r),
                                      sem.at[slot]).start(priority=k % 2)
            return 0
        lax.fori_loop(0, TR, body, 0, unroll=8)

    def gather_wait(slot):
        for k in range(TOP_K):
            pltpu.make_async_copy(_pair_rows(y_ref, 0, TR), ybuf_ref.at[slot, k], sem.at[slot]).wait()

    @pl.when(i == 0)
    def _():
        gather_start(0)

    @pl.when(i + 1 < n_steps)
    def _():
        gather_start(i + 1)

    slot = i & 1
    gather_wait(slot)
    w = w_ref[...]
    y0, y1 = _unpack_halves(ybuf_ref[slot, 0]), _unpack_halves(ybuf_ref[slot, 1])
    moe = jnp.concatenate([w[:, 0:1] * y0[k] + w[:, 1:2] * y1[k] for k in range(2)], axis=-1)
    out_ref[...] = _rmsnorm_rows(x1_ref[...] + moe, g_ref[...])


def _combine(pos_flat, x1, wts_t, norm_f, y, *, row0, n_rows):
    m_rows, d = x1.shape
    tile0 = row0 // TR
    n_steps = n_rows // TR
    return pl.pallas_call(
        functools.partial(_combine_kernel, m_rows=m_rows, row0=row0, n_steps=n_steps),
        grid_spec=pltpu.PrefetchScalarGridSpec(
            num_scalar_prefetch=1,
            grid=(n_steps,),
            in_specs=[
                pl.BlockSpec((TR, d), lambda i, pos: (tile0 + i, 0)),
                pl.BlockSpec((TR, TOP_K), lambda i, pos: (tile0 + i, 0)),
                pl.BlockSpec((1, d), lambda i, pos: (0, 0)),
                pl.BlockSpec(memory_space=pl.ANY),
            ],
            out_specs=pl.BlockSpec((TR, d), lambda i, pos: (i, 0)),
            scratch_shapes=[pltpu.VMEM((2, TOP_K, ROW_PAIR * TR, d // 2), BF16), pltpu.SemaphoreType.DMA((2,))],
        ),
        out_shape=jax.ShapeDtypeStruct((n_rows, d), F32),
        compiler_params=_params(1),
        name="combine",
    )(pos_flat, x1, wts_t, norm_f, y)


def _sample_headers(state, k_width):
    n, _, w = state.shape
    padded = jnp.pad(state, ((0, 0), (SUBLANES - (k_width - 1), 0), (0, 0)))
    return padded.reshape(n * SUBLANES, w)


def kernel(x_prompt, x_sample, state_conv_a, state_conv_r, state_h, meta_tokens, norm1, w_in, conv_a_w, conv_r_w,
           conv_r_b, lru_wa, lru_ba, lru_wx, lru_bx, lru_lam, w_br_a, w_br_r, w_o, norm2, w_group, b_group,
           w_router, b_router, w_gate, w_up, w_down, norm_f):
    batch, seq, d = x_prompt.shape
    dec_batch, dec_seq, _ = x_sample.shape
    depth = norm1.shape[0]
    width = w_br_a.shape[1]
    n_groups = w_group.shape[-1]
    n_experts = w_router.shape[-1]
    n_meta = meta_tokens.shape[0]
    assert depth == 1, "meta rows are folded into an initial state, which only carries one layer"
    assert dec_seq == SUBLANES and n_meta % SUBLANES == 0 and n_meta >= SUBLANES
    assert seq % TM == 0 and (dec_batch * dec_seq) % TM == 0
    assert n_groups <= SUBLANES and n_experts == n_groups * SUBLANES
    assert width % CB == 0 and d % (2 * CB) == 0 and CB % lru_wa.shape[-1] == 0

    n_prompt_rows = batch * seq
    n_sample_rows = dec_batch * dec_seq
    m_rows = n_prompt_rows + n_sample_rows
    n_prompt_tiles = n_prompt_rows // TM
    n_tiles = m_rows // TM
    tiles_per_seq = seq // TM
    n_exp_tiles = (TOP_K * m_rows) // T_EXP + n_experts
    assert n_exp_tiles <= LANES

    xp = x_prompt.reshape(n_prompt_rows, d)
    xs = x_sample.reshape(n_sample_rows, d)
    row = lambda v: v.reshape(1, -1)

    u, um = _norm1(xp, xs, meta_tokens, row(norm1[0]), n_prompt_tiles, n_tiles)

    def mixer_geometry(tm):
        assert seq % tm == 0 and n_sample_rows % tm == 0 and tm % MIX_CHUNK == 0 and MIX_CHUNK % SUBLANES == 0
        return dict(tm=tm, width=width, n_seq=batch, tiles_per_seq=seq // tm,
                    n_prompt_tiles=n_prompt_rows // tm, n_tiles=m_rows // tm)

    out_a, tail_a_p, tail_a_s = _mixer_a(
        u, um, w_in[0], conv_a_w[0], _sample_headers(state_conv_a[0], K_A), **mixer_geometry(TM_MIX_A))
    out_r, tail_r_p, tail_r_s, h_p, h_s = _mixer_r(
        u, um, w_in[0], conv_r_w[0], row(conv_r_b[0]), lru_wa[0], row(lru_ba[0]), lru_wx[0], row(lru_bx[0]),
        row(lru_lam[0]), _sample_headers(state_conv_r[0], K_R), state_h[0], seg0=3, **mixer_geometry(TM_MIX_R))

    gates = _gates(u, w_in[0], col0=5 * width, n_cols=2 * d, tm=TM_OPROJ)
    merged = _merge(gates, out_a, out_r, w_br_a[0], w_br_r[0], tm=TM_OPROJ)
    x1 = _oproj(merged, xp, xs, w_o[0], tm=TM_OPROJ)

    wt = jnp.concatenate([jnp.pad(w_group[0].T, ((0, SUBLANES - n_groups), (0, 0))), w_router[0].T], axis=0)
    bt = jnp.concatenate([jnp.pad(b_group[0], (0, SUBLANES - n_groups)), b_router[0]]).reshape(-1, 1)
    xn, eid, wts = _router(x1, row(norm2[0]), wt, bt, n_groups=n_groups)

    pos, info = _positions(eid, n_experts=n_experts)
    pos_flat = pos.reshape(-1)
    info_flat = info.reshape(-1)
    hid = _expert_up(info_flat, pos_flat, xn, w_gate[0], w_up[0], n_tiles=n_exp_tiles)
    y = _expert_down(info_flat, hid, w_down[0], n_tiles=n_exp_tiles)

    wts_t = wts.T
    g_f = row(norm_f)
    y_prompt = _combine(pos_flat, x1, wts_t, g_f, y, row0=0, n_rows=n_prompt_rows)
    y_sample = _combine(pos_flat, x1, wts_t, g_f, y, row0=n_prompt_rows, n_rows=n_sample_rows)

    def sample_tail(t, k):
        return t.reshape(dec_batch, SUBLANES, width)[:, SUBLANES - k:, :][None]

    return (
        y_prompt.reshape(batch, seq, d),
        y_sample.reshape(dec_batch, dec_seq, d),
        tail_a_p[:, SUBLANES - (K_A - 1):, :][None],
        tail_r_p[:, SUBLANES - (K_R - 1):, :][None],
        h_p[:, SUBLANES - 1, :][None],
        sample_tail(tail_a_s, K_A - 1),
        sample_tail(tail_r_s, K_R - 1),
        sample_tail(h_s, 1)[:, :, 0, :],
    )
```

```python
import functools

import jax
import jax.numpy as jnp
from jax import lax
from jax.experimental import pallas as pl
from jax.experimental.pallas import tpu as pltpu

F32, BF16, I32 = jnp.float32, jnp.bfloat16, jnp.int32

EPS = 1e-6
C_RG = 8.0
K_A = 3
K_R = 4
LRU_HEADS = 16
TOP_K = 2

SUBLANES = 8
LANES = 128
VMEM_LIMIT_BYTES = 56 * 1024 * 1024

TM = 512
TM_OPROJ = 1024
TM_MIX_A = 512
TM_MIX_R = 1024
MIX_CHUNK = 256
CB = 256
T_EXP = 256
TR = 256


def _dot(a, b):
    return jnp.dot(a, b, preferred_element_type=F32)


ROW_PAIR = 2


def _pack_halves(x):
    half = x.shape[1] // 2
    words = pltpu.pack_elementwise([x[:, :half], x[:, half:]], packed_dtype=BF16)
    return pltpu.bitcast(words, BF16)


def _unpack_halves(pairs):
    words = pltpu.bitcast(pairs, jnp.uint32)
    return [pltpu.unpack_elementwise(words, index=k, packed_dtype=BF16, unpacked_dtype=F32) for k in range(2)]


GROUP_ROWS = ROW_PAIR * SUBLANES


def _grouped(x):
    return x.reshape(x.shape[0] // GROUP_ROWS, GROUP_ROWS, x.shape[1])


def _token_rows(ref, token):
    shift = SUBLANES.bit_length() - 1
    row = pl.multiple_of(ROW_PAIR * (token & (SUBLANES - 1)), ROW_PAIR)
    return ref.at[token >> shift, pl.ds(row, ROW_PAIR), :]


def _slot_rows(ref, group, j):
    return ref.at[group, pl.ds(ROW_PAIR * j, ROW_PAIR), :]


def _rmsnorm_rows(x, g):
    y = x * lax.rsqrt(jnp.mean(x * x, axis=-1, keepdims=True) + EPS)
    return y * g


def _params(n_axes):
    return pltpu.CompilerParams(dimension_semantics=("arbitrary",) * n_axes,
                                vmem_limit_bytes=VMEM_LIMIT_BYTES)


def _norm1_kernel(xp_ref, xs_ref, meta_ref, g_ref, u_ref, um_ref, *, n_prompt_tiles):
    i = pl.program_id(0)
    g = g_ref[...]

    @pl.when(i < n_prompt_tiles)
    def _():
        u_ref[...] = _rmsnorm_rows(xp_ref[...], g).astype(BF16)

    @pl.when(i >= n_prompt_tiles)
    def _():
        u_ref[...] = _rmsnorm_rows(xs_ref[...], g).astype(BF16)

    @pl.when(i == 0)
    def _():
        um_ref[...] = _rmsnorm_rows(meta_ref[...], g).astype(BF16)


def _norm1(xp, xs, meta, g, n_prompt_tiles, n_tiles):
    d = xp.shape[1]
    return pl.pallas_call(
        functools.partial(_norm1_kernel, n_prompt_tiles=n_prompt_tiles),
        grid=(n_tiles,),
        in_specs=[
            pl.BlockSpec((TM, d), lambda i: (jnp.minimum(i, n_prompt_tiles - 1), 0)),
            pl.BlockSpec((TM, d), lambda i: (jnp.maximum(i - n_prompt_tiles, 0), 0)),
            pl.BlockSpec(meta.shape, lambda i: (0, 0)),
            pl.BlockSpec((1, d), lambda i: (0, 0)),
        ],
        out_specs=[
            pl.BlockSpec((TM, d), lambda i: (i, 0)),
            pl.BlockSpec(meta.shape, lambda i: (0, 0)),
        ],
        out_shape=[
            jax.ShapeDtypeStruct((n_tiles * TM, d), BF16),
            jax.ShapeDtypeStruct(meta.shape, BF16),
        ],
        compiler_params=_params(1),
        name="norm1",
    )(xp, xs, meta, g)


def _conv_taps(window, cw, k_width):
    acc = window(0) * cw[0:1, :]
    for k in range(1, k_width):
        acc = acc + window(k) * cw[k:k + 1, :]
    return acc


def _conv_flat(ext_ref, values, cw, k_width, r0):
    rows = values.shape[0]
    ext_ref[SUBLANES + r0:SUBLANES + r0 + rows, :] = values
    base = SUBLANES - (k_width - 1) + r0
    return _conv_taps(lambda k: ext_ref[pl.ds(base + k, rows), :], cw, k_width)


def _conv_grouped(ext3_ref, values, header, cw, k_width, s0):
    rows, cols = values.shape
    n_seq = rows // SUBLANES
    ext3_ref[s0:s0 + n_seq, 0:SUBLANES, :] = header.reshape(n_seq, SUBLANES, cols)
    ext3_ref[s0:s0 + n_seq, SUBLANES:, :] = values.reshape(n_seq, SUBLANES, cols)
    base = SUBLANES - (k_width - 1)
    return _conv_taps(
        lambda k: ext3_ref[s0:s0 + n_seq, base + k:base + k + SUBLANES, :].reshape(rows, cols), cw, k_width)


def _mixer_a_kernel(u_ref, um_ref, wb_ref, wc_ref, wv_ref, cw_ref, hdr_ref,
                    oa_ref, tailp_ref, tails_ref,
                    wbf_ref, ext_ref, ext3_ref, carry_ref, mhdr_ref,
                    *, tm, tiles_per_seq, n_prompt_tiles, n_meta):
    m = pl.program_id(1)
    cw = cw_ref[...]
    ch = MIX_CHUNK
    n_chunks = tm // ch

    @pl.when(m == 0)
    def _():
        wbf_ref[0] = wb_ref[...].astype(BF16)
        wbf_ref[1] = wc_ref[...].astype(BF16)
        wbf_ref[2] = wv_ref[...].astype(BF16)
        um = um_ref[...]
        cv_meta = _dot(um, wbf_ref[1]) * _dot(um, wbf_ref[2])
        mhdr_ref[...] = cv_meta[n_meta - SUBLANES:, :]

    def products(r0):
        u = u_ref[r0:r0 + ch, :]
        return _dot(u, wbf_ref[0]), _dot(u, wbf_ref[1]) * _dot(u, wbf_ref[2])

    @pl.when(m < n_prompt_tiles)
    def _():
        @pl.when((m % tiles_per_seq) == 0)
        def _():
            carry_ref[...] = mhdr_ref[...]

        ext_ref[0:SUBLANES, :] = carry_ref[...]
        for c in range(n_chunks):
            r0 = c * ch
            zb, cv = products(r0)
            conv = _conv_flat(ext_ref, cv, cw, K_A, r0)
            oa_ref[r0:r0 + ch, :] = (zb * conv).astype(BF16)
        carry_ref[...] = ext_ref[tm:, :]
        tailp_ref[0] = ext_ref[tm:, :]

    @pl.when(m >= n_prompt_tiles)
    def _():
        for c in range(n_chunks):
            r0 = c * ch
            zb, cv = products(r0)
            conv = _conv_grouped(ext3_ref, cv, hdr_ref[r0:r0 + ch, :], cw, K_A, r0 // SUBLANES)
            oa_ref[r0:r0 + ch, :] = (zb * conv).astype(BF16)
            tails_ref[r0:r0 + ch, :] = cv


def _mixer_a(u, um, w_in, conv_w, hdr_s, *, tm, width, n_seq, tiles_per_seq, n_prompt_tiles, n_tiles):
    d = u.shape[1]
    nb = width // CB
    n_meta = um.shape[0]
    last_seq = n_seq - 1

    def wspec(seg):
        return pl.BlockSpec((d, CB), lambda j, m, seg=seg: (0, seg * nb + j))

    return pl.pallas_call(
        functools.partial(_mixer_a_kernel, tm=tm, tiles_per_seq=tiles_per_seq,
                          n_prompt_tiles=n_prompt_tiles, n_meta=n_meta),
        grid=(nb, n_tiles),
        in_specs=[
            pl.BlockSpec((tm, d), lambda j, m: (m, 0)),
            pl.BlockSpec(um.shape, lambda j, m: (0, 0)),
            wspec(0), wspec(1), wspec(2),
            pl.BlockSpec((K_A, CB), lambda j, m: (0, j)),
            pl.BlockSpec((tm, CB), lambda j, m: (jnp.maximum(m - n_prompt_tiles, 0), j)),
        ],
        out_specs=[
            pl.BlockSpec((tm, CB), lambda j, m: (m, j)),
            pl.BlockSpec((1, SUBLANES, CB),
                         lambda j, m: (jnp.minimum(m // tiles_per_seq, last_seq), 0, j)),
            pl.BlockSpec((tm, CB), lambda j, m: (jnp.maximum(m - n_prompt_tiles, 0), j)),
        ],
        out_shape=[
            jax.ShapeDtypeStruct((n_tiles * tm, width), BF16),
            jax.ShapeDtypeStruct((n_seq, SUBLANES, width), F32),
            jax.ShapeDtypeStruct(hdr_s.shape, F32),
        ],
        scratch_shapes=[
            pltpu.VMEM((3, d, CB), BF16),
            pltpu.VMEM((tm + SUBLANES, CB), F32),
            pltpu.VMEM((tm // SUBLANES, 2 * SUBLANES, CB), F32),
            pltpu.VMEM((SUBLANES, CB), F32),
            pltpu.VMEM((SUBLANES, CB), F32),
        ],
        compiler_params=_params(2),
        name="mixer_a",
    )(u, um, w_in, w_in, w_in, conv_w, hdr_s)


def _softplus(x):
    return jnp.maximum(x, 0.0) + jnp.log1p(jnp.exp(-jnp.abs(x)))


def _lru_kernel(u_ref, um_ref, wx_ref, wy_ref, cw_ref, cbias_ref, wa_ref, ba_ref, wi_ref, bi_ref, lam_ref,
                hdr_ref, h0_ref,
                or_ref, tailp_ref, tails_ref, hp_ref, hs_ref,
                wbf_ref, gbf_ref, ext_ref, ext3_ref, carry_ref, hcarry_ref,
                mhdr_ref, mh_ref,
                *, tm, tiles_per_seq, n_prompt_tiles, n_meta, lru_block):
    m = pl.program_id(1)
    cw = cw_ref[...]
    cbias = cbias_ref[...]
    heads = CB // lru_block
    ch = MIX_CHUNK
    n_chunks = tm // ch
    row8 = lax.broadcasted_iota(I32, (SUBLANES, CB), 0)

    def gate_terms(xc):
        xb = xc.astype(BF16)
        ra, ri = [], []
        for hh in range(heads):
            xh = xb[:, hh * lru_block:(hh + 1) * lru_block]
            ra.append(_dot(xh, gbf_ref[0, hh]))
            ri.append(_dot(xh, gbf_ref[1, hh]))
        r = jax.nn.sigmoid(jnp.concatenate(ra, axis=-1) + ba_ref[...])
        i = jax.nn.sigmoid(jnp.concatenate(ri, axis=-1) + bi_ref[...])
        log_a = -C_RG * r * _softplus(-lam_ref[...])
        a = jnp.exp(log_a)
        b = jnp.sqrt((1.0 + a * a) * jnp.tanh(-log_a)) * i * xc
        return a, b

    def scan(a, b, h_prev, h0_row=None):
        out = []
        for g in range(a.shape[0] // SUBLANES):
            a8 = a[g * SUBLANES:(g + 1) * SUBLANES, :]
            b8 = b[g * SUBLANES:(g + 1) * SUBLANES, :]
            for dist in (1, 2, 4):
                keep = row8 >= dist
                b8 = jnp.where(keep, a8 * pltpu.roll(b8, dist, 0) + b8, b8)
                a8 = jnp.where(keep, a8 * pltpu.roll(a8, dist, 0), a8)
            h8 = a8 * (h_prev if h0_row is None else h0_row(g)) + b8
            out.append(h8)
            h_prev = h8[SUBLANES - 1:SUBLANES, :]
        return jnp.concatenate(out, axis=0), h_prev

    @pl.when(m == 0)
    def _():
        wbf_ref[0] = wx_ref[...].astype(BF16)
        wbf_ref[1] = wy_ref[...].astype(BF16)
        gbf_ref[0] = wa_ref[...].astype(BF16)
        gbf_ref[1] = wi_ref[...].astype(BF16)
        zx_meta = _dot(um_ref[...], wbf_ref[0])
        ext_ref[0:SUBLANES, :] = jnp.zeros((SUBLANES, CB), F32)
        a, b = gate_terms(_conv_flat(ext_ref, zx_meta, cw, K_R, 0) + cbias)
        _, h_meta = scan(a, b, jnp.zeros((1, CB), F32))
        mh_ref[...] = jnp.broadcast_to(h_meta, (SUBLANES, CB))
        mhdr_ref[...] = zx_meta[n_meta - SUBLANES:, :]

    def products(r0):
        u = u_ref[r0:r0 + ch, :]
        return _dot(u, wbf_ref[0]), _dot(u, wbf_ref[1])

    @pl.when(m < n_prompt_tiles)
    def _():
        @pl.when((m % tiles_per_seq) == 0)
        def _():
            carry_ref[...] = mhdr_ref[...]
            hcarry_ref[...] = mh_ref[...]

        ext_ref[0:SUBLANES, :] = carry_ref[...]
        h_prev = hcarry_ref[0:1, :]
        for c in range(n_chunks):
            r0 = c * ch
            zx, zy = products(r0)
            a, b = gate_terms(_conv_flat(ext_ref, zx, cw, K_R, r0) + cbias)
            h, h_prev = scan(a, b, h_prev)
            or_ref[r0:r0 + ch, :] = (h * jax.nn.gelu(zy)).astype(BF16)
        hcarry_ref[0:1, :] = h_prev
        carry_ref[...] = ext_ref[tm:, :]
        tailp_ref[0] = ext_ref[tm:, :]
        hp_ref[0] = h[ch - SUBLANES:, :]

    @pl.when(m >= n_prompt_tiles)
    def _():
        for c in range(n_chunks):
            r0 = c * ch
            s0 = r0 // SUBLANES
            zx, zy = products(r0)
            xc = _conv_grouped(ext3_ref, zx, hdr_ref[r0:r0 + ch, :], cw, K_R, s0) + cbias
            a, b = gate_terms(xc)
            h, _ = scan(a, b, None, lambda g, s0=s0: h0_ref[s0 + g:s0 + g + 1, :])
            or_ref[r0:r0 + ch, :] = (h * jax.nn.gelu(zy)).astype(BF16)
            tails_ref[r0:r0 + ch, :] = zx
            hs_ref[r0:r0 + ch, :] = h


def _mixer_r(u, um, w_in, conv_w, conv_b, lru_wa, lru_ba, lru_wx, lru_bx, lru_lam, hdr_s, h0_s,
             *, tm, width, seg0, n_seq, tiles_per_seq, n_prompt_tiles, n_tiles):
    d = u.shape[1]
    nb = width // CB
    n_meta = um.shape[0]
    last_seq = n_seq - 1
    lru_block = lru_wa.shape[-1]
    heads = CB // lru_block
    seqs_per_tile = tm // SUBLANES

    def wspec(seg):
        return pl.BlockSpec((d, CB), lambda j, m, seg=seg: (0, seg * nb + j))

    def vec():
        return pl.BlockSpec((1, CB), lambda j, m: (0, j))

    def gspec():
        return pl.BlockSpec((heads, lru_block, lru_block), lambda j, m: (j, 0, 0))

    def sample_rows():
        return pl.BlockSpec((tm, CB), lambda j, m: (jnp.maximum(m - n_prompt_tiles, 0), j))

    def seq_tail():
        return pl.BlockSpec((1, SUBLANES, CB),
                            lambda j, m: (jnp.minimum(m // tiles_per_seq, last_seq), 0, j))

    return pl.pallas_call(
        functools.partial(_lru_kernel, tm=tm, tiles_per_seq=tiles_per_seq, n_prompt_tiles=n_prompt_tiles,
                          n_meta=n_meta, lru_block=lru_block),
        grid=(nb, n_tiles),
        in_specs=[
            pl.BlockSpec((tm, d), lambda j, m: (m, 0)),
            pl.BlockSpec(um.shape, lambda j, m: (0, 0)),
            wspec(seg0), wspec(seg0 + 1),
            pl.BlockSpec((K_R, CB), lambda j, m: (0, j)),
            vec(),
            gspec(), vec(), gspec(), vec(), vec(),
            sample_rows(),
            pl.BlockSpec((seqs_per_tile, CB), lambda j, m: (jnp.maximum(m - n_prompt_tiles, 0), j)),
        ],
        out_specs=[
            pl.BlockSpec((tm, CB), lambda j, m: (m, j)),
            seq_tail(), sample_rows(), seq_tail(), sample_rows(),
        ],
        out_shape=[
            jax.ShapeDtypeStruct((n_tiles * tm, width), BF16),
            jax.ShapeDtypeStruct((n_seq, SUBLANES, width), F32),
            jax.ShapeDtypeStruct(hdr_s.shape, F32),
            jax.ShapeDtypeStruct((n_seq, SUBLANES, width), F32),
            jax.ShapeDtypeStruct(hdr_s.shape, F32),
        ],
        scratch_shapes=[
            pltpu.VMEM((2, d, CB), BF16),
            pltpu.VMEM((2, heads, lru_block, lru_block), BF16),
            pltpu.VMEM((tm + SUBLANES, CB), F32),
            pltpu.VMEM((seqs_per_tile, 2 * SUBLANES, CB), F32),
            pltpu.VMEM((SUBLANES, CB), F32),
            pltpu.VMEM((SUBLANES, CB), F32),
            pltpu.VMEM((SUBLANES, CB), F32),
            pltpu.VMEM((SUBLANES, CB), F32),
        ],
        compiler_params=_params(2),
        name="mixer_r",
    )(u, um, w_in, w_in, conv_w, conv_b, lru_wa, lru_ba, lru_wx, lru_bx, lru_lam, hdr_s, h0_s)


def _gates_kernel(u_ref, w_ref, g_ref, wbf_ref):
    @pl.when(pl.program_id(1) == 0)
    def _():
        wbf_ref[...] = w_ref[...].astype(BF16)

    g_ref[...] = jax.nn.sigmoid(_dot(u_ref[...], wbf_ref[...]))


def _gates(u, w_in, *, col0, n_cols, tm):
    m_rows, d = u.shape
    cb = 2 * CB
    return pl.pallas_call(
        _gates_kernel,
        grid=(n_cols // cb, m_rows // tm),
        in_specs=[
            pl.BlockSpec((tm, d), lambda j, m: (m, 0)),
            pl.BlockSpec((d, cb), lambda j, m: (0, col0 // cb + j)),
        ],
        out_specs=pl.BlockSpec((tm, cb), lambda j, m: (m, j)),
        out_shape=jax.ShapeDtypeStruct((m_rows, n_cols), F32),
        scratch_shapes=[pltpu.VMEM((d, cb), BF16)],
        compiler_params=_params(2),
        name="gates",
    )(u, w_in)


def _merge_kernel(ga_ref, gr_ref, oa_ref, or_ref, wba_ref, wbr_ref, mg_ref, wb_bf):
    @pl.when(pl.program_id(1) == 0)
    def _():
        wb_bf[0] = wba_ref[...].astype(BF16)
        wb_bf[1] = wbr_ref[...].astype(BF16)

    mg = ga_ref[...] * _dot(oa_ref[...], wb_bf[0]) + gr_ref[...] * _dot(or_ref[...], wb_bf[1])
    mg_ref[...] = mg.astype(BF16)


def _merge(gates, out_a, out_r, w_br_a, w_br_r, *, tm):
    m_rows, width = out_a.shape
    d = w_br_a.shape[1]
    cb = 2 * CB
    nb = d // cb
    return pl.pallas_call(
        _merge_kernel,
        grid=(nb, m_rows // tm),
        in_specs=[
            pl.BlockSpec((tm, cb), lambda j, m: (m, j)),
            pl.BlockSpec((tm, cb), lambda j, m: (m, nb + j)),
            pl.BlockSpec((tm, width), lambda j, m: (m, 0)),
            pl.BlockSpec((tm, width), lambda j, m: (m, 0)),
            pl.BlockSpec((width, cb), lambda j, m: (0, j)),
            pl.BlockSpec((width, cb), lambda j, m: (0, j)),
        ],
        out_specs=pl.BlockSpec((tm, cb), lambda j, m: (m, j)),
        out_shape=jax.ShapeDtypeStruct((m_rows, d), BF16),
        scratch_shapes=[pltpu.VMEM((2, width, cb), BF16)],
        compiler_params=_params(2),
        name="merge",
    )(gates, gates, out_a, out_r, w_br_a, w_br_r)


def _oproj_kernel(mg_ref, xp_ref, xs_ref, wo_ref, x1_ref, wbf_ref, *, n_prompt_tiles):
    m = pl.program_id(1)

    @pl.when(m == 0)
    def _():
        wbf_ref[...] = wo_ref[...].astype(BF16)

    y = _dot(mg_ref[...], wbf_ref[...])

    @pl.when(m < n_prompt_tiles)
    def _():
        x1_ref[...] = xp_ref[...] + y

    @pl.when(m >= n_prompt_tiles)
    def _():
        x1_ref[...] = xs_ref[...] + y


def _oproj(mg, xp, xs, w_o, *, tm):
    m_rows, d = mg.shape
    cb = 2 * CB
    nb = d // cb
    n_prompt_tiles = xp.shape[0] // tm
    n_tiles = m_rows // tm
    assert xp.shape[0] % tm == 0 and xs.shape[0] % tm == 0
    return pl.pallas_call(
        functools.partial(_oproj_kernel, n_prompt_tiles=n_prompt_tiles),
        grid=(nb, n_tiles),
        in_specs=[
            pl.BlockSpec((tm, d), lambda j, m: (m, 0)),
            pl.BlockSpec((tm, cb), lambda j, m: (jnp.minimum(m, n_prompt_tiles - 1), j)),
            pl.BlockSpec((tm, cb), lambda j, m: (jnp.maximum(m - n_prompt_tiles, 0), j)),
            pl.BlockSpec((d, cb), lambda j, m: (0, j)),
        ],
        out_specs=pl.BlockSpec((tm, cb), lambda j, m: (m, j)),
        out_shape=jax.ShapeDtypeStruct((m_rows, d), F32),
        scratch_shapes=[pltpu.VMEM((d, cb), BF16)],
        compiler_params=_params(2),
        name="oproj",
    )(mg, xp, xs, w_o)


def _router_kernel(x1_ref, g_ref, wt_ref, bt_ref, xn_ref, eid_ref, wts_ref, *, n_groups):
    xn = _rmsnorm_rows(x1_ref[...], g_ref[...])
    xn_ref[...] = _pack_halves(xn)
    lg = lax.dot_general(wt_ref[...].astype(BF16), xn.astype(BF16), (((1,), (1,)), ((), ())),
                         preferred_element_type=F32) + bt_ref[...]
    rows = xn.shape[0]
    row8 = lax.broadcasted_iota(I32, (SUBLANES, rows), 0).astype(F32)

    def first_index_of(v, vmax):
        return jnp.min(jnp.where(v == vmax, row8, float(SUBLANES)), axis=0, keepdims=True)

    gl = jnp.where(row8 < float(n_groups), lg[0:SUBLANES, :], -jnp.inf)
    ge = jnp.exp(gl - jnp.max(gl, axis=0, keepdims=True))
    gprob = ge / jnp.sum(ge, axis=0, keepdims=True)
    g_p = jnp.max(gprob, axis=0, keepdims=True)
    g_idx = first_index_of(gprob, g_p)

    esel = jnp.zeros((SUBLANES, rows), F32)
    for k in range(n_groups):
        esel = jnp.where(g_idx == float(k), lg[(k + 1) * SUBLANES:(k + 2) * SUBLANES, :], esel)
    ee = jnp.exp(esel - jnp.max(esel, axis=0, keepdims=True))
    ep = ee / jnp.sum(ee, axis=0, keepdims=True)
    p1 = jnp.max(ep, axis=0, keepdims=True)
    i1 = first_index_of(ep, p1)
    ep_rest = jnp.where(row8 == i1, -1.0, ep)
    p2 = jnp.max(ep_rest, axis=0, keepdims=True)
    i2 = first_index_of(ep_rest, p2)

    den = p1 + p2
    wts_ref[0:1, :] = p1 / den * g_p
    wts_ref[1:2, :] = p2 / den * g_p
    eid_ref[0:1, :] = (g_idx * float(SUBLANES) + i1).astype(I32)
    eid_ref[1:2, :] = (g_idx * float(SUBLANES) + i2).astype(I32)


def _router(x1, g2, wt, bt, *, n_groups):
    m_rows, d = x1.shape
    return pl.pallas_call(
        functools.partial(_router_kernel, n_groups=n_groups),
        grid=(m_rows // TR,),
        in_specs=[
            pl.BlockSpec((TR, d), lambda i: (i, 0)),
            pl.BlockSpec((1, d), lambda i: (0, 0)),
            pl.BlockSpec(wt.shape, lambda i: (0, 0)),
            pl.BlockSpec(bt.shape, lambda i: (0, 0)),
        ],
        out_specs=[
            pl.BlockSpec((ROW_PAIR * TR, d // 2), lambda i: (i, 0)),
            pl.BlockSpec((TOP_K, TR), lambda i: (0, i)),
            pl.BlockSpec((TOP_K, TR), lambda i: (0, i)),
        ],
        out_shape=[
            jax.ShapeDtypeStruct((ROW_PAIR * m_rows, d // 2), BF16),
            jax.ShapeDtypeStruct((TOP_K, m_rows), I32),
            jax.ShapeDtypeStruct((TOP_K, m_rows), F32),
        ],
        compiler_params=_params(1),
        name="router",
    )(x1, g2, wt, bt)


INFO_TILE_EXPERT, INFO_NUM_TILES, INFO_VALID_END, INFO_END_TILE, INFO_ROWS = 0, 1, 2, 3, SUBLANES


def _positions_kernel(eid_ref, pos_ref, info_ref, exc_ref, *, n_experts, tile_shift):
    m_rows = eid_ref.shape[1]
    chunk = 2 * LANES
    n_chunks = m_rows // chunk
    tile = 1 << tile_shift
    e_col = lax.broadcasted_iota(I32, (n_experts, chunk), 0)
    upper = (lax.broadcasted_iota(I32, (chunk, chunk), 0)
             < lax.broadcasted_iota(I32, (chunk, chunk), 1)).astype(BF16)

    def onehots(c):
        off = pl.multiple_of(c * chunk, chunk)
        ids = eid_ref[:, pl.ds(off, chunk)]
        return off, [(e_col == ids[k:k + 1, :]) for k in range(TOP_K)]

    def count(c, carry):
        off, hot = onehots(c)
        used = jnp.where(hot[0] | hot[1], 1.0, 0.0)
        exc_ref[:, pl.ds(off, chunk)] = _dot(used.astype(BF16), upper) + carry
        return carry + jnp.sum(used, axis=1, keepdims=True)

    counts = lax.fori_loop(0, n_chunks, count, jnp.zeros((n_experts, 1), F32))

    lane = lax.broadcasted_iota(I32, (n_experts, LANES), 1)
    sub = lax.broadcasted_iota(I32, (n_experts, LANES), 0)

    def to_lanes(col):
        return jnp.sum(jnp.where(lane == sub, col, 0.0), axis=0, keepdims=True)

    n_tiles_e = ((counts.astype(I32) + (tile - 1)) >> tile_shift).astype(F32)
    first_tile = jnp.sum(jnp.where(lane < sub, to_lanes(n_tiles_e), 0.0), axis=1, keepdims=True)
    end_tile = first_tile + n_tiles_e
    first_row = first_tile * float(tile)

    def place(c, _):
        off, hot = onehots(c)
        dest = exc_ref[:, pl.ds(off, chunk)] + first_row
        for k in range(TOP_K):
            pos_ref[k:k + 1, pl.ds(off, chunk)] = jnp.sum(
                jnp.where(hot[k], dest, 0.0), axis=0, keepdims=True).astype(I32)
        return 0

    lax.fori_loop(0, n_chunks, place, 0)

    tile_expert = jnp.sum(jnp.where(end_tile <= lane.astype(F32), 1.0, 0.0), axis=0, keepdims=True)
    num_tiles = jnp.max(end_tile, axis=0, keepdims=True)
    valid_end = to_lanes(first_row + counts)
    info_ref[...] = jnp.zeros(info_ref.shape, I32)
    info_ref[INFO_TILE_EXPERT:INFO_TILE_EXPERT + 1, :] = jnp.minimum(
        tile_expert, float(n_experts - 1)).astype(I32)
    info_ref[INFO_NUM_TILES:INFO_NUM_TILES + 1, :] = jnp.broadcast_to(num_tiles, (1, LANES)).astype(I32)
    info_ref[INFO_VALID_END:INFO_VALID_END + 1, :] = valid_end.astype(I32)
    info_ref[INFO_END_TILE:INFO_END_TILE + 1, :] = to_lanes(end_tile).astype(I32)


def _positions(eid, *, n_experts):
    m_rows = eid.shape[1]
    tile_shift = T_EXP.bit_length() - 1
    return pl.pallas_call(
        functools.partial(_positions_kernel, n_experts=n_experts, tile_shift=tile_shift),
        out_shape=[
            jax.ShapeDtypeStruct((TOP_K, m_rows), I32),
            jax.ShapeDtypeStruct((INFO_ROWS, LANES), I32),
        ],
        scratch_shapes=[pltpu.VMEM((n_experts, m_rows), F32)],
        compiler_params=pltpu.CompilerParams(vmem_limit_bytes=VMEM_LIMIT_BYTES),
        name="positions",
    )(eid)


def _info(info_ref, row, lane):
    return info_ref[row * LANES + lane]


def _tile_state(i, info_ref):
    n_used = _info(info_ref, INFO_NUM_TILES, 0)
    ic = jnp.minimum(i, n_used - 1)
    e = _info(info_ref, INFO_TILE_EXPERT, ic)
    prev = _info(info_ref, INFO_TILE_EXPERT, jnp.maximum(ic - 1, 0))
    first = jnp.logical_or(ic == 0, e != prev)
    n_valid = jnp.minimum(_info(info_ref, INFO_VALID_END, e) - ic * T_EXP, T_EXP)
    return i < n_used, ic, e, first, n_valid


def _stream_expert_weights(i, info_ref, e, first, w_refs, wf_ref, wbf_ref, wsem, slot_ref, first_step=None):
    n_used = _info(info_ref, INFO_NUM_TILES, 0)
    rows = wf_ref.shape[2]
    chunk = min(rows, 512)

    def copies(expert, slot):
        return [pltpu.make_async_copy(w.at[expert], wf_ref.at[slot, j], wsem.at[slot])
                for j, w in enumerate(w_refs)]

    @pl.when(i == 0)
    def _():
        slot_ref[0] = 0
        for cp in copies(e, 0):
            cp.start(priority=1)
        if first_step is not None:
            first_step()

    @pl.when(jnp.logical_and(first, i > 0))
    def _():
        slot_ref[0] = 1 - slot_ref[0]

    @pl.when(first)
    def _():
        slot = slot_ref[0]
        for cp in copies(e, slot):
            cp.wait()
        for j in range(len(w_refs)):
            def cast(c, _, j=j):
                r0 = pl.multiple_of(c * chunk, chunk)
                wbf_ref[j, pl.ds(r0, chunk), :] = wf_ref[slot, j, pl.ds(r0, chunk), :].astype(BF16)
                return 0
            lax.fori_loop(0, rows // chunk, cast, 0)
        next_tile = _info(info_ref, INFO_END_TILE, e)

        @pl.when(next_tile < n_used)
        def _():
            for cp in copies(_info(info_ref, INFO_TILE_EXPERT, next_tile), 1 - slot):
                cp.start(priority=1)


def _expert_up_kernel(info_ref, pos_ref, xn_ref, wg_ref, wu_ref, h_ref,
                      xbuf_ref, wf_ref, wbf_ref, inv_ref, slot_ref, xsem, wsem, *, m_rows):
    i = pl.program_id(0)
    used, _, e, first, n_valid = _tile_state(i, info_ref)
    n_used = _info(info_ref, INFO_NUM_TILES, 0)

    def gather(tile, slot, n_rows, start):
        n_groups = (n_rows + (SUBLANES - 1)) >> (SUBLANES.bit_length() - 1)
        if start:
            first_row = tile * T_EXP
            last_row = first_row + n_rows - 1

            def body(g, _):
                for j in range(SUBLANES):
                    r = g * SUBLANES + j
                    src = inv_ref[jnp.minimum(first_row + r, last_row)]
                    pltpu.make_async_copy(_token_rows(xn_ref, src), _slot_rows(xbuf_ref.at[slot], g, j),
                                          xsem.at[slot]).start()
                return 0
            lax.fori_loop(0, n_groups, body, 0)
        else:
            for bit in range((T_EXP // SUBLANES).bit_length()):
                n = 1 << bit

                @pl.when((n_groups & n) != 0)
                def _():
                    pltpu.make_async_copy(xn_ref.at[pl.ds(0, n)], xbuf_ref.at[slot, pl.ds(0, n)],
                                          xsem.at[slot]).wait()

    def first_step():
        def invert(t, _):
            for k in range(TOP_K):
                inv_ref[pos_ref[k * m_rows + t]] = t
            return 0
        lax.fori_loop(0, m_rows, invert, 0, unroll=8)
        xbuf_ref[...] = jnp.zeros(xbuf_ref.shape, BF16)
        gather(0, 0, n_valid, True)

    @pl.when(used)
    def _():
        _stream_expert_weights(i, info_ref, e, first, (wg_ref, wu_ref), wf_ref, wbf_ref, wsem, slot_ref,
                               first_step)
        slot = i & 1
        gather(i, slot, n_valid, False)

        @pl.when(i + 1 < n_used)
        def _():
            gather(i + 1, 1 - slot, _tile_state(i + 1, info_ref)[4], True)

        pairs = xbuf_ref[slot].reshape(ROW_PAIR * T_EXP, xbuf_ref.shape[-1])
        x_lo, x_hi = [x.astype(BF16) for x in _unpack_halves(pairs)]
        half = x_lo.shape[1]

        def proj(j):
            return _dot(x_lo, wbf_ref[j, 0:half, :]) + _dot(x_hi, wbf_ref[j, half:, :])

        h = jax.nn.silu(proj(0)) * proj(1)
        h_ref[...] = h.astype(BF16)

    @pl.when(jnp.logical_not(used))
    def _():
        h_ref[...] = jnp.zeros(h_ref.shape, BF16)


def _expert_up(info_flat, pos_flat, xn, w_gate, w_up, *, n_tiles):
    m_rows, half = xn.shape[0] * SUBLANES, xn.shape[2]
    _, d, d_exp = w_gate.shape
    assert d == 2 * half
    p_rows = n_tiles * T_EXP
    any_spec = pl.BlockSpec(memory_space=pl.ANY)
    return pl.pallas_call(
        functools.partial(_expert_up_kernel, m_rows=m_rows),
        grid_spec=pltpu.PrefetchScalarGridSpec(
            num_scalar_prefetch=2,
            grid=(n_tiles,),
            in_specs=[any_spec, any_spec, any_spec],
            out_specs=pl.BlockSpec((T_EXP, d_exp), lambda i, info, pos: (i, 0)),
            scratch_shapes=[
                pltpu.VMEM((2, T_EXP // SUBLANES, GROUP_ROWS, half), BF16),
                pltpu.VMEM((2, 2, d, d_exp), F32),
                pltpu.VMEM((2, d, d_exp), BF16),
                pltpu.SMEM((p_rows,), I32),
                pltpu.SMEM((1,), I32),
                pltpu.SemaphoreType.DMA((2,)),
                pltpu.SemaphoreType.DMA((2,)),
            ],
        ),
        out_shape=jax.ShapeDtypeStruct((p_rows, d_exp), BF16),
        compiler_params=_params(1),
        name="expert_up",
    )(info_flat, pos_flat, xn, w_gate, w_up)


def _expert_down_kernel(info_ref, h_ref, wd_ref, y_ref, wf_ref, wbf_ref, slot_ref, wsem):
    i = pl.program_id(0)
    used, _, e, first, _ = _tile_state(i, info_ref)

    @pl.when(used)
    def _():
        _stream_expert_weights(i, info_ref, e, first, (wd_ref,), wf_ref, wbf_ref, wsem, slot_ref)
        y_ref[...] = _pack_halves(_dot(h_ref[...], wbf_ref[0]))

    @pl.when(jnp.logical_not(used))
    def _():
        y_ref[...] = jnp.zeros(y_ref.shape, BF16)


def _expert_down(info_flat, h, w_down, *, n_tiles):
    p_rows, d_exp = h.shape
    d = w_down.shape[-1]
    return pl.pallas_call(
        _expert_down_kernel,
        grid_spec=pltpu.PrefetchScalarGridSpec(
            num_scalar_prefetch=1,
            grid=(n_tiles,),
            in_specs=[
                pl.BlockSpec((T_EXP, d_exp), lambda i, info: (_tile_state(i, info)[1], 0)),
                pl.BlockSpec(memory_space=pl.ANY),
            ],
            out_specs=pl.BlockSpec((ROW_PAIR * T_EXP, d // 2), lambda i, info: (i, 0)),
            scratch_shapes=[
                pltpu.VMEM((2, 1, d_exp, d), F32),
                pltpu.VMEM((1, d_exp, d), BF16),
                pltpu.SMEM((1,), I32),
                pltpu.SemaphoreType.DMA((2,)),
            ],
        ),
        out_shape=jax.ShapeDtypeStruct((ROW_PAIR * p_rows, d // 2), BF16),
        compiler_params=_params(1),
        name="expert_down",
    )(info_flat, h, w_down)


def _combine_kernel(pos_ref, x1_ref, w_ref, g_ref, y_ref, out_ref, ybuf_ref, sem, *, m_rows, row0, n_steps):
    i = pl.program_id(0)

    def gather_start(step):
        base = row0 + step * TR
        slot = step & 1

        def body(g, _):
            for j in range(SUBLANES):
                for k in range(TOP_K):
                    src = pos_ref[k * m_rows + base + g * SUBLANES + j]
                    pltpu.make_async_copy(_token_rows(y_ref, src), _slot_rows(ybuf_ref.at[slot, k], g, j),
                                          sem.at[slot]).start(priority=k % 2)
            return 0
        lax.fori_loop(0, TR // SUBLANES, body, 0)

    def gather_wait(slot):
        for k in range(TOP_K):
            pltpu.make_async_copy(y_ref.at[pl.ds(0, TR // SUBLANES)], ybuf_ref.at[slot, k], sem.at[slot]).wait()

    @pl.when(i == 0)
    def _():
        gather_start(0)

    @pl.when(i + 1 < n_steps)
    def _():
        gather_start(i + 1)

    slot = i & 1
    gather_wait(slot)
    w = w_ref[...]
    y0, y1 = [_unpack_halves(ybuf_ref[slot, k].reshape(ROW_PAIR * TR, ybuf_ref.shape[-1])) for k in range(TOP_K)]
    moe = jnp.concatenate([w[:, 0:1] * y0[k] + w[:, 1:2] * y1[k] for k in range(2)], axis=-1)
    out_ref[...] = _rmsnorm_rows(x1_ref[...] + moe, g_ref[...])


def _combine(pos_flat, x1, wts_t, norm_f, y, *, row0, n_rows):
    m_rows, d = x1.shape
    tile0 = row0 // TR
    n_steps = n_rows // TR
    return pl.pallas_call(
        functools.partial(_combine_kernel, m_rows=m_rows, row0=row0, n_steps=n_steps),
        grid_spec=pltpu.PrefetchScalarGridSpec(
            num_scalar_prefetch=1,
            grid=(n_steps,),
            in_specs=[
                pl.BlockSpec((TR, d), lambda i, pos: (tile0 + i, 0)),
                pl.BlockSpec((TR, TOP_K), lambda i, pos: (tile0 + i, 0)),
                pl.BlockSpec((1, d), lambda i, pos: (0, 0)),
                pl.BlockSpec(memory_space=pl.ANY),
            ],
            out_specs=pl.BlockSpec((TR, d), lambda i, pos: (i, 0)),
            scratch_shapes=[pltpu.VMEM((2, TOP_K, TR // SUBLANES, GROUP_ROWS, d // 2), BF16),
                            pltpu.SemaphoreType.DMA((2,))],
        ),
        out_shape=jax.ShapeDtypeStruct((n_rows, d), F32),
        compiler_params=_params(1),
        name="combine",
    )(pos_flat, x1, wts_t, norm_f, y)


def _sample_headers(state, k_width):
    n, _, w = state.shape
    padded = jnp.pad(state, ((0, 0), (SUBLANES - (k_width - 1), 0), (0, 0)))
    return padded.reshape(n * SUBLANES, w)


def kernel(x_prompt, x_sample, state_conv_a, state_conv_r, state_h, meta_tokens, norm1, w_in, conv_a_w, conv_r_w,
           conv_r_b, lru_wa, lru_ba, lru_wx, lru_bx, lru_lam, w_br_a, w_br_r, w_o, norm2, w_group, b_group,
           w_router, b_router, w_gate, w_up, w_down, norm_f):
    batch, seq, d = x_prompt.shape
    dec_batch, dec_seq, _ = x_sample.shape
    depth = norm1.shape[0]
    width = w_br_a.shape[1]
    n_groups = w_group.shape[-1]
    n_experts = w_router.shape[-1]
    n_meta = meta_tokens.shape[0]
    assert depth == 1, "meta rows are folded into an initial state, which only carries one layer"
    assert dec_seq == SUBLANES and n_meta % SUBLANES == 0 and n_meta >= SUBLANES
    assert seq % TM == 0 and (dec_batch * dec_seq) % TM == 0
    assert n_groups <= SUBLANES and n_experts == n_groups * SUBLANES
    assert width % CB == 0 and d % (2 * CB) == 0 and CB % lru_wa.shape[-1] == 0

    n_prompt_rows = batch * seq
    n_sample_rows = dec_batch * dec_seq
    m_rows = n_prompt_rows + n_sample_rows
    n_prompt_tiles = n_prompt_rows // TM
    n_tiles = m_rows // TM
    tiles_per_seq = seq // TM
    n_exp_tiles = (TOP_K * m_rows) // T_EXP + n_experts
    assert n_exp_tiles <= LANES

    xp = x_prompt.reshape(n_prompt_rows, d)
    xs = x_sample.reshape(n_sample_rows, d)
    row = lambda v: v.reshape(1, -1)

    u, um = _norm1(xp, xs, meta_tokens, row(norm1[0]), n_prompt_tiles, n_tiles)

    def mixer_geometry(tm):
        assert seq % tm == 0 and n_sample_rows % tm == 0 and tm % MIX_CHUNK == 0 and MIX_CHUNK % SUBLANES == 0
        return dict(tm=tm, width=width, n_seq=batch, tiles_per_seq=seq // tm,
                    n_prompt_tiles=n_prompt_rows // tm, n_tiles=m_rows // tm)

    out_a, tail_a_p, tail_a_s = _mixer_a(
        u, um, w_in[0], conv_a_w[0], _sample_headers(state_conv_a[0], K_A), **mixer_geometry(TM_MIX_A))
    out_r, tail_r_p, tail_r_s, h_p, h_s = _mixer_r(
        u, um, w_in[0], conv_r_w[0], row(conv_r_b[0]), lru_wa[0], row(lru_ba[0]), lru_wx[0], row(lru_bx[0]),
        row(lru_lam[0]), _sample_headers(state_conv_r[0], K_R), state_h[0], seg0=3, **mixer_geometry(TM_MIX_R))

    gates = _gates(u, w_in[0], col0=5 * width, n_cols=2 * d, tm=TM_OPROJ)
    merged = _merge(gates, out_a, out_r, w_br_a[0], w_br_r[0], tm=TM_OPROJ)
    x1 = _oproj(merged, xp, xs, w_o[0], tm=TM_OPROJ)

    wt = jnp.concatenate([jnp.pad(w_group[0].T, ((0, SUBLANES - n_groups), (0, 0))), w_router[0].T], axis=0)
    bt = jnp.concatenate([jnp.pad(b_group[0], (0, SUBLANES - n_groups)), b_router[0]]).reshape(-1, 1)
    xn, eid, wts = _router(x1, row(norm2[0]), wt, bt, n_groups=n_groups)

    pos, info = _positions(eid, n_experts=n_experts)
    pos_flat = pos.reshape(-1)
    info_flat = info.reshape(-1)
    hid = _expert_up(info_flat, pos_flat, _grouped(xn), w_gate[0], w_up[0], n_tiles=n_exp_tiles)
    y = _grouped(_expert_down(info_flat, hid, w_down[0], n_tiles=n_exp_tiles))

    wts_t = wts.T
    g_f = row(norm_f)
    y_prompt = _combine(pos_flat, x1, wts_t, g_f, y, row0=0, n_rows=n_prompt_rows)
    y_sample = _combine(pos_flat, x1, wts_t, g_f, y, row0=n_prompt_rows, n_rows=n_sample_rows)

    def sample_tail(t, k):
        return t.reshape(dec_batch, SUBLANES, width)[:, SUBLANES - k:, :][None]

    return (
        y_prompt.reshape(batch, seq, d),
        y_sample.reshape(dec_batch, dec_seq, d),
        tail_a_p[:, SUBLANES - (K_A - 1):, :][None],
        tail_r_p[:, SUBLANES - (K_R - 1):, :][None],
        h_p[:, SUBLANES - 1, :][None],
        sample_tail(tail_a_s, K_A - 1),
        sample_tail(tail_r_s, K_R - 1),
        sample_tail(h_s, 1)[:, :, 0, :],
    )
```

```python
import functools

import jax
import jax.numpy as jnp
from jax import lax
from jax.experimental import pallas as pl
from jax.experimental.pallas import tpu as pltpu

F32, BF16, I32 = jnp.float32, jnp.bfloat16, jnp.int32

EPS = 1e-6
C_RG = 8.0
K_A = 3
K_R = 4
LRU_HEADS = 16
TOP_K = 2

SUBLANES = 8
LANES = 128
VMEM_LIMIT_BYTES = 56 * 1024 * 1024

TM = 512
TM_OPROJ = 1024
TM_MIX_A = 512
TM_MIX_R = 1024
MIX_CHUNK = 256
CB = 256
T_EXP = 256
TR = 256


def _dot(a, b):
    return jnp.dot(a, b, preferred_element_type=F32)


ROW_PAIR = 2


def _pack_halves(x):
    half = x.shape[1] // 2
    words = pltpu.pack_elementwise([x[:, :half], x[:, half:]], packed_dtype=BF16)
    return pltpu.bitcast(words, BF16)


def _unpack_halves(pairs):
    words = pltpu.bitcast(pairs, jnp.uint32)
    return [pltpu.unpack_elementwise(words, index=k, packed_dtype=BF16, unpacked_dtype=F32) for k in range(2)]


GROUP_ROWS = ROW_PAIR * SUBLANES


def _grouped(x):
    return x.reshape(x.shape[0] // GROUP_ROWS, GROUP_ROWS, x.shape[1])


def _token_rows(ref, token):
    shift = SUBLANES.bit_length() - 1
    row = pl.multiple_of(ROW_PAIR * (token & (SUBLANES - 1)), ROW_PAIR)
    return ref.at[token >> shift, pl.ds(row, ROW_PAIR), :]


def _slot_rows(ref, group, j):
    return ref.at[group, pl.ds(ROW_PAIR * j, ROW_PAIR), :]


def _rmsnorm_rows(x, g):
    y = x * lax.rsqrt(jnp.mean(x * x, axis=-1, keepdims=True) + EPS)
    return y * g


def _params(n_axes):
    return pltpu.CompilerParams(dimension_semantics=("arbitrary",) * n_axes,
                                vmem_limit_bytes=VMEM_LIMIT_BYTES)


def _norm1_kernel(xp_ref, xs_ref, meta_ref, g_ref, u_ref, um_ref, *, n_prompt_tiles):
    i = pl.program_id(0)
    g = g_ref[...]

    @pl.when(i < n_prompt_tiles)
    def _():
        u_ref[...] = _rmsnorm_rows(xp_ref[...], g).astype(BF16)

    @pl.when(i >= n_prompt_tiles)
    def _():
        u_ref[...] = _rmsnorm_rows(xs_ref[...], g).astype(BF16)

    @pl.when(i == 0)
    def _():
        um_ref[...] = _rmsnorm_rows(meta_ref[...], g).astype(BF16)


def _norm1(xp, xs, meta, g, n_prompt_tiles, n_tiles):
    d = xp.shape[1]
    return pl.pallas_call(
        functools.partial(_norm1_kernel, n_prompt_tiles=n_prompt_tiles),
        grid=(n_tiles,),
        in_specs=[
            pl.BlockSpec((TM, d), lambda i: (jnp.minimum(i, n_prompt_tiles - 1), 0)),
            pl.BlockSpec((TM, d), lambda i: (jnp.maximum(i - n_prompt_tiles, 0), 0)),
            pl.BlockSpec(meta.shape, lambda i: (0, 0)),
            pl.BlockSpec((1, d), lambda i: (0, 0)),
        ],
        out_specs=[
            pl.BlockSpec((TM, d), lambda i: (i, 0)),
            pl.BlockSpec(meta.shape, lambda i: (0, 0)),
        ],
        out_shape=[
            jax.ShapeDtypeStruct((n_tiles * TM, d), BF16),
            jax.ShapeDtypeStruct(meta.shape, BF16),
        ],
        compiler_params=_params(1),
        name="norm1",
    )(xp, xs, meta, g)


def _conv_taps(window, cw, k_width):
    acc = window(0) * cw[0:1, :]
    for k in range(1, k_width):
        acc = acc + window(k) * cw[k:k + 1, :]
    return acc


def _conv_flat(ext_ref, values, cw, k_width, r0):
    rows = values.shape[0]
    ext_ref[SUBLANES + r0:SUBLANES + r0 + rows, :] = values
    base = SUBLANES - (k_width - 1) + r0
    return _conv_taps(lambda k: ext_ref[pl.ds(base + k, rows), :], cw, k_width)


def _conv_grouped(ext3_ref, values, header, cw, k_width, s0):
    rows, cols = values.shape
    n_seq = rows // SUBLANES
    ext3_ref[s0:s0 + n_seq, 0:SUBLANES, :] = header.reshape(n_seq, SUBLANES, cols)
    ext3_ref[s0:s0 + n_seq, SUBLANES:, :] = values.reshape(n_seq, SUBLANES, cols)
    base = SUBLANES - (k_width - 1)
    return _conv_taps(
        lambda k: ext3_ref[s0:s0 + n_seq, base + k:base + k + SUBLANES, :].reshape(rows, cols), cw, k_width)


def _mixer_a_kernel(u_ref, um_ref, wb_ref, wc_ref, wv_ref, cw_ref, hdr_ref,
                    oa_ref, tailp_ref, tails_ref,
                    wbf_ref, ext_ref, ext3_ref, carry_ref, mhdr_ref,
                    *, tm, tiles_per_seq, n_prompt_tiles, n_meta):
    m = pl.program_id(1)
    cw = cw_ref[...]
    ch = MIX_CHUNK
    n_chunks = tm // ch

    @pl.when(m == 0)
    def _():
        wbf_ref[0] = wb_ref[...].astype(BF16)
        wbf_ref[1] = wc_ref[...].astype(BF16)
        wbf_ref[2] = wv_ref[...].astype(BF16)
        um = um_ref[...]
        cv_meta = _dot(um, wbf_ref[1]) * _dot(um, wbf_ref[2])
        mhdr_ref[...] = cv_meta[n_meta - SUBLANES:, :]

    def products(r0):
        u = u_ref[r0:r0 + ch, :]
        return _dot(u, wbf_ref[0]), _dot(u, wbf_ref[1]) * _dot(u, wbf_ref[2])

    @pl.when(m < n_prompt_tiles)
    def _():
        @pl.when((m % tiles_per_seq) == 0)
        def _():
            carry_ref[...] = mhdr_ref[...]

        ext_ref[0:SUBLANES, :] = carry_ref[...]
        for c in range(n_chunks):
            r0 = c * ch
            zb, cv = products(r0)
            conv = _conv_flat(ext_ref, cv, cw, K_A, r0)
            oa_ref[r0:r0 + ch, :] = (zb * conv).astype(BF16)
        carry_ref[...] = ext_ref[tm:, :]
        tailp_ref[0] = ext_ref[tm:, :]

    @pl.when(m >= n_prompt_tiles)
    def _():
        for c in range(n_chunks):
            r0 = c * ch
            zb, cv = products(r0)
            conv = _conv_grouped(ext3_ref, cv, hdr_ref[r0:r0 + ch, :], cw, K_A, r0 // SUBLANES)
            oa_ref[r0:r0 + ch, :] = (zb * conv).astype(BF16)
            tails_ref[r0:r0 + ch, :] = cv


def _mixer_a(u, um, w_in, conv_w, hdr_s, *, tm, width, n_seq, tiles_per_seq, n_prompt_tiles, n_tiles):
    d = u.shape[1]
    nb = width // CB
    n_meta = um.shape[0]
    last_seq = n_seq - 1

    def wspec(seg):
        return pl.BlockSpec((d, CB), lambda j, m, seg=seg: (0, seg * nb + j))

    return pl.pallas_call(
        functools.partial(_mixer_a_kernel, tm=tm, tiles_per_seq=tiles_per_seq,
                          n_prompt_tiles=n_prompt_tiles, n_meta=n_meta),
        grid=(nb, n_tiles),
        in_specs=[
            pl.BlockSpec((tm, d), lambda j, m: (m, 0)),
            pl.BlockSpec(um.shape, lambda j, m: (0, 0)),
            wspec(0), wspec(1), wspec(2),
            pl.BlockSpec((K_A, CB), lambda j, m: (0, j)),
            pl.BlockSpec((tm, CB), lambda j, m: (jnp.maximum(m - n_prompt_tiles, 0), j)),
        ],
        out_specs=[
            pl.BlockSpec((tm, CB), lambda j, m: (m, j)),
            pl.BlockSpec((1, SUBLANES, CB),
                         lambda j, m: (jnp.minimum(m // tiles_per_seq, last_seq), 0, j)),
            pl.BlockSpec((tm, CB), lambda j, m: (jnp.maximum(m - n_prompt_tiles, 0), j)),
        ],
        out_shape=[
            jax.ShapeDtypeStruct((n_tiles * tm, width), BF16),
            jax.ShapeDtypeStruct((n_seq, SUBLANES, width), F32),
            jax.ShapeDtypeStruct(hdr_s.shape, F32),
        ],
        scratch_shapes=[
            pltpu.VMEM((3, d, CB), BF16),
            pltpu.VMEM((tm + SUBLANES, CB), F32),
            pltpu.VMEM((tm // SUBLANES, 2 * SUBLANES, CB), F32),
            pltpu.VMEM((SUBLANES, CB), F32),
            pltpu.VMEM((SUBLANES, CB), F32),
        ],
        compiler_params=_params(2),
        name="mixer_a",
    )(u, um, w_in, w_in, w_in, conv_w, hdr_s)


def _softplus(x):
    return jnp.maximum(x, 0.0) + jnp.log1p(jnp.exp(-jnp.abs(x)))


def _lru_kernel(u_ref, um_ref, wx_ref, wy_ref, cw_ref, cbias_ref, wa_ref, ba_ref, wi_ref, bi_ref, lam_ref,
                hdr_ref, h0_ref,
                or_ref, tailp_ref, tails_ref, hp_ref, hs_ref,
                wbf_ref, gbf_ref, ext_ref, ext3_ref, carry_ref, hcarry_ref,
                mhdr_ref, mh_ref,
                *, tm, tiles_per_seq, n_prompt_tiles, n_meta, lru_block):
    m = pl.program_id(1)
    cw = cw_ref[...]
    cbias = cbias_ref[...]
    heads = CB // lru_block
    ch = MIX_CHUNK
    n_chunks = tm // ch
    row8 = lax.broadcasted_iota(I32, (SUBLANES, CB), 0)

    def gate_terms(xc):
        xb = xc.astype(BF16)
        ra, ri = [], []
        for hh in range(heads):
            xh = xb[:, hh * lru_block:(hh + 1) * lru_block]
            ra.append(_dot(xh, gbf_ref[0, hh]))
            ri.append(_dot(xh, gbf_ref[1, hh]))
        r = jax.nn.sigmoid(jnp.concatenate(ra, axis=-1) + ba_ref[...])
        i = jax.nn.sigmoid(jnp.concatenate(ri, axis=-1) + bi_ref[...])
        log_a = -C_RG * r * _softplus(-lam_ref[...])
        a = jnp.exp(log_a)
        b = jnp.sqrt((1.0 + a * a) * jnp.tanh(-log_a)) * i * xc
        return a, b

    def scan(a, b, h_prev, h0_row=None):
        out = []
        for g in range(a.shape[0] // SUBLANES):
            a8 = a[g * SUBLANES:(g + 1) * SUBLANES, :]
            b8 = b[g * SUBLANES:(g + 1) * SUBLANES, :]
            for dist in (1, 2, 4):
                keep = row8 >= dist
                b8 = jnp.where(keep, a8 * pltpu.roll(b8, dist, 0) + b8, b8)
                a8 = jnp.where(keep, a8 * pltpu.roll(a8, dist, 0), a8)
            h8 = a8 * (h_prev if h0_row is None else h0_row(g)) + b8
            out.append(h8)
            h_prev = h8[SUBLANES - 1:SUBLANES, :]
        return jnp.concatenate(out, axis=0), h_prev

    @pl.when(m == 0)
    def _():
        wbf_ref[0] = wx_ref[...].astype(BF16)
        wbf_ref[1] = wy_ref[...].astype(BF16)
        gbf_ref[0] = wa_ref[...].astype(BF16)
        gbf_ref[1] = wi_ref[...].astype(BF16)
        zx_meta = _dot(um_ref[...], wbf_ref[0])
        ext_ref[0:SUBLANES, :] = jnp.zeros((SUBLANES, CB), F32)
        a, b = gate_terms(_conv_flat(ext_ref, zx_meta, cw, K_R, 0) + cbias)
        _, h_meta = scan(a, b, jnp.zeros((1, CB), F32))
        mh_ref[...] = jnp.broadcast_to(h_meta, (SUBLANES, CB))
        mhdr_ref[...] = zx_meta[n_meta - SUBLANES:, :]

    def products(r0):
        u = u_ref[r0:r0 + ch, :]
        return _dot(u, wbf_ref[0]), _dot(u, wbf_ref[1])

    @pl.when(m < n_prompt_tiles)
    def _():
        @pl.when((m % tiles_per_seq) == 0)
        def _():
            carry_ref[...] = mhdr_ref[...]
            hcarry_ref[...] = mh_ref[...]

        ext_ref[0:SUBLANES, :] = carry_ref[...]
        h_prev = hcarry_ref[0:1, :]
        for c in range(n_chunks):
            r0 = c * ch
            zx, zy = products(r0)
            a, b = gate_terms(_conv_flat(ext_ref, zx, cw, K_R, r0) + cbias)
            h, h_prev = scan(a, b, h_prev)
            or_ref[r0:r0 + ch, :] = (h * jax.nn.gelu(zy)).astype(BF16)
        hcarry_ref[0:1, :] = h_prev
        carry_ref[...] = ext_ref[tm:, :]
        tailp_ref[0] = ext_ref[tm:, :]
        hp_ref[0] = h[ch - SUBLANES:, :]

    @pl.when(m >= n_prompt_tiles)
    def _():
        for c in range(n_chunks):
            r0 = c * ch
            s0 = r0 // SUBLANES
            zx, zy = products(r0)
            xc = _conv_grouped(ext3_ref, zx, hdr_ref[r0:r0 + ch, :], cw, K_R, s0) + cbias
            a, b = gate_terms(xc)
            h, _ = scan(a, b, None, lambda g, s0=s0: h0_ref[s0 + g:s0 + g + 1, :])
            or_ref[r0:r0 + ch, :] = (h * jax.nn.gelu(zy)).astype(BF16)
            tails_ref[r0:r0 + ch, :] = zx
            hs_ref[r0:r0 + ch, :] = h


def _mixer_r(u, um, w_in, conv_w, conv_b, lru_wa, lru_ba, lru_wx, lru_bx, lru_lam, hdr_s, h0_s,
             *, tm, width, seg0, n_seq, tiles_per_seq, n_prompt_tiles, n_tiles):
    d = u.shape[1]
    nb = width // CB
    n_meta = um.shape[0]
    last_seq = n_seq - 1
    lru_block = lru_wa.shape[-1]
    heads = CB // lru_block
    seqs_per_tile = tm // SUBLANES

    def wspec(seg):
        return pl.BlockSpec((d, CB), lambda j, m, seg=seg: (0, seg * nb + j))

    def vec():
        return pl.BlockSpec((1, CB), lambda j, m: (0, j))

    def gspec():
        return pl.BlockSpec((heads, lru_block, lru_block), lambda j, m: (j, 0, 0))

    def sample_rows():
        return pl.BlockSpec((tm, CB), lambda j, m: (jnp.maximum(m - n_prompt_tiles, 0), j))

    def seq_tail():
        return pl.BlockSpec((1, SUBLANES, CB),
                            lambda j, m: (jnp.minimum(m // tiles_per_seq, last_seq), 0, j))

    return pl.pallas_call(
        functools.partial(_lru_kernel, tm=tm, tiles_per_seq=tiles_per_seq, n_prompt_tiles=n_prompt_tiles,
                          n_meta=n_meta, lru_block=lru_block),
        grid=(nb, n_tiles),
        in_specs=[
            pl.BlockSpec((tm, d), lambda j, m: (m, 0)),
            pl.BlockSpec(um.shape, lambda j, m: (0, 0)),
            wspec(seg0), wspec(seg0 + 1),
            pl.BlockSpec((K_R, CB), lambda j, m: (0, j)),
            vec(),
            gspec(), vec(), gspec(), vec(), vec(),
            sample_rows(),
            pl.BlockSpec((seqs_per_tile, CB), lambda j, m: (jnp.maximum(m - n_prompt_tiles, 0), j)),
        ],
        out_specs=[
            pl.BlockSpec((tm, CB), lambda j, m: (m, j)),
            seq_tail(), sample_rows(), seq_tail(), sample_rows(),
        ],
        out_shape=[
            jax.ShapeDtypeStruct((n_tiles * tm, width), BF16),
            jax.ShapeDtypeStruct((n_seq, SUBLANES, width), F32),
            jax.ShapeDtypeStruct(hdr_s.shape, F32),
            jax.ShapeDtypeStruct((n_seq, SUBLANES, width), F32),
            jax.ShapeDtypeStruct(hdr_s.shape, F32),
        ],
        scratch_shapes=[
            pltpu.VMEM((2, d, CB), BF16),
            pltpu.VMEM((2, heads, lru_block, lru_block), BF16),
            pltpu.VMEM((tm + SUBLANES, CB), F32),
            pltpu.VMEM((seqs_per_tile, 2 * SUBLANES, CB), F32),
            pltpu.VMEM((SUBLANES, CB), F32),
            pltpu.VMEM((SUBLANES, CB), F32),
            pltpu.VMEM((SUBLANES, CB), F32),
            pltpu.VMEM((SUBLANES, CB), F32),
        ],
        compiler_params=_params(2),
        name="mixer_r",
    )(u, um, w_in, w_in, conv_w, conv_b, lru_wa, lru_ba, lru_wx, lru_bx, lru_lam, hdr_s, h0_s)


def _gates_kernel(u_ref, w_ref, g_ref, wbf_ref):
    @pl.when(pl.program_id(1) == 0)
    def _():
        wbf_ref[...] = w_ref[...].astype(BF16)

    g_ref[...] = jax.nn.sigmoid(_dot(u_ref[...], wbf_ref[...]))


def _gates(u, w_in, *, col0, n_cols, tm):
    m_rows, d = u.shape
    cb = 2 * CB
    return pl.pallas_call(
        _gates_kernel,
        grid=(n_cols // cb, m_rows // tm),
        in_specs=[
            pl.BlockSpec((tm, d), lambda j, m: (m, 0)),
            pl.BlockSpec((d, cb), lambda j, m: (0, col0 // cb + j)),
        ],
        out_specs=pl.BlockSpec((tm, cb), lambda j, m: (m, j)),
        out_shape=jax.ShapeDtypeStruct((m_rows, n_cols), F32),
        scratch_shapes=[pltpu.VMEM((d, cb), BF16)],
        compiler_params=_params(2),
        name="gates",
    )(u, w_in)


def _merge_kernel(ga_ref, gr_ref, oa_ref, or_ref, wba_ref, wbr_ref, mg_ref, wb_bf):
    @pl.when(pl.program_id(1) == 0)
    def _():
        wb_bf[0] = wba_ref[...].astype(BF16)
        wb_bf[1] = wbr_ref[...].astype(BF16)

    mg = ga_ref[...] * _dot(oa_ref[...], wb_bf[0]) + gr_ref[...] * _dot(or_ref[...], wb_bf[1])
    mg_ref[...] = mg.astype(BF16)


def _merge(gates, out_a, out_r, w_br_a, w_br_r, *, tm):
    m_rows, width = out_a.shape
    d = w_br_a.shape[1]
    cb = 2 * CB
    nb = d // cb
    return pl.pallas_call(
        _merge_kernel,
        grid=(nb, m_rows // tm),
        in_specs=[
            pl.BlockSpec((tm, cb), lambda j, m: (m, j)),
            pl.BlockSpec((tm, cb), lambda j, m: (m, nb + j)),
            pl.BlockSpec((tm, width), lambda j, m: (m, 0)),
            pl.BlockSpec((tm, width), lambda j, m: (m, 0)),
            pl.BlockSpec((width, cb), lambda j, m: (0, j)),
            pl.BlockSpec((width, cb), lambda j, m: (0, j)),
        ],
        out_specs=pl.BlockSpec((tm, cb), lambda j, m: (m, j)),
        out_shape=jax.ShapeDtypeStruct((m_rows, d), BF16),
        scratch_shapes=[pltpu.VMEM((2, width, cb), BF16)],
        compiler_params=_params(2),
        name="merge",
    )(gates, gates, out_a, out_r, w_br_a, w_br_r)


def _oproj_kernel(mg_ref, xp_ref, xs_ref, wo_ref, x1_ref, wbf_ref, *, n_prompt_tiles):
    m = pl.program_id(1)

    @pl.when(m == 0)
    def _():
        wbf_ref[...] = wo_ref[...].astype(BF16)

    y = _dot(mg_ref[...], wbf_ref[...])

    @pl.when(m < n_prompt_tiles)
    def _():
        x1_ref[...] = xp_ref[...] + y

    @pl.when(m >= n_prompt_tiles)
    def _():
        x1_ref[...] = xs_ref[...] + y


def _oproj(mg, xp, xs, w_o, *, tm):
    m_rows, d = mg.shape
    cb = 2 * CB
    nb = d // cb
    n_prompt_tiles = xp.shape[0] // tm
    n_tiles = m_rows // tm
    assert xp.shape[0] % tm == 0 and xs.shape[0] % tm == 0
    return pl.pallas_call(
        functools.partial(_oproj_kernel, n_prompt_tiles=n_prompt_tiles),
        grid=(nb, n_tiles),
        in_specs=[
            pl.BlockSpec((tm, d), lambda j, m: (m, 0)),
            pl.BlockSpec((tm, cb), lambda j, m: (jnp.minimum(m, n_prompt_tiles - 1), j)),
            pl.BlockSpec((tm, cb), lambda j, m: (jnp.maximum(m - n_prompt_tiles, 0), j)),
            pl.BlockSpec((d, cb), lambda j, m: (0, j)),
        ],
        out_specs=pl.BlockSpec((tm, cb), lambda j, m: (m, j)),
        out_shape=jax.ShapeDtypeStruct((m_rows, d), F32),
        scratch_shapes=[pltpu.VMEM((d, cb), BF16)],
        compiler_params=_params(2),
        name="oproj",
    )(mg, xp, xs, w_o)


def _router_kernel(x1_ref, g_ref, wt_ref, bt_ref, xn_ref, eid_ref, wts_ref, *, n_groups):
    xn = _rmsnorm_rows(x1_ref[...], g_ref[...])
    xn_ref[...] = _pack_halves(xn)
    lg = lax.dot_general(wt_ref[...].astype(BF16), xn.astype(BF16), (((1,), (1,)), ((), ())),
                         preferred_element_type=F32) + bt_ref[...]
    rows = xn.shape[0]
    row8 = lax.broadcasted_iota(I32, (SUBLANES, rows), 0).astype(F32)

    def first_index_of(v, vmax):
        return jnp.min(jnp.where(v == vmax, row8, float(SUBLANES)), axis=0, keepdims=True)

    gl = jnp.where(row8 < float(n_groups), lg[0:SUBLANES, :], -jnp.inf)
    ge = jnp.exp(gl - jnp.max(gl, axis=0, keepdims=True))
    gprob = ge / jnp.sum(ge, axis=0, keepdims=True)
    g_p = jnp.max(gprob, axis=0, keepdims=True)
    g_idx = first_index_of(gprob, g_p)

    esel = jnp.zeros((SUBLANES, rows), F32)
    for k in range(n_groups):
        esel = jnp.where(g_idx == float(k), lg[(k + 1) * SUBLANES:(k + 2) * SUBLANES, :], esel)
    ee = jnp.exp(esel - jnp.max(esel, axis=0, keepdims=True))
    ep = ee / jnp.sum(ee, axis=0, keepdims=True)
    p1 = jnp.max(ep, axis=0, keepdims=True)
    i1 = first_index_of(ep, p1)
    ep_rest = jnp.where(row8 == i1, -1.0, ep)
    p2 = jnp.max(ep_rest, axis=0, keepdims=True)
    i2 = first_index_of(ep_rest, p2)

    den = p1 + p2
    wts_ref[0:1, :] = p1 / den * g_p
    wts_ref[1:2, :] = p2 / den * g_p
    eid_ref[0:1, :] = (g_idx * float(SUBLANES) + i1).astype(I32)
    eid_ref[1:2, :] = (g_idx * float(SUBLANES) + i2).astype(I32)


def _router(x1, g2, wt, bt, *, n_groups):
    m_rows, d = x1.shape
    return pl.pallas_call(
        functools.partial(_router_kernel, n_groups=n_groups),
        grid=(m_rows // TR,),
        in_specs=[
            pl.BlockSpec((TR, d), lambda i: (i, 0)),
            pl.BlockSpec((1, d), lambda i: (0, 0)),
            pl.BlockSpec(wt.shape, lambda i: (0, 0)),
            pl.BlockSpec(bt.shape, lambda i: (0, 0)),
        ],
        out_specs=[
            pl.BlockSpec((ROW_PAIR * TR, d // 2), lambda i: (i, 0)),
            pl.BlockSpec((TOP_K, TR), lambda i: (0, i)),
            pl.BlockSpec((TOP_K, TR), lambda i: (0, i)),
        ],
        out_shape=[
            jax.ShapeDtypeStruct((ROW_PAIR * m_rows, d // 2), BF16),
            jax.ShapeDtypeStruct((TOP_K, m_rows), I32),
            jax.ShapeDtypeStruct((TOP_K, m_rows), F32),
        ],
        compiler_params=_params(1),
        name="router",
    )(x1, g2, wt, bt)


INFO_TILE_EXPERT, INFO_NUM_TILES, INFO_VALID_END, INFO_END_TILE, INFO_ROWS = 0, 1, 2, 3, SUBLANES


def _positions_kernel(eid_ref, pos_ref, info_ref, exc_ref, *, n_experts, tile_shift):
    m_rows = eid_ref.shape[1]
    chunk = 2 * LANES
    n_chunks = m_rows // chunk
    tile = 1 << tile_shift
    e_col = lax.broadcasted_iota(I32, (n_experts, chunk), 0)
    upper = (lax.broadcasted_iota(I32, (chunk, chunk), 0)
             < lax.broadcasted_iota(I32, (chunk, chunk), 1)).astype(BF16)

    def onehots(c):
        off = pl.multiple_of(c * chunk, chunk)
        ids = eid_ref[:, pl.ds(off, chunk)]
        return off, [(e_col == ids[k:k + 1, :]) for k in range(TOP_K)]

    def count(c, carry):
        off, hot = onehots(c)
        used = jnp.where(hot[0] | hot[1], 1.0, 0.0)
        exc_ref[:, pl.ds(off, chunk)] = _dot(used.astype(BF16), upper) + carry
        return carry + jnp.sum(used, axis=1, keepdims=True)

    counts = lax.fori_loop(0, n_chunks, count, jnp.zeros((n_experts, 1), F32))

    lane = lax.broadcasted_iota(I32, (n_experts, LANES), 1)
    sub = lax.broadcasted_iota(I32, (n_experts, LANES), 0)

    def to_lanes(col):
        return jnp.sum(jnp.where(lane == sub, col, 0.0), axis=0, keepdims=True)

    n_tiles_e = ((counts.astype(I32) + (tile - 1)) >> tile_shift).astype(F32)
    first_tile = jnp.sum(jnp.where(lane < sub, to_lanes(n_tiles_e), 0.0), axis=1, keepdims=True)
    end_tile = first_tile + n_tiles_e
    first_row = first_tile * float(tile)

    def place(c, _):
        off, hot = onehots(c)
        dest = exc_ref[:, pl.ds(off, chunk)] + first_row
        for k in range(TOP_K):
            pos_ref[k:k + 1, pl.ds(off, chunk)] = jnp.sum(
                jnp.where(hot[k], dest, 0.0), axis=0, keepdims=True).astype(I32)
        return 0

    lax.fori_loop(0, n_chunks, place, 0)

    tile_expert = jnp.sum(jnp.where(end_tile <= lane.astype(F32), 1.0, 0.0), axis=0, keepdims=True)
    num_tiles = jnp.max(end_tile, axis=0, keepdims=True)
    valid_end = to_lanes(first_row + counts)
    info_ref[...] = jnp.zeros(info_ref.shape, I32)
    info_ref[INFO_TILE_EXPERT:INFO_TILE_EXPERT + 1, :] = jnp.minimum(
        tile_expert, float(n_experts - 1)).astype(I32)
    info_ref[INFO_NUM_TILES:INFO_NUM_TILES + 1, :] = jnp.broadcast_to(num_tiles, (1, LANES)).astype(I32)
    info_ref[INFO_VALID_END:INFO_VALID_END + 1, :] = valid_end.astype(I32)
    info_ref[INFO_END_TILE:INFO_END_TILE + 1, :] = to_lanes(end_tile).astype(I32)


def _positions(eid, *, n_experts):
    m_rows = eid.shape[1]
    tile_shift = T_EXP.bit_length() - 1
    return pl.pallas_call(
        functools.partial(_positions_kernel, n_experts=n_experts, tile_shift=tile_shift),
        out_shape=[
            jax.ShapeDtypeStruct((TOP_K, m_rows), I32),
            jax.ShapeDtypeStruct((INFO_ROWS, LANES), I32),
        ],
        scratch_shapes=[pltpu.VMEM((n_experts, m_rows), F32)],
        compiler_params=pltpu.CompilerParams(vmem_limit_bytes=VMEM_LIMIT_BYTES),
        name="positions",
    )(eid)


def _info(info_ref, row, lane):
    return info_ref[row * LANES + lane]


def _tile_state(i, info_ref):
    n_used = _info(info_ref, INFO_NUM_TILES, 0)
    ic = jnp.minimum(i, n_used - 1)
    e = _info(info_ref, INFO_TILE_EXPERT, ic)
    prev = _info(info_ref, INFO_TILE_EXPERT, jnp.maximum(ic - 1, 0))
    first = jnp.logical_or(ic == 0, e != prev)
    n_valid = jnp.minimum(_info(info_ref, INFO_VALID_END, e) - ic * T_EXP, T_EXP)
    return i < n_used, ic, e, first, n_valid


def _stream_expert_weights(i, info_ref, e, first, w_refs, wf_ref, wbf_ref, wsem, slot_ref, first_step=None):
    n_used = _info(info_ref, INFO_NUM_TILES, 0)
    n_slots, _, rows, _ = wf_ref.shape
    chunk = min(rows, 512)

    def copies(expert, slot):
        return [pltpu.make_async_copy(w.at[expert], wf_ref.at[slot, j], wsem.at[slot])
                for j, w in enumerate(w_refs)]

    @pl.when(i == 0)
    def _():
        slot_ref[0] = 0
        for cp in copies(e, 0):
            cp.start(priority=1)
        if first_step is not None:
            first_step()

    if n_slots > 1:
        @pl.when(jnp.logical_and(first, i > 0))
        def _():
            slot_ref[0] = 1 - slot_ref[0]

    @pl.when(first)
    def _():
        slot = slot_ref[0] if n_slots > 1 else 0
        for cp in copies(e, slot):
            cp.wait()
        for j in range(len(w_refs)):
            def cast(c, _, j=j):
                r0 = pl.multiple_of(c * chunk, chunk)
                wbf_ref[j, pl.ds(r0, chunk), :] = wf_ref[slot, j, pl.ds(r0, chunk), :].astype(BF16)
                return 0
            lax.fori_loop(0, rows // chunk, cast, 0)
        next_tile = _info(info_ref, INFO_END_TILE, e)

        @pl.when(next_tile < n_used)
        def _():
            for cp in copies(_info(info_ref, INFO_TILE_EXPERT, next_tile), 1 - slot if n_slots > 1 else 0):
                cp.start(priority=1)


def _experts_kernel(info_ref, pos_ref, xn_ref, wg_ref, wu_ref, wd_ref, y_ref,
                    xbuf_ref, wf_ref, wbf_ref, wdf_ref, wdbf_ref, inv_ref, slot_ref, xsem, wsem, wdsem, *, m_rows):
    i = pl.program_id(0)
    used, _, e, first, n_valid = _tile_state(i, info_ref)
    n_used = _info(info_ref, INFO_NUM_TILES, 0)

    def gather(tile, slot, n_rows, start):
        n_groups = (n_rows + (SUBLANES - 1)) >> (SUBLANES.bit_length() - 1)
        if start:
            first_row = tile * T_EXP
            last_row = first_row + n_rows - 1

            def body(g, _):
                for j in range(SUBLANES):
                    r = g * SUBLANES + j
                    src = inv_ref[jnp.minimum(first_row + r, last_row)]
                    pltpu.make_async_copy(_token_rows(xn_ref, src), _slot_rows(xbuf_ref.at[slot], g, j),
                                          xsem.at[slot]).start()
                return 0
            lax.fori_loop(0, n_groups, body, 0)
        else:
            for bit in range((T_EXP // SUBLANES).bit_length()):
                n = 1 << bit

                @pl.when((n_groups & n) != 0)
                def _():
                    pltpu.make_async_copy(xn_ref.at[pl.ds(0, n)], xbuf_ref.at[slot, pl.ds(0, n)],
                                          xsem.at[slot]).wait()

    def first_step():
        def invert(t, _):
            for k in range(TOP_K):
                inv_ref[pos_ref[k * m_rows + t]] = t
            return 0
        lax.fori_loop(0, m_rows, invert, 0, unroll=8)
        xbuf_ref[...] = jnp.zeros(xbuf_ref.shape, BF16)
        gather(0, 0, n_valid, True)

    @pl.when(used)
    def _():
        _stream_expert_weights(i, info_ref, e, first, (wg_ref, wu_ref), wf_ref, wbf_ref, wsem, slot_ref,
                               first_step)
        _stream_expert_weights(i, info_ref, e, first, (wd_ref,), wdf_ref, wdbf_ref, wdsem, slot_ref)
        slot = i & 1
        gather(i, slot, n_valid, False)

        @pl.when(i + 1 < n_used)
        def _():
            gather(i + 1, 1 - slot, _tile_state(i + 1, info_ref)[4], True)

        pairs = xbuf_ref[slot].reshape(ROW_PAIR * T_EXP, xbuf_ref.shape[-1])
        x_lo, x_hi = [x.astype(BF16) for x in _unpack_halves(pairs)]
        half = x_lo.shape[1]

        def proj(j):
            return _dot(x_lo, wbf_ref[j, 0:half, :]) + _dot(x_hi, wbf_ref[j, half:, :])

        h = (jax.nn.silu(proj(0)) * proj(1)).astype(BF16)
        y_ref[...] = _pack_halves(_dot(h, wdbf_ref[0]))

    @pl.when(jnp.logical_not(used))
    def _():
        y_ref[...] = jnp.zeros(y_ref.shape, BF16)


def _experts(info_flat, pos_flat, xn, w_gate, w_up, w_down, *, n_tiles):
    m_rows, half = xn.shape[0] * SUBLANES, xn.shape[2]
    _, d, d_exp = w_gate.shape
    assert d == 2 * half
    p_rows = n_tiles * T_EXP
    any_spec = pl.BlockSpec(memory_space=pl.ANY)
    return pl.pallas_call(
        functools.partial(_experts_kernel, m_rows=m_rows),
        grid_spec=pltpu.PrefetchScalarGridSpec(
            num_scalar_prefetch=2,
            grid=(n_tiles,),
            in_specs=[any_spec, any_spec, any_spec, any_spec],
            out_specs=pl.BlockSpec((ROW_PAIR * T_EXP, half), lambda i, info, pos: (i, 0)),
            scratch_shapes=[
                pltpu.VMEM((2, T_EXP // SUBLANES, GROUP_ROWS, half), BF16),
                pltpu.VMEM((1, 2, d, d_exp), F32),
                pltpu.VMEM((2, d, d_exp), BF16),
                pltpu.VMEM((1, 1, d_exp, d), F32),
                pltpu.VMEM((1, d_exp, d), BF16),
                pltpu.SMEM((p_rows,), I32),
                pltpu.SMEM((1,), I32),
                pltpu.SemaphoreType.DMA((2,)),
                pltpu.SemaphoreType.DMA((1,)),
                pltpu.SemaphoreType.DMA((1,)),
            ],
        ),
        out_shape=jax.ShapeDtypeStruct((ROW_PAIR * p_rows, half), BF16),
        compiler_params=_params(1),
        name="experts",
    )(info_flat, pos_flat, xn, w_gate, w_up, w_down)


def _combine_kernel(pos_ref, x1_ref, w_ref, g_ref, y_ref, out_ref, ybuf_ref, sem, *, m_rows, row0, n_steps):
    i = pl.program_id(0)

    def gather_start(step):
        base = row0 + step * TR
        slot = step & 1

        def body(g, _):
            for j in range(SUBLANES):
                for k in range(TOP_K):
                    src = pos_ref[k * m_rows + base + g * SUBLANES + j]
                    pltpu.make_async_copy(_token_rows(y_ref, src), _slot_rows(ybuf_ref.at[slot, k], g, j),
                                          sem.at[slot]).start(priority=k % 2)
            return 0
        lax.fori_loop(0, TR // SUBLANES, body, 0)

    def gather_wait(slot):
        for k in range(TOP_K):
            pltpu.make_async_copy(y_ref.at[pl.ds(0, TR // SUBLANES)], ybuf_ref.at[slot, k], sem.at[slot]).wait()

    @pl.when(i == 0)
    def _():
        gather_start(0)

    @pl.when(i + 1 < n_steps)
    def _():
        gather_start(i + 1)

    slot = i & 1
    gather_wait(slot)
    w = w_ref[...]
    y0, y1 = [_unpack_halves(ybuf_ref[slot, k].reshape(ROW_PAIR * TR, ybuf_ref.shape[-1])) for k in range(TOP_K)]
    moe = jnp.concatenate([w[:, 0:1] * y0[k] + w[:, 1:2] * y1[k] for k in range(2)], axis=-1)
    out_ref[...] = _rmsnorm_rows(x1_ref[...] + moe, g_ref[...])


def _combine(pos_flat, x1, wts_t, norm_f, y, *, row0, n_rows):
    m_rows, d = x1.shape
    tile0 = row0 // TR
    n_steps = n_rows // TR
    return pl.pallas_call(
        functools.partial(_combine_kernel, m_rows=m_rows, row0=row0, n_steps=n_steps),
        grid_spec=pltpu.PrefetchScalarGridSpec(
            num_scalar_prefetch=1,
            grid=(n_steps,),
            in_specs=[
                pl.BlockSpec((TR, d), lambda i, pos: (tile0 + i, 0)),
                pl.BlockSpec((TR, TOP_K), lambda i, pos: (tile0 + i, 0)),
                pl.BlockSpec((1, d), lambda i, pos: (0, 0)),
                pl.BlockSpec(memory_space=pl.ANY),
            ],
            out_specs=pl.BlockSpec((TR, d), lambda i, pos: (i, 0)),
            scratch_shapes=[pltpu.VMEM((2, TOP_K, TR // SUBLANES, GROUP_ROWS, d // 2), BF16),
                            pltpu.SemaphoreType.DMA((2,))],
        ),
        out_shape=jax.ShapeDtypeStruct((n_rows, d), F32),
        compiler_params=_params(1),
        name="combine",
    )(pos_flat, x1, wts_t, norm_f, y)


def _sample_headers(state, k_width):
    n, _, w = state.shape
    padded = jnp.pad(state, ((0, 0), (SUBLANES - (k_width - 1), 0), (0, 0)))
    return padded.reshape(n * SUBLANES, w)


def kernel(x_prompt, x_sample, state_conv_a, state_conv_r, state_h, meta_tokens, norm1, w_in, conv_a_w, conv_r_w,
           conv_r_b, lru_wa, lru_ba, lru_wx, lru_bx, lru_lam, w_br_a, w_br_r, w_o, norm2, w_group, b_group,
           w_router, b_router, w_gate, w_up, w_down, norm_f):
    batch, seq, d = x_prompt.shape
    dec_batch, dec_seq, _ = x_sample.shape
    depth = norm1.shape[0]
    width = w_br_a.shape[1]
    n_groups = w_group.shape[-1]
    n_experts = w_router.shape[-1]
    n_meta = meta_tokens.shape[0]
    assert depth == 1, "meta rows are folded into an initial state, which only carries one layer"
    assert dec_seq == SUBLANES and n_meta % SUBLANES == 0 and n_meta >= SUBLANES
    assert seq % TM == 0 and (dec_batch * dec_seq) % TM == 0
    assert n_groups <= SUBLANES and n_experts == n_groups * SUBLANES
    assert width % CB == 0 and d % (2 * CB) == 0 and CB % lru_wa.shape[-1] == 0

    n_prompt_rows = batch * seq
    n_sample_rows = dec_batch * dec_seq
    m_rows = n_prompt_rows + n_sample_rows
    n_prompt_tiles = n_prompt_rows // TM
    n_tiles = m_rows // TM
    tiles_per_seq = seq // TM
    n_exp_tiles = (TOP_K * m_rows) // T_EXP + n_experts
    assert n_exp_tiles <= LANES

    xp = x_prompt.reshape(n_prompt_rows, d)
    xs = x_sample.reshape(n_sample_rows, d)
    row = lambda v: v.reshape(1, -1)

    u, um = _norm1(xp, xs, meta_tokens, row(norm1[0]), n_prompt_tiles, n_tiles)

    def mixer_geometry(tm):
        assert seq % tm == 0 and n_sample_rows % tm == 0 and tm % MIX_CHUNK == 0 and MIX_CHUNK % SUBLANES == 0
        return dict(tm=tm, width=width, n_seq=batch, tiles_per_seq=seq // tm,
                    n_prompt_tiles=n_prompt_rows // tm, n_tiles=m_rows // tm)

    out_a, tail_a_p, tail_a_s = _mixer_a(
        u, um, w_in[0], conv_a_w[0], _sample_headers(state_conv_a[0], K_A), **mixer_geometry(TM_MIX_A))
    out_r, tail_r_p, tail_r_s, h_p, h_s = _mixer_r(
        u, um, w_in[0], conv_r_w[0], row(conv_r_b[0]), lru_wa[0], row(lru_ba[0]), lru_wx[0], row(lru_bx[0]),
        row(lru_lam[0]), _sample_headers(state_conv_r[0], K_R), state_h[0], seg0=3, **mixer_geometry(TM_MIX_R))

    gates = _gates(u, w_in[0], col0=5 * width, n_cols=2 * d, tm=TM_OPROJ)
    merged = _merge(gates, out_a, out_r, w_br_a[0], w_br_r[0], tm=TM_OPROJ)
    x1 = _oproj(merged, xp, xs, w_o[0], tm=TM_OPROJ)

    wt = jnp.concatenate([jnp.pad(w_group[0].T, ((0, SUBLANES - n_groups), (0, 0))), w_router[0].T], axis=0)
    bt = jnp.concatenate([jnp.pad(b_group[0], (0, SUBLANES - n_groups)), b_router[0]]).reshape(-1, 1)
    xn, eid, wts = _router(x1, row(norm2[0]), wt, bt, n_groups=n_groups)

    pos, info = _positions(eid, n_experts=n_experts)
    pos_flat = pos.reshape(-1)
    info_flat = info.reshape(-1)
    y = _grouped(_experts(info_flat, pos_flat, _grouped(xn), w_gate[0], w_up[0], w_down[0], n_tiles=n_exp_tiles))

    wts_t = wts.T
    g_f = row(norm_f)
    y_prompt = _combine(pos_flat, x1, wts_t, g_f, y, row0=0, n_rows=n_prompt_rows)
    y_sample = _combine(pos_flat, x1, wts_t, g_f, y, row0=n_prompt_rows, n_rows=n_sample_rows)

    def sample_tail(t, k):
        return t.reshape(dec_batch, SUBLANES, width)[:, SUBLANES - k:, :][None]

    return (
        y_prompt.reshape(batch, seq, d),
        y_sample.reshape(dec_batch, dec_seq, d),
        tail_a_p[:, SUBLANES - (K_A - 1):, :][None],
        tail_r_p[:, SUBLANES - (K_R - 1):, :][None],
        h_p[:, SUBLANES - 1, :][None],
        sample_tail(tail_a_s, K_A - 1),
        sample_tail(tail_r_s, K_R - 1),
        sample_tail(h_s, 1)[:, :, 0, :],
    )
```

```python
import functools

import jax
import jax.numpy as jnp
from jax import lax
from jax.experimental import pallas as pl
from jax.experimental.pallas import tpu as pltpu

F32, BF16, I32 = jnp.float32, jnp.bfloat16, jnp.int32

EPS = 1e-6
C_RG = 8.0
K_A = 3
K_R = 4
LRU_HEADS = 16
TOP_K = 2

SUBLANES = 8
LANES = 128
VMEM_LIMIT_BYTES = 56 * 1024 * 1024

TM = 512
TM_OPROJ = 1024
TM_MIX_A = 512
TM_MIX_R = 1024
MIX_CHUNK = 256
CB = 256
T_EXP = 256
TR = 256


def _dot(a, b):
    return jnp.dot(a, b, preferred_element_type=F32)


ROW_PAIR = 2


def _pack_halves(x):
    half = x.shape[1] // 2
    words = pltpu.pack_elementwise([x[:, :half], x[:, half:]], packed_dtype=BF16)
    return pltpu.bitcast(words, BF16)


def _unpack_halves(pairs):
    words = pltpu.bitcast(pairs, jnp.uint32)
    return [pltpu.unpack_elementwise(words, index=k, packed_dtype=BF16, unpacked_dtype=F32) for k in range(2)]


GROUP_ROWS = ROW_PAIR * SUBLANES


def _grouped(x):
    return x.reshape(x.shape[0] // GROUP_ROWS, GROUP_ROWS, x.shape[1])


def _token_rows(ref, token):
    shift = SUBLANES.bit_length() - 1
    row = pl.multiple_of(ROW_PAIR * (token & (SUBLANES - 1)), ROW_PAIR)
    return ref.at[token >> shift, pl.ds(row, ROW_PAIR), :]


def _slot_rows(ref, group, j):
    return ref.at[group, pl.ds(ROW_PAIR * j, ROW_PAIR), :]


def _rmsnorm_rows(x, g):
    y = x * lax.rsqrt(jnp.mean(x * x, axis=-1, keepdims=True) + EPS)
    return y * g


def _params(n_axes):
    return pltpu.CompilerParams(dimension_semantics=("arbitrary",) * n_axes,
                                vmem_limit_bytes=VMEM_LIMIT_BYTES)


def _norm1_kernel(xp_ref, xs_ref, meta_ref, g_ref, u_ref, um_ref, *, n_prompt_tiles):
    i = pl.program_id(0)
    g = g_ref[...]

    @pl.when(i < n_prompt_tiles)
    def _():
        u_ref[...] = _rmsnorm_rows(xp_ref[...], g).astype(BF16)

    @pl.when(i >= n_prompt_tiles)
    def _():
        u_ref[...] = _rmsnorm_rows(xs_ref[...], g).astype(BF16)

    @pl.when(i == 0)
    def _():
        um_ref[...] = _rmsnorm_rows(meta_ref[...], g).astype(BF16)


def _norm1(xp, xs, meta, g, n_prompt_tiles, n_tiles):
    d = xp.shape[1]
    return pl.pallas_call(
        functools.partial(_norm1_kernel, n_prompt_tiles=n_prompt_tiles),
        grid=(n_tiles,),
        in_specs=[
            pl.BlockSpec((TM, d), lambda i: (jnp.minimum(i, n_prompt_tiles - 1), 0)),
            pl.BlockSpec((TM, d), lambda i: (jnp.maximum(i - n_prompt_tiles, 0), 0)),
            pl.BlockSpec(meta.shape, lambda i: (0, 0)),
            pl.BlockSpec((1, d), lambda i: (0, 0)),
        ],
        out_specs=[
            pl.BlockSpec((TM, d), lambda i: (i, 0)),
            pl.BlockSpec(meta.shape, lambda i: (0, 0)),
        ],
        out_shape=[
            jax.ShapeDtypeStruct((n_tiles * TM, d), BF16),
            jax.ShapeDtypeStruct(meta.shape, BF16),
        ],
        compiler_params=_params(1),
        name="norm1",
    )(xp, xs, meta, g)


def _conv_taps(window, cw, k_width):
    acc = window(0) * cw[0:1, :]
    for k in range(1, k_width):
        acc = acc + window(k) * cw[k:k + 1, :]
    return acc


def _conv_flat(ext_ref, values, cw, k_width, r0):
    rows = values.shape[0]
    ext_ref[SUBLANES + r0:SUBLANES + r0 + rows, :] = values
    base = SUBLANES - (k_width - 1) + r0
    return _conv_taps(lambda k: ext_ref[pl.ds(base + k, rows), :], cw, k_width)


def _conv_grouped(ext3_ref, values, header, cw, k_width, s0):
    rows, cols = values.shape
    n_seq = rows // SUBLANES
    ext3_ref[s0:s0 + n_seq, 0:SUBLANES, :] = header.reshape(n_seq, SUBLANES, cols)
    ext3_ref[s0:s0 + n_seq, SUBLANES:, :] = values.reshape(n_seq, SUBLANES, cols)
    base = SUBLANES - (k_width - 1)
    return _conv_taps(
        lambda k: ext3_ref[s0:s0 + n_seq, base + k:base + k + SUBLANES, :].reshape(rows, cols), cw, k_width)


def _mixer_a_kernel(u_ref, um_ref, wb_ref, wc_ref, wv_ref, cw_ref, hdr_ref,
                    oa_ref, tailp_ref, tails_ref,
                    wbf_ref, ext_ref, ext3_ref, carry_ref, mhdr_ref,
                    *, tm, tiles_per_seq, n_prompt_tiles, n_meta):
    m = pl.program_id(1)
    cw = cw_ref[...]
    ch = MIX_CHUNK
    n_chunks = tm // ch

    @pl.when(m == 0)
    def _():
        wbf_ref[0] = wb_ref[...].astype(BF16)
        wbf_ref[1] = wc_ref[...].astype(BF16)
        wbf_ref[2] = wv_ref[...].astype(BF16)
        um = um_ref[...]
        cv_meta = _dot(um, wbf_ref[1]) * _dot(um, wbf_ref[2])
        mhdr_ref[...] = cv_meta[n_meta - SUBLANES:, :]

    def products(r0):
        u = u_ref[r0:r0 + ch, :]
        return _dot(u, wbf_ref[0]), _dot(u, wbf_ref[1]) * _dot(u, wbf_ref[2])

    @pl.when(m < n_prompt_tiles)
    def _():
        @pl.when((m % tiles_per_seq) == 0)
        def _():
            carry_ref[...] = mhdr_ref[...]

        ext_ref[0:SUBLANES, :] = carry_ref[...]
        for c in range(n_chunks):
            r0 = c * ch
            zb, cv = products(r0)
            conv = _conv_flat(ext_ref, cv, cw, K_A, r0)
            oa_ref[r0:r0 + ch, :] = (zb * conv).astype(BF16)
        carry_ref[...] = ext_ref[tm:, :]
        tailp_ref[0] = ext_ref[tm:, :]

    @pl.when(m >= n_prompt_tiles)
    def _():
        for c in range(n_chunks):
            r0 = c * ch
            zb, cv = products(r0)
            conv = _conv_grouped(ext3_ref, cv, hdr_ref[r0:r0 + ch, :], cw, K_A, r0 // SUBLANES)
            oa_ref[r0:r0 + ch, :] = (zb * conv).astype(BF16)
            tails_ref[r0:r0 + ch, :] = cv


def _mixer_a(u, um, w_in, conv_w, hdr_s, *, tm, width, n_seq, tiles_per_seq, n_prompt_tiles, n_tiles):
    d = u.shape[1]
    nb = width // CB
    n_meta = um.shape[0]
    last_seq = n_seq - 1

    def wspec(seg):
        return pl.BlockSpec((d, CB), lambda j, m, seg=seg: (0, seg * nb + j))

    return pl.pallas_call(
        functools.partial(_mixer_a_kernel, tm=tm, tiles_per_seq=tiles_per_seq,
                          n_prompt_tiles=n_prompt_tiles, n_meta=n_meta),
        grid=(nb, n_tiles),
        in_specs=[
            pl.BlockSpec((tm, d), lambda j, m: (m, 0)),
            pl.BlockSpec(um.shape, lambda j, m: (0, 0)),
            wspec(0), wspec(1), wspec(2),
            pl.BlockSpec((K_A, CB), lambda j, m: (0, j)),
            pl.BlockSpec((tm, CB), lambda j, m: (jnp.maximum(m - n_prompt_tiles, 0), j)),
        ],
        out_specs=[
            pl.BlockSpec((tm, CB), lambda j, m: (m, j)),
            pl.BlockSpec((1, SUBLANES, CB),
                         lambda j, m: (jnp.minimum(m // tiles_per_seq, last_seq), 0, j)),
            pl.BlockSpec((tm, CB), lambda j, m: (jnp.maximum(m - n_prompt_tiles, 0), j)),
        ],
        out_shape=[
            jax.ShapeDtypeStruct((n_tiles * tm, width), BF16),
            jax.ShapeDtypeStruct((n_seq, SUBLANES, width), F32),
            jax.ShapeDtypeStruct(hdr_s.shape, F32),
        ],
        scratch_shapes=[
            pltpu.VMEM((3, d, CB), BF16),
            pltpu.VMEM((tm + SUBLANES, CB), F32),
            pltpu.VMEM((tm // SUBLANES, 2 * SUBLANES, CB), F32),
            pltpu.VMEM((SUBLANES, CB), F32),
            pltpu.VMEM((SUBLANES, CB), F32),
        ],
        compiler_params=_params(2),
        name="mixer_a",
    )(u, um, w_in, w_in, w_in, conv_w, hdr_s)


def _softplus(x):
    return jnp.maximum(x, 0.0) + jnp.log1p(jnp.exp(-jnp.abs(x)))


def _lru_kernel(u_ref, um_ref, wx_ref, wy_ref, cw_ref, cbias_ref, wa_ref, ba_ref, wi_ref, bi_ref, lam_ref,
                hdr_ref, h0_ref,
                or_ref, tailp_ref, tails_ref, hp_ref, hs_ref,
                wbf_ref, gbf_ref, ext_ref, ext3_ref, carry_ref, hcarry_ref,
                mhdr_ref, mh_ref,
                *, tm, tiles_per_seq, n_prompt_tiles, n_meta, lru_block):
    m = pl.program_id(1)
    cw = cw_ref[...]
    cbias = cbias_ref[...]
    heads = CB // lru_block
    ch = MIX_CHUNK
    n_chunks = tm // ch
    row8 = lax.broadcasted_iota(I32, (SUBLANES, CB), 0)

    def gate_terms(xc):
        xb = xc.astype(BF16)
        ra, ri = [], []
        for hh in range(heads):
            xh = xb[:, hh * lru_block:(hh + 1) * lru_block]
            ra.append(_dot(xh, gbf_ref[0, hh]))
            ri.append(_dot(xh, gbf_ref[1, hh]))
        r = jax.nn.sigmoid(jnp.concatenate(ra, axis=-1) + ba_ref[...])
        i = jax.nn.sigmoid(jnp.concatenate(ri, axis=-1) + bi_ref[...])
        log_a = -C_RG * r * _softplus(-lam_ref[...])
        a = jnp.exp(log_a)
        b = jnp.sqrt((1.0 + a * a) * jnp.tanh(-log_a)) * i * xc
        return a, b

    def scan(a, b, h_prev, h0_row=None):
        out = []
        for g in range(a.shape[0] // SUBLANES):
            a8 = a[g * SUBLANES:(g + 1) * SUBLANES, :]
            b8 = b[g * SUBLANES:(g + 1) * SUBLANES, :]
            for dist in (1, 2, 4):
                keep = row8 >= dist
                b8 = jnp.where(keep, a8 * pltpu.roll(b8, dist, 0) + b8, b8)
                a8 = jnp.where(keep, a8 * pltpu.roll(a8, dist, 0), a8)
            h8 = a8 * (h_prev if h0_row is None else h0_row(g)) + b8
            out.append(h8)
            h_prev = h8[SUBLANES - 1:SUBLANES, :]
        return jnp.concatenate(out, axis=0), h_prev

    @pl.when(m == 0)
    def _():
        wbf_ref[0] = wx_ref[...].astype(BF16)
        wbf_ref[1] = wy_ref[...].astype(BF16)
        gbf_ref[0] = wa_ref[...].astype(BF16)
        gbf_ref[1] = wi_ref[...].astype(BF16)
        zx_meta = _dot(um_ref[...], wbf_ref[0])
        ext_ref[0:SUBLANES, :] = jnp.zeros((SUBLANES, CB), F32)
        a, b = gate_terms(_conv_flat(ext_ref, zx_meta, cw, K_R, 0) + cbias)
        _, h_meta = scan(a, b, jnp.zeros((1, CB), F32))
        mh_ref[...] = jnp.broadcast_to(h_meta, (SUBLANES, CB))
        mhdr_ref[...] = zx_meta[n_meta - SUBLANES:, :]

    def products(r0):
        u = u_ref[r0:r0 + ch, :]
        return _dot(u, wbf_ref[0]), _dot(u, wbf_ref[1])

    @pl.when(m < n_prompt_tiles)
    def _():
        @pl.when((m % tiles_per_seq) == 0)
        def _():
            carry_ref[...] = mhdr_ref[...]
            hcarry_ref[...] = mh_ref[...]

        ext_ref[0:SUBLANES, :] = carry_ref[...]
        h_prev = hcarry_ref[0:1, :]
        for c in range(n_chunks):
            r0 = c * ch
            zx, zy = products(r0)
            a, b = gate_terms(_conv_flat(ext_ref, zx, cw, K_R, r0) + cbias)
            h, h_prev = scan(a, b, h_prev)
            or_ref[r0:r0 + ch, :] = (h * jax.nn.gelu(zy)).astype(BF16)
        hcarry_ref[0:1, :] = h_prev
        carry_ref[...] = ext_ref[tm:, :]
        tailp_ref[0] = ext_ref[tm:, :]
        hp_ref[0] = h[ch - SUBLANES:, :]

    @pl.when(m >= n_prompt_tiles)
    def _():
        for c in range(n_chunks):
            r0 = c * ch
            s0 = r0 // SUBLANES
            zx, zy = products(r0)
            xc = _conv_grouped(ext3_ref, zx, hdr_ref[r0:r0 + ch, :], cw, K_R, s0) + cbias
            a, b = gate_terms(xc)
            h, _ = scan(a, b, None, lambda g, s0=s0: h0_ref[s0 + g:s0 + g + 1, :])
            or_ref[r0:r0 + ch, :] = (h * jax.nn.gelu(zy)).astype(BF16)
            tails_ref[r0:r0 + ch, :] = zx
            hs_ref[r0:r0 + ch, :] = h


def _mixer_r(u, um, w_in, conv_w, conv_b, lru_wa, lru_ba, lru_wx, lru_bx, lru_lam, hdr_s, h0_s,
             *, tm, width, seg0, n_seq, tiles_per_seq, n_prompt_tiles, n_tiles):
    d = u.shape[1]
    nb = width // CB
    n_meta = um.shape[0]
    last_seq = n_seq - 1
    lru_block = lru_wa.shape[-1]
    heads = CB // lru_block
    seqs_per_tile = tm // SUBLANES

    def wspec(seg):
        return pl.BlockSpec((d, CB), lambda j, m, seg=seg: (0, seg * nb + j))

    def vec():
        return pl.BlockSpec((1, CB), lambda j, m: (0, j))

    def gspec():
        return pl.BlockSpec((heads, lru_block, lru_block), lambda j, m: (j, 0, 0))

    def sample_rows():
        return pl.BlockSpec((tm, CB), lambda j, m: (jnp.maximum(m - n_prompt_tiles, 0), j))

    def seq_tail():
        return pl.BlockSpec((1, SUBLANES, CB),
                            lambda j, m: (jnp.minimum(m // tiles_per_seq, last_seq), 0, j))

    return pl.pallas_call(
        functools.partial(_lru_kernel, tm=tm, tiles_per_seq=tiles_per_seq, n_prompt_tiles=n_prompt_tiles,
                          n_meta=n_meta, lru_block=lru_block),
        grid=(nb, n_tiles),
        in_specs=[
            pl.BlockSpec((tm, d), lambda j, m: (m, 0)),
            pl.BlockSpec(um.shape, lambda j, m: (0, 0)),
            wspec(seg0), wspec(seg0 + 1),
            pl.BlockSpec((K_R, CB), lambda j, m: (0, j)),
            vec(),
            gspec(), vec(), gspec(), vec(), vec(),
            sample_rows(),
            pl.BlockSpec((seqs_per_tile, CB), lambda j, m: (jnp.maximum(m - n_prompt_tiles, 0), j)),
        ],
        out_specs=[
            pl.BlockSpec((tm, CB), lambda j, m: (m, j)),
            seq_tail(), sample_rows(), seq_tail(), sample_rows(),
        ],
        out_shape=[
            jax.ShapeDtypeStruct((n_tiles * tm, width), BF16),
            jax.ShapeDtypeStruct((n_seq, SUBLANES, width), F32),
            jax.ShapeDtypeStruct(hdr_s.shape, F32),
            jax.ShapeDtypeStruct((n_seq, SUBLANES, width), F32),
            jax.ShapeDtypeStruct(hdr_s.shape, F32),
        ],
        scratch_shapes=[
            pltpu.VMEM((2, d, CB), BF16),
            pltpu.VMEM((2, heads, lru_block, lru_block), BF16),
            pltpu.VMEM((tm + SUBLANES, CB), F32),
            pltpu.VMEM((seqs_per_tile, 2 * SUBLANES, CB), F32),
            pltpu.VMEM((SUBLANES, CB), F32),
            pltpu.VMEM((SUBLANES, CB), F32),
            pltpu.VMEM((SUBLANES, CB), F32),
            pltpu.VMEM((SUBLANES, CB), F32),
        ],
        compiler_params=_params(2),
        name="mixer_r",
    )(u, um, w_in, w_in, conv_w, conv_b, lru_wa, lru_ba, lru_wx, lru_bx, lru_lam, hdr_s, h0_s)


def _gates_kernel(u_ref, w_ref, g_ref, wbf_ref):
    @pl.when(pl.program_id(1) == 0)
    def _():
        wbf_ref[...] = w_ref[...].astype(BF16)

    for r0 in range(0, u_ref.shape[0], MIX_CHUNK):
        rows = slice(r0, r0 + MIX_CHUNK)
        g_ref[rows, :] = jax.nn.sigmoid(_dot(u_ref[rows, :], wbf_ref[...]))


def _gates(u, w_in, *, col0, n_cols, tm):
    m_rows, d = u.shape
    cb = 2 * CB
    return pl.pallas_call(
        _gates_kernel,
        grid=(n_cols // cb, m_rows // tm),
        in_specs=[
            pl.BlockSpec((tm, d), lambda j, m: (m, 0)),
            pl.BlockSpec((d, cb), lambda j, m: (0, col0 // cb + j)),
        ],
        out_specs=pl.BlockSpec((tm, cb), lambda j, m: (m, j)),
        out_shape=jax.ShapeDtypeStruct((m_rows, n_cols), F32),
        scratch_shapes=[pltpu.VMEM((d, cb), BF16)],
        compiler_params=_params(2),
        name="gates",
    )(u, w_in)


def _merge_kernel(ga_ref, gr_ref, oa_ref, or_ref, wba_ref, wbr_ref, mg_ref, wb_bf):
    @pl.when(pl.program_id(1) == 0)
    def _():
        wb_bf[0] = wba_ref[...].astype(BF16)
        wb_bf[1] = wbr_ref[...].astype(BF16)

    for r0 in range(0, mg_ref.shape[0], MIX_CHUNK):
        rows = slice(r0, r0 + MIX_CHUNK)
        mg = ga_ref[rows, :] * _dot(oa_ref[rows, :], wb_bf[0]) + gr_ref[rows, :] * _dot(or_ref[rows, :], wb_bf[1])
        mg_ref[rows, :] = mg.astype(BF16)


def _merge(gates, out_a, out_r, w_br_a, w_br_r, *, tm):
    m_rows, width = out_a.shape
    d = w_br_a.shape[1]
    cb = 2 * CB
    nb = d // cb
    return pl.pallas_call(
        _merge_kernel,
        grid=(nb, m_rows // tm),
        in_specs=[
            pl.BlockSpec((tm, cb), lambda j, m: (m, j)),
            pl.BlockSpec((tm, cb), lambda j, m: (m, nb + j)),
            pl.BlockSpec((tm, width), lambda j, m: (m, 0)),
            pl.BlockSpec((tm, width), lambda j, m: (m, 0)),
            pl.BlockSpec((width, cb), lambda j, m: (0, j)),
            pl.BlockSpec((width, cb), lambda j, m: (0, j)),
        ],
        out_specs=pl.BlockSpec((tm, cb), lambda j, m: (m, j)),
        out_shape=jax.ShapeDtypeStruct((m_rows, d), BF16),
        scratch_shapes=[pltpu.VMEM((2, width, cb), BF16)],
        compiler_params=_params(2),
        name="merge",
    )(gates, gates, out_a, out_r, w_br_a, w_br_r)


def _oproj_kernel(mg_ref, xp_ref, xs_ref, wo_ref, x1_ref, wbf_ref, *, n_prompt_tiles):
    m = pl.program_id(1)

    @pl.when(m == 0)
    def _():
        wbf_ref[...] = wo_ref[...].astype(BF16)

    def project(x_ref):
        for r0 in range(0, mg_ref.shape[0], MIX_CHUNK):
            rows = slice(r0, r0 + MIX_CHUNK)
            x1_ref[rows, :] = x_ref[rows, :] + _dot(mg_ref[rows, :], wbf_ref[...])

    @pl.when(m < n_prompt_tiles)
    def _():
        project(xp_ref)

    @pl.when(m >= n_prompt_tiles)
    def _():
        project(xs_ref)


def _oproj(mg, xp, xs, w_o, *, tm):
    m_rows, d = mg.shape
    cb = 2 * CB
    nb = d // cb
    n_prompt_tiles = xp.shape[0] // tm
    n_tiles = m_rows // tm
    assert xp.shape[0] % tm == 0 and xs.shape[0] % tm == 0
    return pl.pallas_call(
        functools.partial(_oproj_kernel, n_prompt_tiles=n_prompt_tiles),
        grid=(nb, n_tiles),
        in_specs=[
            pl.BlockSpec((tm, d), lambda j, m: (m, 0)),
            pl.BlockSpec((tm, cb), lambda j, m: (jnp.minimum(m, n_prompt_tiles - 1), j)),
            pl.BlockSpec((tm, cb), lambda j, m: (jnp.maximum(m - n_prompt_tiles, 0), j)),
            pl.BlockSpec((d, cb), lambda j, m: (0, j)),
        ],
        out_specs=pl.BlockSpec((tm, cb), lambda j, m: (m, j)),
        out_shape=jax.ShapeDtypeStruct((m_rows, d), F32),
        scratch_shapes=[pltpu.VMEM((d, cb), BF16)],
        compiler_params=_params(2),
        name="oproj",
    )(mg, xp, xs, w_o)


def _router_kernel(x1_ref, g_ref, wt_ref, bt_ref, xn_ref, eid_ref, wts_ref, *, n_groups):
    xn = _rmsnorm_rows(x1_ref[...], g_ref[...])
    xn_ref[...] = _pack_halves(xn)
    lg = lax.dot_general(wt_ref[...].astype(BF16), xn.astype(BF16), (((1,), (1,)), ((), ())),
                         preferred_element_type=F32) + bt_ref[...]
    rows = xn.shape[0]
    row8 = lax.broadcasted_iota(I32, (SUBLANES, rows), 0).astype(F32)

    def first_index_of(v, vmax):
        return jnp.min(jnp.where(v == vmax, row8, float(SUBLANES)), axis=0, keepdims=True)

    gl = jnp.where(row8 < float(n_groups), lg[0:SUBLANES, :], -jnp.inf)
    ge = jnp.exp(gl - jnp.max(gl, axis=0, keepdims=True))
    gprob = ge / jnp.sum(ge, axis=0, keepdims=True)
    g_p = jnp.max(gprob, axis=0, keepdims=True)
    g_idx = first_index_of(gprob, g_p)

    esel = jnp.zeros((SUBLANES, rows), F32)
    for k in range(n_groups):
        esel = jnp.where(g_idx == float(k), lg[(k + 1) * SUBLANES:(k + 2) * SUBLANES, :], esel)
    ee = jnp.exp(esel - jnp.max(esel, axis=0, keepdims=True))
    ep = ee / jnp.sum(ee, axis=0, keepdims=True)
    p1 = jnp.max(ep, axis=0, keepdims=True)
    i1 = first_index_of(ep, p1)
    ep_rest = jnp.where(row8 == i1, -1.0, ep)
    p2 = jnp.max(ep_rest, axis=0, keepdims=True)
    i2 = first_index_of(ep_rest, p2)

    den = p1 + p2
    wts_ref[0:1, :] = p1 / den * g_p
    wts_ref[1:2, :] = p2 / den * g_p
    eid_ref[0:1, :] = (g_idx * float(SUBLANES) + i1).astype(I32)
    eid_ref[1:2, :] = (g_idx * float(SUBLANES) + i2).astype(I32)


def _router(x1, g2, wt, bt, *, n_groups):
    m_rows, d = x1.shape
    return pl.pallas_call(
        functools.partial(_router_kernel, n_groups=n_groups),
        grid=(m_rows // TR,),
        in_specs=[
            pl.BlockSpec((TR, d), lambda i: (i, 0)),
            pl.BlockSpec((1, d), lambda i: (0, 0)),
            pl.BlockSpec(wt.shape, lambda i: (0, 0)),
            pl.BlockSpec(bt.shape, lambda i: (0, 0)),
        ],
        out_specs=[
            pl.BlockSpec((ROW_PAIR * TR, d // 2), lambda i: (i, 0)),
            pl.BlockSpec((TOP_K, TR), lambda i: (0, i)),
            pl.BlockSpec((TOP_K, TR), lambda i: (0, i)),
        ],
        out_shape=[
            jax.ShapeDtypeStruct((ROW_PAIR * m_rows, d // 2), BF16),
            jax.ShapeDtypeStruct((TOP_K, m_rows), I32),
            jax.ShapeDtypeStruct((TOP_K, m_rows), F32),
        ],
        compiler_params=_params(1),
        name="router",
    )(x1, g2, wt, bt)


INFO_TILE_EXPERT, INFO_NUM_TILES, INFO_VALID_END, INFO_END_TILE, INFO_ROWS = 0, 1, 2, 3, SUBLANES


def _positions_kernel(eid_ref, pos_ref, info_ref, exc_ref, *, n_experts, tile_shift):
    m_rows = eid_ref.shape[1]
    chunk = 2 * LANES
    n_chunks = m_rows // chunk
    tile = 1 << tile_shift
    e_col = lax.broadcasted_iota(I32, (n_experts, chunk), 0)
    upper = (lax.broadcasted_iota(I32, (chunk, chunk), 0)
             < lax.broadcasted_iota(I32, (chunk, chunk), 1)).astype(BF16)

    def onehots(c):
        off = pl.multiple_of(c * chunk, chunk)
        ids = eid_ref[:, pl.ds(off, chunk)]
        return off, [(e_col == ids[k:k + 1, :]) for k in range(TOP_K)]

    def count(c, carry):
        off, hot = onehots(c)
        used = jnp.where(hot[0] | hot[1], 1.0, 0.0)
        exc_ref[:, pl.ds(off, chunk)] = _dot(used.astype(BF16), upper) + carry
        return carry + jnp.sum(used, axis=1, keepdims=True)

    counts = lax.fori_loop(0, n_chunks, count, jnp.zeros((n_experts, 1), F32))

    lane = lax.broadcasted_iota(I32, (n_experts, LANES), 1)
    sub = lax.broadcasted_iota(I32, (n_experts, LANES), 0)

    def to_lanes(col):
        return jnp.sum(jnp.where(lane == sub, col, 0.0), axis=0, keepdims=True)

    n_tiles_e = ((counts.astype(I32) + (tile - 1)) >> tile_shift).astype(F32)
    first_tile = jnp.sum(jnp.where(lane < sub, to_lanes(n_tiles_e), 0.0), axis=1, keepdims=True)
    end_tile = first_tile + n_tiles_e
    first_row = first_tile * float(tile)

    def place(c, _):
        off, hot = onehots(c)
        dest = exc_ref[:, pl.ds(off, chunk)] + first_row
        for k in range(TOP_K):
            pos_ref[k:k + 1, pl.ds(off, chunk)] = jnp.sum(
                jnp.where(hot[k], dest, 0.0), axis=0, keepdims=True).astype(I32)
        return 0

    lax.fori_loop(0, n_chunks, place, 0)

    tile_expert = jnp.sum(jnp.where(end_tile <= lane.astype(F32), 1.0, 0.0), axis=0, keepdims=True)
    num_tiles = jnp.max(end_tile, axis=0, keepdims=True)
    valid_end = to_lanes(first_row + counts)
    info_ref[...] = jnp.zeros(info_ref.shape, I32)
    info_ref[INFO_TILE_EXPERT:INFO_TILE_EXPERT + 1, :] = jnp.minimum(
        tile_expert, float(n_experts - 1)).astype(I32)
    info_ref[INFO_NUM_TILES:INFO_NUM_TILES + 1, :] = jnp.broadcast_to(num_tiles, (1, LANES)).astype(I32)
    info_ref[INFO_VALID_END:INFO_VALID_END + 1, :] = valid_end.astype(I32)
    info_ref[INFO_END_TILE:INFO_END_TILE + 1, :] = to_lanes(end_tile).astype(I32)


def _positions(eid, *, n_experts):
    m_rows = eid.shape[1]
    tile_shift = T_EXP.bit_length() - 1
    return pl.pallas_call(
        functools.partial(_positions_kernel, n_experts=n_experts, tile_shift=tile_shift),
        out_shape=[
            jax.ShapeDtypeStruct((TOP_K, m_rows), I32),
            jax.ShapeDtypeStruct((INFO_ROWS, LANES), I32),
        ],
        scratch_shapes=[pltpu.VMEM((n_experts, m_rows), F32)],
        compiler_params=pltpu.CompilerParams(vmem_limit_bytes=VMEM_LIMIT_BYTES),
        name="positions",
    )(eid)


def _info(info_ref, row, lane):
    return info_ref[row * LANES + lane]


def _tile_state(i, info_ref):
    n_used = _info(info_ref, INFO_NUM_TILES, 0)
    ic = jnp.minimum(i, n_used - 1)
    e = _info(info_ref, INFO_TILE_EXPERT, ic)
    prev = _info(info_ref, INFO_TILE_EXPERT, jnp.maximum(ic - 1, 0))
    first = jnp.logical_or(ic == 0, e != prev)
    n_valid = jnp.minimum(_info(info_ref, INFO_VALID_END, e) - ic * T_EXP, T_EXP)
    return i < n_used, ic, e, first, n_valid


def _stream_expert_weights(i, info_ref, e, first, w_refs, wf_ref, wbf_ref, wsem, slot_ref, first_step=None):
    n_used = _info(info_ref, INFO_NUM_TILES, 0)
    n_slots, _, rows, _ = wf_ref.shape
    chunk = min(rows, 512)

    def copies(expert, slot):
        return [pltpu.make_async_copy(w.at[expert], wf_ref.at[slot, j], wsem.at[slot])
                for j, w in enumerate(w_refs)]

    @pl.when(i == 0)
    def _():
        slot_ref[0] = 0
        for cp in copies(e, 0):
            cp.start(priority=1)
        if first_step is not None:
            first_step()

    if n_slots > 1:
        @pl.when(jnp.logical_and(first, i > 0))
        def _():
            slot_ref[0] = 1 - slot_ref[0]

    @pl.when(first)
    def _():
        slot = slot_ref[0] if n_slots > 1 else 0
        for cp in copies(e, slot):
            cp.wait()
        for j in range(len(w_refs)):
            def cast(c, _, j=j):
                r0 = pl.multiple_of(c * chunk, chunk)
                wbf_ref[j, pl.ds(r0, chunk), :] = wf_ref[slot, j, pl.ds(r0, chunk), :].astype(BF16)
                return 0
            lax.fori_loop(0, rows // chunk, cast, 0)
        next_tile = _info(info_ref, INFO_END_TILE, e)

        @pl.when(next_tile < n_used)
        def _():
            for cp in copies(_info(info_ref, INFO_TILE_EXPERT, next_tile), 1 - slot if n_slots > 1 else 0):
                cp.start(priority=1)


def _experts_kernel(info_ref, pos_ref, xn_ref, wg_ref, wu_ref, wd_ref, y_ref,
                    xbuf_ref, wf_ref, wbf_ref, wdf_ref, wdbf_ref, inv_ref, slot_ref, xsem, wsem, wdsem, *, m_rows):
    i = pl.program_id(0)
    used, _, e, first, n_valid = _tile_state(i, info_ref)
    n_used = _info(info_ref, INFO_NUM_TILES, 0)

    def gather(tile, slot, n_rows, start):
        n_groups = (n_rows + (SUBLANES - 1)) >> (SUBLANES.bit_length() - 1)
        if start:
            first_row = tile * T_EXP
            last_row = first_row + n_rows - 1

            def body(g, _):
                for j in range(SUBLANES):
                    r = g * SUBLANES + j
                    src = inv_ref[jnp.minimum(first_row + r, last_row)]
                    pltpu.make_async_copy(_token_rows(xn_ref, src), _slot_rows(xbuf_ref.at[slot], g, j),
                                          xsem.at[slot]).start()
                return 0
            lax.fori_loop(0, n_groups, body, 0)
        else:
            for bit in range((T_EXP // SUBLANES).bit_length()):
                n = 1 << bit

                @pl.when((n_groups & n) != 0)
                def _():
                    pltpu.make_async_copy(xn_ref.at[pl.ds(0, n)], xbuf_ref.at[slot, pl.ds(0, n)],
                                          xsem.at[slot]).wait()

    def first_step():
        def invert(t, _):
            for k in range(TOP_K):
                inv_ref[pos_ref[k * m_rows + t]] = t
            return 0
        lax.fori_loop(0, m_rows, invert, 0, unroll=8)
        xbuf_ref[...] = jnp.zeros(xbuf_ref.shape, BF16)
        gather(0, 0, n_valid, True)

    @pl.when(used)
    def _():
        _stream_expert_weights(i, info_ref, e, first, (wg_ref, wu_ref), wf_ref, wbf_ref, wsem, slot_ref,
                               first_step)
        _stream_expert_weights(i, info_ref, e, first, (wd_ref,), wdf_ref, wdbf_ref, wdsem, slot_ref)
        slot = i & 1
        gather(i, slot, n_valid, False)

        @pl.when(i + 1 < n_used)
        def _():
            gather(i + 1, 1 - slot, _tile_state(i + 1, info_ref)[4], True)

        pairs = xbuf_ref[slot].reshape(ROW_PAIR * T_EXP, xbuf_ref.shape[-1])
        x_lo, x_hi = [x.astype(BF16) for x in _unpack_halves(pairs)]
        half = x_lo.shape[1]

        def proj(j):
            return _dot(x_lo, wbf_ref[j, 0:half, :]) + _dot(x_hi, wbf_ref[j, half:, :])

        h = (jax.nn.silu(proj(0)) * proj(1)).astype(BF16)
        y_ref[...] = _pack_halves(_dot(h, wdbf_ref[0]))

    @pl.when(jnp.logical_not(used))
    def _():
        y_ref[...] = jnp.zeros(y_ref.shape, BF16)


def _experts(info_flat, pos_flat, xn, w_gate, w_up, w_down, *, n_tiles):
    m_rows, half = xn.shape[0] * SUBLANES, xn.shape[2]
    _, d, d_exp = w_gate.shape
    assert d == 2 * half
    p_rows = n_tiles * T_EXP
    any_spec = pl.BlockSpec(memory_space=pl.ANY)
    return pl.pallas_call(
        functools.partial(_experts_kernel, m_rows=m_rows),
        grid_spec=pltpu.PrefetchScalarGridSpec(
            num_scalar_prefetch=2,
            grid=(n_tiles,),
            in_specs=[any_spec, any_spec, any_spec, any_spec],
            out_specs=pl.BlockSpec((ROW_PAIR * T_EXP, half), lambda i, info, pos: (i, 0)),
            scratch_shapes=[
                pltpu.VMEM((2, T_EXP // SUBLANES, GROUP_ROWS, half), BF16),
                pltpu.VMEM((1, 2, d, d_exp), F32),
                pltpu.VMEM((2, d, d_exp), BF16),
                pltpu.VMEM((1, 1, d_exp, d), F32),
                pltpu.VMEM((1, d_exp, d), BF16),
                pltpu.SMEM((p_rows,), I32),
                pltpu.SMEM((1,), I32),
                pltpu.SemaphoreType.DMA((2,)),
                pltpu.SemaphoreType.DMA((1,)),
                pltpu.SemaphoreType.DMA((1,)),
            ],
        ),
        out_shape=jax.ShapeDtypeStruct((ROW_PAIR * p_rows, half), BF16),
        compiler_params=_params(1),
        name="experts",
    )(info_flat, pos_flat, xn, w_gate, w_up, w_down)


def _combine_kernel(pos_ref, x1_ref, w_ref, g_ref, y_ref, out_ref, ybuf_ref, sem, *, m_rows, row0, n_steps):
    i = pl.program_id(0)

    def gather_start(step):
        base = row0 + step * TR
        slot = step & 1

        def body(g, _):
            for j in range(SUBLANES):
                for k in range(TOP_K):
                    src = pos_ref[k * m_rows + base + g * SUBLANES + j]
                    pltpu.make_async_copy(_token_rows(y_ref, src), _slot_rows(ybuf_ref.at[slot, k], g, j),
                                          sem.at[slot]).start(priority=k % 2)
            return 0
        lax.fori_loop(0, TR // SUBLANES, body, 0)

    def gather_wait(slot):
        for k in range(TOP_K):
            pltpu.make_async_copy(y_ref.at[pl.ds(0, TR // SUBLANES)], ybuf_ref.at[slot, k], sem.at[slot]).wait()

    @pl.when(i == 0)
    def _():
        gather_start(0)

    @pl.when(i + 1 < n_steps)
    def _():
        gather_start(i + 1)

    slot = i & 1
    gather_wait(slot)
    w = w_ref[...]
    y0, y1 = [_unpack_halves(ybuf_ref[slot, k].reshape(ROW_PAIR * TR, ybuf_ref.shape[-1])) for k in range(TOP_K)]
    moe = jnp.concatenate([w[:, 0:1] * y0[k] + w[:, 1:2] * y1[k] for k in range(2)], axis=-1)
    out_ref[...] = _rmsnorm_rows(x1_ref[...] + moe, g_ref[...])


def _combine(pos_flat, x1, wts_t, norm_f, y, *, row0, n_rows):
    m_rows, d = x1.shape
    tile0 = row0 // TR
    n_steps = n_rows // TR
    return pl.pallas_call(
        functools.partial(_combine_kernel, m_rows=m_rows, row0=row0, n_steps=n_steps),
        grid_spec=pltpu.PrefetchScalarGridSpec(
            num_scalar_prefetch=1,
            grid=(n_steps,),
            in_specs=[
                pl.BlockSpec((TR, d), lambda i, pos: (tile0 + i, 0)),
                pl.BlockSpec((TR, TOP_K), lambda i, pos: (tile0 + i, 0)),
                pl.BlockSpec((1, d), lambda i, pos: (0, 0)),
                pl.BlockSpec(memory_space=pl.ANY),
            ],
            out_specs=pl.BlockSpec((TR, d), lambda i, pos: (i, 0)),
            scratch_shapes=[pltpu.VMEM((2, TOP_K, TR // SUBLANES, GROUP_ROWS, d // 2), BF16),
                            pltpu.SemaphoreType.DMA((2,))],
        ),
        out_shape=jax.ShapeDtypeStruct((n_rows, d), F32),
        compiler_params=_params(1),
        name="combine",
    )(pos_flat, x1, wts_t, norm_f, y)


def _sample_headers(state, k_width):
    n, _, w = state.shape
    padded = jnp.pad(state, ((0, 0), (SUBLANES - (k_width - 1), 0), (0, 0)))
    return padded.reshape(n * SUBLANES, w)


def kernel(x_prompt, x_sample, state_conv_a, state_conv_r, state_h, meta_tokens, norm1, w_in, conv_a_w, conv_r_w,
           conv_r_b, lru_wa, lru_ba, lru_wx, lru_bx, lru_lam, w_br_a, w_br_r, w_o, norm2, w_group, b_group,
           w_router, b_router, w_gate, w_up, w_down, norm_f):
    batch, seq, d = x_prompt.shape
    dec_batch, dec_seq, _ = x_sample.shape
    depth = norm1.shape[0]
    width = w_br_a.shape[1]
    n_groups = w_group.shape[-1]
    n_experts = w_router.shape[-1]
    n_meta = meta_tokens.shape[0]
    assert depth == 1, "meta rows are folded into an initial state, which only carries one layer"
    assert dec_seq == SUBLANES and n_meta % SUBLANES == 0 and n_meta >= SUBLANES
    assert seq % TM == 0 and (dec_batch * dec_seq) % TM == 0
    assert n_groups <= SUBLANES and n_experts == n_groups * SUBLANES
    assert width % CB == 0 and d % (2 * CB) == 0 and CB % lru_wa.shape[-1] == 0

    n_prompt_rows = batch * seq
    n_sample_rows = dec_batch * dec_seq
    m_rows = n_prompt_rows + n_sample_rows
    n_prompt_tiles = n_prompt_rows // TM
    n_tiles = m_rows // TM
    tiles_per_seq = seq // TM
    n_exp_tiles = (TOP_K * m_rows) // T_EXP + n_experts
    assert n_exp_tiles <= LANES

    xp = x_prompt.reshape(n_prompt_rows, d)
    xs = x_sample.reshape(n_sample_rows, d)
    row = lambda v: v.reshape(1, -1)

    u, um = _norm1(xp, xs, meta_tokens, row(norm1[0]), n_prompt_tiles, n_tiles)

    def mixer_geometry(tm):
        assert seq % tm == 0 and n_sample_rows % tm == 0 and tm % MIX_CHUNK == 0 and MIX_CHUNK % SUBLANES == 0
        return dict(tm=tm, width=width, n_seq=batch, tiles_per_seq=seq // tm,
                    n_prompt_tiles=n_prompt_rows // tm, n_tiles=m_rows // tm)

    out_a, tail_a_p, tail_a_s = _mixer_a(
        u, um, w_in[0], conv_a_w[0], _sample_headers(state_conv_a[0], K_A), **mixer_geometry(TM_MIX_A))
    out_r, tail_r_p, tail_r_s, h_p, h_s = _mixer_r(
        u, um, w_in[0], conv_r_w[0], row(conv_r_b[0]), lru_wa[0], row(lru_ba[0]), lru_wx[0], row(lru_bx[0]),
        row(lru_lam[0]), _sample_headers(state_conv_r[0], K_R), state_h[0], seg0=3, **mixer_geometry(TM_MIX_R))

    gates = _gates(u, w_in[0], col0=5 * width, n_cols=2 * d, tm=TM_OPROJ)
    merged = _merge(gates, out_a, out_r, w_br_a[0], w_br_r[0], tm=TM_OPROJ)
    x1 = _oproj(merged, xp, xs, w_o[0], tm=TM_OPROJ)

    wt = jnp.concatenate([jnp.pad(w_group[0].T, ((0, SUBLANES - n_groups), (0, 0))), w_router[0].T], axis=0)
    bt = jnp.concatenate([jnp.pad(b_group[0], (0, SUBLANES - n_groups)), b_router[0]]).reshape(-1, 1)
    xn, eid, wts = _router(x1, row(norm2[0]), wt, bt, n_groups=n_groups)

    pos, info = _positions(eid, n_experts=n_experts)
    pos_flat = pos.reshape(-1)
    info_flat = info.reshape(-1)
    y = _grouped(_experts(info_flat, pos_flat, _grouped(xn), w_gate[0], w_up[0], w_down[0], n_tiles=n_exp_tiles))

    wts_t = wts.T
    g_f = row(norm_f)
    y_prompt = _combine(pos_flat, x1, wts_t, g_f, y, row0=0, n_rows=n_prompt_rows)
    y_sample = _combine(pos_flat, x1, wts_t, g_f, y, row0=n_prompt_rows, n_rows=n_sample_rows)

    def sample_tail(t, k):
        return t.reshape(dec_batch, SUBLANES, width)[:, SUBLANES - k:, :][None]

    return (
        y_prompt.reshape(batch, seq, d),
        y_sample.reshape(dec_batch, dec_seq, d),
        tail_a_p[:, SUBLANES - (K_A - 1):, :][None],
        tail_r_p[:, SUBLANES - (K_R - 1):, :][None],
        h_p[:, SUBLANES - 1, :][None],
        sample_tail(tail_a_s, K_A - 1),
        sample_tail(tail_r_s, K_R - 1),
        sample_tail(h_s, 1)[:, :, 0, :],
    )
```
